```python
import math
import jax
import jax.numpy as jnp
from jax import lax
import numpy as np

D_MODEL = 2048
BATCH = 4
SEQ = 2048
DEPTH = 4

HEAD_DIM = 128
NSA_HEADS = 8
NSA_KV_HEADS = 2
CMP_LEN = 32
CMP_STRIDE = 16
CMP_HIDDEN = 2 * HEAD_DIM
SEL_BLOCK = 64
SEL_TOP_N = 16
NSA_WINDOW = 512
NSA_Q_CHUNK = 64
SWA_HEADS = 8
SWA_KV_HEADS = 2
SWA_WINDOW = 128
MOBA_HEADS = 8
MOBA_KV_HEADS = 2
MOBA_BLOCK = 256
MOBA_TOP_K = 3
MOBA_Q_CHUNK = 16
BAND_BLOCK = 128
N_BUCKETS = 32
BUCKET_MAX_DIST = 128
TOTAL_HEADS = NSA_HEADS + SWA_HEADS + MOBA_HEADS
N_BRANCHES = 3
MIX_WIDTH = NSA_HEADS * HEAD_DIM
D_FF = 4 * D_MODEL
RMS_EPS = 1e-6
NEG_INF = -1e30
FORCE_SCORE = 1e30
TINY = 1e-30

SPLIT_SIZES = (
    NSA_HEADS * HEAD_DIM,
    NSA_KV_HEADS * HEAD_DIM,
    NSA_KV_HEADS * HEAD_DIM,
    NSA_KV_HEADS * HEAD_DIM,
    NSA_KV_HEADS * HEAD_DIM,
    NSA_KV_HEADS * HEAD_DIM,
    NSA_KV_HEADS * HEAD_DIM,
    NSA_HEADS * 3,
    SWA_HEADS * HEAD_DIM,
    SWA_KV_HEADS * HEAD_DIM,
    SWA_KV_HEADS * HEAD_DIM,
    MOBA_HEADS * HEAD_DIM,
    MOBA_KV_HEADS * HEAD_DIM,
    MOBA_KV_HEADS * HEAD_DIM,
    N_BRANCHES * D_MODEL,
)
SPLIT_POINTS = tuple(int(v) for v in np.cumsum(SPLIT_SIZES)[:-1])
D_IN = sum(SPLIT_SIZES)

kernel_name = 'hybrid_nsa_swa_moba_block'


def rms_norm(x, gain):
    x32 = x.astype(jnp.float32)
    y = x32 * lax.rsqrt(jnp.mean(x32 * x32, axis=-1, keepdims=True) + RMS_EPS)
    return (y * gain.astype(jnp.float32)).astype(x.dtype)


def t5_bucket(dist):
    n = jnp.maximum(dist, 0)
    exact = N_BUCKETS // 2
    log_ratio = jnp.log(jnp.maximum(n, 1).astype(jnp.float32) / exact) / math.log(BUCKET_MAX_DIST / exact)
    large = jnp.minimum(exact + (log_ratio * (N_BUCKETS - exact)).astype(jnp.int32), N_BUCKETS - 1)
    return jnp.where(n < exact, n, large)


def masked_softmax(logits, mask, sink=None):
    logits = jnp.where(mask, logits, NEG_INF)
    m = jnp.max(logits, axis=-1, keepdims=True)
    if sink is not None:
        m = jnp.maximum(m, sink)
    p = jnp.where(mask, jnp.exp(logits - m), 0.0)
    denom = jnp.sum(p, axis=-1, keepdims=True)
    if sink is not None:
        denom = denom + jnp.exp(sink - m)
    return p / jnp.maximum(denom, TINY)


def split_heads(t, n_heads):
    return t.reshape(t.shape[0], t.shape[1], n_heads, HEAD_DIM)


def banded_attention(q, k, v, bias_tbl, window, sinks=None):
    b, s, h, d = q.shape
    n_kv = k.shape[2]
    g = h // n_kv
    nb = s // BAND_BLOCK
    pad = -(-(window - 1) // BAND_BLOCK) * BAND_BLOCK
    span = BAND_BLOCK + pad
    key_idx = jnp.arange(nb)[:, None] * BAND_BLOCK + jnp.arange(span)[None, :]
    padding = ((0, 0), (pad, 0), (0, 0), (0, 0))
    kp = jnp.pad(k, padding)[:, key_idx]
    vp = jnp.pad(v, padding)[:, key_idx]
    qb = q.reshape(b, nb, BAND_BLOCK, n_kv, g, d)
    logits = jnp.einsum('bnqkgd,bnlkd->bnkgql', qb, kp, preferred_element_type=jnp.float32) * (d ** -0.5)
    dist = (jnp.arange(BAND_BLOCK)[:, None] + pad) - jnp.arange(span)[None, :]
    bias = jnp.transpose(bias_tbl[t5_bucket(dist)], (2, 0, 1)).reshape(n_kv, g, BAND_BLOCK, span)
    logits = logits + bias.astype(jnp.float32)
    mask = ((dist >= 0) & (dist < window))[None] & (key_idx >= pad)[:, None, :]
    mask = mask[None, :, None, None]
    sink = None
    if sinks is not None:
        sink = sinks.astype(jnp.float32).reshape(n_kv, g, 1, 1)
    p = masked_softmax(logits, mask, sink)
    out = jnp.einsum('bnkgql,bnlkd->bnqkgd', p.astype(v.dtype), vp)
    return out.reshape(b, s, h, d)


def nsa_compress(kv, pos_emb, w1, w2):
    b, s, n_kv, d = kv.shape
    n_cmp = (s - CMP_LEN) // CMP_STRIDE + 1
    idx = jnp.arange(n_cmp)[:, None] * CMP_STRIDE + jnp.arange(CMP_LEN)[None, :]
    blocks = kv[:, idx] + pos_emb[None, None, :, None, :]
    blocks = jnp.moveaxis(blocks, 3, 2).reshape(b, n_cmp, n_kv, CMP_LEN * d)
    return jax.nn.gelu(blocks @ w1) @ w2


def selection_weights(n_cmp, n_sel):
    starts = np.arange(n_cmp)[:, None] * CMP_STRIDE
    blk = np.arange(n_sel)[None, :] * SEL_BLOCK
    shared = np.clip(np.minimum(starts + CMP_LEN, blk + SEL_BLOCK) - np.maximum(starts, blk), 0, None)
    return (shared / CMP_STRIDE).astype(np.float32)


def nsa_selected_attention(qg, k_sel, v_sel, sel_idx, bias_tbl):
    b, s, n_kv, g, d = qg.shape
    n_top = sel_idx.shape[-1]
    n_sel = s // SEL_BLOCK
    n_keys = n_top * SEL_BLOCK
    scale = d ** -0.5
    kb = jnp.moveaxis(k_sel.reshape(b, n_sel, SEL_BLOCK, n_kv, d), 3, 1)
    vb = jnp.moveaxis(v_sel.reshape(b, n_sel, SEL_BLOCK, n_kv, d), 3, 1)
    tbl = jnp.moveaxis(bias_tbl.reshape(N_BUCKETS, n_kv, g), 0, 1)
    n_chunk = s // NSA_Q_CHUNK
    q_chunks = jnp.moveaxis(qg.reshape(b, n_chunk, NSA_Q_CHUNK, n_kv, g, d), 1, 0)
    idx_chunks = jnp.moveaxis(sel_idx.reshape(b, n_kv, n_chunk, NSA_Q_CHUNK, n_top), 2, 0)
    starts = jnp.arange(n_chunk) * NSA_Q_CHUNK
    bi = jnp.arange(b)[:, None, None, None]
    kvi = jnp.arange(n_kv)[None, :, None, None]
    pos_in_blk = jnp.arange(SEL_BLOCK)

    def chunk(xs):
        q_c, idx_c, start = xs
        t_c = start + jnp.arange(NSA_Q_CHUNK)
        k_c = kb[bi, kvi, idx_c]
        v_c = vb[bi, kvi, idx_c]
        dist = t_c[:, None, None] - (idx_c[..., None] * SEL_BLOCK + pos_in_blk)
        bias = jnp.moveaxis(tbl[kvi[..., None], t5_bucket(dist)], -1, 2)
        logits = jnp.einsum('bqkgd,bkqnld->bkgqnl', q_c, k_c, preferred_element_type=jnp.float32) * scale + bias
        mask = (dist >= 0)[:, :, None]
        p = masked_softmax(logits.reshape(b, n_kv, g, NSA_Q_CHUNK, n_keys),
                           mask.reshape(b, n_kv, 1, NSA_Q_CHUNK, n_keys))
        return jnp.einsum('bkgqm,bkqmd->bqkgd', p.astype(v_sel.dtype),
                          v_c.reshape(b, n_kv, NSA_Q_CHUNK, n_keys, d))

    out = lax.map(chunk, (q_chunks, idx_chunks, starts))
    return jnp.moveaxis(out, 0, 1).reshape(b, s, n_kv * g, d)


def nsa_attention(q, k_cmp, v_cmp, k_sel, v_sel, k_win, v_win, gate_logits, cmp_pos, cmp_w1, cmp_w2, bias_tbl):
    b, s, h, d = q.shape
    n_kv = k_cmp.shape[2]
    g = h // n_kv
    scale = d ** -0.5
    qg = q.reshape(b, s, n_kv, g, d)
    t = jnp.arange(s)
    kc = nsa_compress(k_cmp, cmp_pos[0], cmp_w1[0], cmp_w2[0])
    vc = nsa_compress(v_cmp, cmp_pos[1], cmp_w1[1], cmp_w2[1])
    n_cmp = kc.shape[1]
    dist_c = t[:, None] - (jnp.arange(n_cmp) * CMP_STRIDE + CMP_LEN - 1)[None, :]
    bias_c = jnp.moveaxis(bias_tbl[t5_bucket(dist_c)], -1, 0).reshape(n_kv, g, s, n_cmp)
    logits_c = jnp.einsum('bskgd,bckd->bkgsc', qg, kc, preferred_element_type=jnp.float32) * scale + bias_c
    p_cmp = masked_softmax(logits_c, dist_c >= 0)
    o_cmp = jnp.einsum('bkgsc,bckd->bskgd', p_cmp.astype(vc.dtype), vc).reshape(b, s, h, d)
    n_sel = s // SEL_BLOCK
    n_top = min(SEL_TOP_N, n_sel)
    share = jnp.asarray(selection_weights(n_cmp, n_sel))
    importance = jnp.einsum('bkgsc,cj->bksj', p_cmp, share)
    blk = jnp.arange(n_sel)[None, :]
    cur = (t // SEL_BLOCK)[:, None]
    forced = (blk == 0) | (blk == cur) | (blk == cur - 1)
    score = jnp.where(forced, FORCE_SCORE, jnp.where(blk <= cur, importance, NEG_INF))
    _, sel_idx = lax.top_k(score, n_top)
    o_sel = nsa_selected_attention(qg, k_sel, v_sel, sel_idx, bias_tbl)
    o_win = banded_attention(q, k_win, v_win, bias_tbl, NSA_WINDOW)
    gate = jax.nn.sigmoid(gate_logits.astype(jnp.float32)).reshape(b, s, h, 3).astype(q.dtype)
    out = gate[..., 0:1] * o_cmp + gate[..., 1:2] * o_sel + gate[..., 2:3] * o_win
    return out.reshape(b, s, h * d)


def moba_attention(q, k, v, bias_tbl):
    b, s, h, d = q.shape
    n_kv = k.shape[2]
    g = h // n_kv
    scale = d ** -0.5
    n_blk = -(-s // MOBA_BLOCK)
    padding = ((0, 0), (0, n_blk * MOBA_BLOCK - s), (0, 0), (0, 0))
    kb = jnp.moveaxis(jnp.pad(k, padding).reshape(b, n_blk, MOBA_BLOCK, n_kv, d), 3, 1)
    vb = jnp.moveaxis(jnp.pad(v, padding).reshape(b, n_blk, MOBA_BLOCK, n_kv, d), 3, 1)
    qg = q.reshape(b, s, n_kv, g, d)
    tbl = jnp.transpose(bias_tbl.reshape(N_BUCKETS, n_kv, g), (1, 2, 0))
    n_top = min(MOBA_TOP_K, n_blk - 1)
    n_chunk = s // MOBA_Q_CHUNK
    q_chunks = jnp.moveaxis(qg.reshape(b, n_chunk, MOBA_Q_CHUNK, n_kv, g, d), 1, 0)
    starts = jnp.arange(n_chunk) * MOBA_Q_CHUNK
    pos_in_blk = jnp.arange(MOBA_BLOCK)
    bi = jnp.arange(b)[:, None, None, None, None]
    kvi = jnp.arange(n_kv)[None, :, None, None, None]
    gi = jnp.arange(g)[None, None, :, None, None]

    def own_block(q_c, start):
        t_c = start + jnp.arange(MOBA_Q_CHUNK)
        blk = start // MOBA_BLOCK
        k_own = lax.dynamic_index_in_dim(kb, blk, axis=2, keepdims=False)
        v_own = lax.dynamic_index_in_dim(vb, blk, axis=2, keepdims=False)
        dist = t_c[:, None] - (blk * MOBA_BLOCK + pos_in_blk)[None, :]
        logits = jnp.einsum('bqkgd,bkld->bkgql', q_c, k_own, preferred_element_type=jnp.float32) * scale
        logits = logits + tbl[:, :, t5_bucket(dist)]
        mask = jnp.broadcast_to(dist >= 0, logits.shape)
        return logits, mask, v_own, t_c

    if n_top > 0:
        t = jnp.arange(s)
        q_blk = t // MOBA_BLOCK
        k_mean = jnp.mean(kb, axis=3)
        gate = jnp.einsum('bskgd,bknd->bkgsn', qg, k_mean, preferred_element_type=jnp.float32)
        gate = jnp.where(jnp.arange(n_blk)[None, :] < q_blk[:, None], gate, NEG_INF)
        _, sel_idx = lax.top_k(gate, n_top)
        valid = sel_idx < q_blk[:, None]
        idx_chunks = jnp.moveaxis(sel_idx.reshape(b, n_kv, g, n_chunk, MOBA_Q_CHUNK, n_top), 3, 0)
        valid_chunks = jnp.moveaxis(valid.reshape(b, n_kv, g, n_chunk, MOBA_Q_CHUNK, n_top), 3, 0)
        n_sel_keys = n_top * MOBA_BLOCK

        def chunk(xs):
            q_c, start, idx_c, valid_c = xs
            logits_own, mask_own, v_own, t_c = own_block(q_c, start)
            k_sel = kb[bi, kvi, idx_c]
            v_sel = vb[bi, kvi, idx_c]
            dist = t_c[:, None, None] - (idx_c[..., None] * MOBA_BLOCK + pos_in_blk)
            bias = tbl[kvi[..., None], gi[..., None], t5_bucket(dist)]
            logits_sel = jnp.einsum('bqkgd,bkgqnld->bkgqnl', q_c, k_sel, preferred_element_type=jnp.float32) * scale + bias
            logits_sel = logits_sel.reshape(b, n_kv, g, MOBA_Q_CHUNK, n_sel_keys)
            mask_sel = jnp.broadcast_to(valid_c[..., None], dist.shape).reshape(logits_sel.shape)
            p = masked_softmax(jnp.concatenate([logits_sel, logits_own], axis=-1),
                               jnp.concatenate([mask_sel, mask_own], axis=-1)).astype(v.dtype)
            o_sel = jnp.einsum('bkgqm,bkgqmd->bqkgd', p[..., :n_sel_keys],
                               v_sel.reshape(b, n_kv, g, MOBA_Q_CHUNK, n_sel_keys, d))
            o_own = jnp.einsum('bkgql,bkld->bqkgd', p[..., n_sel_keys:], v_own)
            return o_sel + o_own

        out = lax.map(chunk, (q_chunks, starts, idx_chunks, valid_chunks))
    else:
        def chunk(xs):
            q_c, start = xs
            logits_own, mask_own, v_own, _ = own_block(q_c, start)
            p = masked_softmax(logits_own, mask_own).astype(v.dtype)
            return jnp.einsum('bkgql,bkld->bqkgd', p, v_own)

        out = lax.map(chunk, (q_chunks, starts))
    return jnp.moveaxis(out, 0, 1).reshape(b, s, h, d)


def setup_inputs(seed: int = 0) -> dict:
    key = jax.random.key(seed)
    ks = jax.random.split(key, 14)

    def nrm(k, shape, scale):
        return jax.random.normal(k, shape, jnp.float32) * scale

    out_scale = (2 * DEPTH) ** -0.5
    return {
        'x': nrm(ks[0], (BATCH, SEQ, D_MODEL), 1.0),
        'w_in': nrm(ks[1], (DEPTH, D_MODEL, D_IN), D_MODEL ** -0.5),
        'cmp_pos': nrm(ks[2], (DEPTH, 2, CMP_LEN, HEAD_DIM), 0.1),
        'cmp_w1': nrm(ks[3], (DEPTH, 2, CMP_LEN * HEAD_DIM, CMP_HIDDEN), (CMP_LEN * HEAD_DIM) ** -0.5),
        'cmp_w2': nrm(ks[4], (DEPTH, 2, CMP_HIDDEN, HEAD_DIM), CMP_HIDDEN ** -0.5),
        'swa_sinks': nrm(ks[5], (DEPTH, SWA_HEADS), 1.0),
        'w_branch': nrm(ks[6], (DEPTH, N_BRANCHES, MIX_WIDTH, D_MODEL), MIX_WIDTH ** -0.5),
        'w_out': nrm(ks[7], (DEPTH, D_MODEL, D_MODEL), D_MODEL ** -0.5 * out_scale),
        'w_mlp_in': nrm(ks[8], (DEPTH, D_MODEL, D_FF), D_MODEL ** -0.5),
        'w_mlp_out': nrm(ks[9], (DEPTH, D_FF, D_MODEL), D_FF ** -0.5 * out_scale),
        'norm_mix': 1.0 + nrm(ks[10], (DEPTH, D_MODEL), 0.02),
        'norm_mlp': 1.0 + nrm(ks[11], (DEPTH, D_MODEL), 0.02),
        'norm_final': 1.0 + nrm(ks[12], (D_MODEL,), 0.02),
        'rel_bias': nrm(ks[13], (N_BUCKETS, TOTAL_HEADS), 0.5),
    }


def reference(x, w_in, cmp_pos, cmp_w1, cmp_w2, swa_sinks, w_branch, w_out, w_mlp_in, w_mlp_out,
              norm_mix, norm_mlp, norm_final, rel_bias):
    b, s, _ = x.shape
    bias_a = rel_bias[:, :NSA_HEADS]
    bias_b = rel_bias[:, NSA_HEADS:NSA_HEADS + SWA_HEADS]
    bias_c = rel_bias[:, NSA_HEADS + SWA_HEADS:]
    for layer in range(DEPTH):
        h = rms_norm(x, norm_mix[layer])
        (a_q, a_kc, a_vc, a_ks, a_vs, a_kw, a_vw, a_gate,
         b_q, b_k, b_v, c_q, c_k, c_v, merge_gate) = jnp.split(h @ w_in[layer], SPLIT_POINTS, axis=-1)
        o_a = nsa_attention(split_heads(a_q, NSA_HEADS),
                            split_heads(a_kc, NSA_KV_HEADS), split_heads(a_vc, NSA_KV_HEADS),
                            split_heads(a_ks, NSA_KV_HEADS), split_heads(a_vs, NSA_KV_HEADS),
                            split_heads(a_kw, NSA_KV_HEADS), split_heads(a_vw, NSA_KV_HEADS),
                            a_gate, cmp_pos[layer], cmp_w1[layer], cmp_w2[layer], bias_a)
        o_b = banded_attention(split_heads(b_q, SWA_HEADS), split_heads(b_k, SWA_KV_HEADS),
                               split_heads(b_v, SWA_KV_HEADS), bias_b, SWA_WINDOW,
                               swa_sinks[layer]).reshape(b, s, MIX_WIDTH)
        o_c = moba_attention(split_heads(c_q, MOBA_HEADS), split_heads(c_k, MOBA_KV_HEADS),
                             split_heads(c_v, MOBA_KV_HEADS), bias_c).reshape(b, s, MIX_WIDTH)
        branches = jnp.stack([o_a, o_b, o_c], axis=2)
        y = jnp.einsum('bsmc,mcd->bsmd', branches, w_branch[layer])
        gate = jax.nn.sigmoid(merge_gate.reshape(b, s, N_BRANCHES, D_MODEL))
        x = x + jnp.sum(gate * y, axis=2) @ w_out[layer]
        h = rms_norm(x, norm_mlp[layer])
        x = x + jnp.square(jax.nn.relu(h @ w_mlp_in[layer])) @ w_mlp_out[layer]
    return rms_norm(x, norm_final)
```

```python
import functools
import math

import numpy as np
import jax
import jax.numpy as jnp
from jax import lax
from jax.experimental import pallas as pl
from jax.experimental.pallas import tpu as pltpu

F32 = jnp.float32
BF16 = jnp.bfloat16

D_MODEL = 2048
HEAD_DIM = 128
N_HEADS = 8
N_KV = 2
GROUP = N_HEADS // N_KV
MIX_WIDTH = N_HEADS * HEAD_DIM
CMP_LEN = 32
CMP_STRIDE = 16
CMP_HIDDEN = 2 * HEAD_DIM
SEL_BLOCK = 64
SEL_TOP_N = 16
NSA_WINDOW = 512
SWA_WINDOW = 128
MOBA_BLOCK = 256
MOBA_TOP_K = 3
N_BUCKETS = 32
BUCKET_EXACT = N_BUCKETS // 2
BUCKET_MAX_DIST = 128
TOTAL_HEADS = 3 * N_HEADS
D_FF = 4 * D_MODEL
RMS_EPS = 1e-6
NEG_INF = -1e30
FORCE_SCORE = 1e30
TINY = 1e-30
BELOW_ALL = -3e38
SCALE = HEAD_DIM ** -0.5

LANES = 128
VMEM_LIMIT = 56 * 1024 * 1024
TQ = 128
TK = 128
ROWS = GROUP * TQ

COL_A_Q = 0
COL_A_KC = 1024
COL_A_KS = 1536
COL_A_VS = 1792
COL_A_KW = 2048
COL_A_VW = 2304
COL_B_Q = 2560
COL_B_K = 3584
COL_B_V = 3840
COL_C_Q = 4096
COL_C_K = 5120
COL_C_V = 5376
COL_A_GATE = 5632
COL_MERGE = 6144
N_PROJ = COL_MERGE + 3 * D_MODEL
ORIG_QKV_A = 2560
ORIG_GATE_W = 3 * N_HEADS


def _cparams(semantics):
    return pltpu.CompilerParams(dimension_semantics=semantics, vmem_limit_bytes=VMEM_LIMIT)


def _rms(x, gain):
    y = x * lax.rsqrt(jnp.mean(x * x, axis=-1, keepdims=True) + RMS_EPS)
    return y * gain


def _norm_matmul_kernel(x_ref, g_ref, w_ref, o_ref, h_scr):
    @pl.when(pl.program_id(1) == 0)
    def _():
        h_scr[...] = _rms(x_ref[...], g_ref[...]).astype(BF16)

    o_ref[...] = jnp.dot(h_scr[...], w_ref[...], preferred_element_type=F32)


def norm_matmul(x, gain, w, *, tm=1024, tn=512):
    t, k = x.shape
    n = w.shape[1]
    return pl.pallas_call(
        _norm_matmul_kernel,
        out_shape=jax.ShapeDtypeStruct((t, n), F32),
        grid=(t // tm, n // tn),
        in_specs=[pl.BlockSpec((tm, k), lambda i, j: (i, 0)),
                  pl.BlockSpec((1, k), lambda i, j: (0, 0)),
                  pl.BlockSpec((k, tn), lambda i, j: (0, j))],
        out_specs=pl.BlockSpec((tm, tn), lambda i, j: (i, j)),
        scratch_shapes=[pltpu.VMEM((tm, k), BF16)],
        compiler_params=_cparams(("parallel", "arbitrary")),
        name="norm_matmul",
    )(x, gain.reshape(1, k), w)


def _merge_kernel(oa_ref, ob_ref, oc_ref, wb_ref, g0_ref, g1_ref, g2_ref, z_ref):
    acc = jax.nn.sigmoid(g0_ref[...]) * jnp.dot(oa_ref[...], wb_ref[0], preferred_element_type=F32)
    acc += jax.nn.sigmoid(g1_ref[...]) * jnp.dot(ob_ref[...], wb_ref[1], preferred_element_type=F32)
    acc += jax.nn.sigmoid(g2_ref[...]) * jnp.dot(oc_ref[...], wb_ref[2], preferred_element_type=F32)
    z_ref[...] = acc.astype(BF16)


def merge_branches(o_a, o_b, o_c, w_branch, proj, *, tm=512, tn=512):
    t = o_a.shape[0]
    gate_blk = COL_MERGE // tn
    per_branch = D_MODEL // tn
    o_spec = pl.BlockSpec((tm, MIX_WIDTH), lambda i, j: (i, 0))

    def gate_spec(m):
        return pl.BlockSpec((tm, tn), lambda i, j: (i, gate_blk + m * per_branch + j))

    return pl.pallas_call(
        _merge_kernel,
        out_shape=jax.ShapeDtypeStruct((t, D_MODEL), BF16),
        grid=(t // tm, D_MODEL // tn),
        in_specs=[o_spec, o_spec, o_spec,
                  pl.BlockSpec((3, MIX_WIDTH, tn), lambda i, j: (0, 0, j)),
                  gate_spec(0), gate_spec(1), gate_spec(2)],
        out_specs=pl.BlockSpec((tm, tn), lambda i, j: (i, j)),
        compiler_params=_cparams(("parallel", "parallel")),
        name="merge_branches",
    )(o_a, o_b, o_c, w_branch, proj, proj, proj)


def _matmul_res_kernel(a_ref, w_ref, x_ref, o_ref):
    o_ref[...] = x_ref[...] + jnp.dot(a_ref[...], w_ref[...], preferred_element_type=F32)


def matmul_residual(a, w, x, *, tm=1024, tn=512):
    t, k = a.shape
    n = w.shape[1]
    return pl.pallas_call(
        _matmul_res_kernel,
        out_shape=jax.ShapeDtypeStruct((t, n), F32),
        grid=(t // tm, n // tn),
        in_specs=[pl.BlockSpec((tm, k), lambda i, j: (i, 0)),
                  pl.BlockSpec((k, tn), lambda i, j: (0, j)),
                  pl.BlockSpec((tm, tn), lambda i, j: (i, j))],
        out_specs=pl.BlockSpec((tm, tn), lambda i, j: (i, j)),
        compiler_params=_cparams(("parallel", "parallel")),
        name="matmul_residual",
    )(a, w, x)


def _mlp_kernel(x_ref, g_ref, w1_ref, w2_ref, o_ref, h_scr, acc_scr):
    f = pl.program_id(1)

    @pl.when(f == 0)
    def _():
        h_scr[...] = _rms(x_ref[...], g_ref[...]).astype(BF16)
        acc_scr[...] = jnp.zeros_like(acc_scr)

    u = jnp.dot(h_scr[...], w1_ref[...], preferred_element_type=F32)
    u = jnp.square(jnp.maximum(u, 0.0)).astype(BF16)
    acc_scr[...] += jnp.dot(u, w2_ref[...], preferred_element_type=F32)

    @pl.when(f == pl.num_programs(1) - 1)
    def _():
        o_ref[...] = x_ref[...] + acc_scr[...]


def mlp_block(x, gain, w1, w2, *, tm=512, tf=512):
    t, d = x.shape
    dff = w1.shape[1]
    return pl.pallas_call(
        _mlp_kernel,
        out_shape=jax.ShapeDtypeStruct((t, d), F32),
        grid=(t // tm, dff // tf),
        in_specs=[pl.BlockSpec((tm, d), lambda i, f: (i, 0)),
                  pl.BlockSpec((1, d), lambda i, f: (0, 0)),
                  pl.BlockSpec((d, tf), lambda i, f: (0, f)),
                  pl.BlockSpec((tf, d), lambda i, f: (f, 0))],
        out_specs=pl.BlockSpec((tm, d), lambda i, f: (i, 0)),
        scratch_shapes=[pltpu.VMEM((tm, d), BF16), pltpu.VMEM((tm, d), F32)],
        compiler_params=_cparams(("parallel", "arbitrary")),
        name="mlp_block",
    )(x, gain.reshape(1, d), w1, w2)


def _final_norm_kernel(x_ref, g_ref, o_ref):
    o_ref[...] = _rms(x_ref[...], g_ref[...])


def final_norm(x, gain, *, tm=512):
    t, d = x.shape
    return pl.pallas_call(
        _final_norm_kernel,
        out_shape=jax.ShapeDtypeStruct((t, d), F32),
        grid=(t // tm,),
        in_specs=[pl.BlockSpec((tm, d), lambda i: (i, 0)),
                  pl.BlockSpec((1, d), lambda i: (0, 0))],
        out_specs=pl.BlockSpec((tm, d), lambda i: (i, 0)),
        compiler_params=_cparams(("parallel",)),
        name="final_norm",
    )(x, gain.reshape(1, d))


def _t5_bucket(n):
    log_ratio = jnp.log(jnp.maximum(n, 1).astype(F32) / BUCKET_EXACT) / math.log(BUCKET_MAX_DIST / BUCKET_EXACT)
    large = jnp.minimum(BUCKET_EXACT + (log_ratio * (N_BUCKETS - BUCKET_EXACT)).astype(jnp.int32), N_BUCKETS - 1)
    return jnp.where(n < BUCKET_EXACT, n, large)


def _lookup_bias(tbl_ref, head, n):
    bucket = _t5_bucket(n)
    out = jnp.zeros(n.shape, F32)
    for b in range(N_BUCKETS):
        out = jnp.where(bucket == b, tbl_ref[b * TOTAL_HEADS + head], out)
    return out


def _band_bias_kernel(tbl_ref, o_ref):
    q = lax.broadcasted_iota(jnp.int32, (TQ, 2 * TK), 0)
    l = lax.broadcasted_iota(jnp.int32, (TQ, 2 * TK), 1)
    o_ref[0] = _lookup_bias(tbl_ref, pl.program_id(0), jnp.maximum(q + TK - l, 0))


def band_bias(tbl_flat):
    return pl.pallas_call(
        _band_bias_kernel,
        out_shape=jax.ShapeDtypeStruct((TOTAL_HEADS, TQ, 2 * TK), F32),
        grid=(TOTAL_HEADS,),
        in_specs=[pl.BlockSpec(memory_space=pltpu.SMEM)],
        out_specs=pl.BlockSpec((1, TQ, 2 * TK), lambda h: (h, 0, 0)),
        compiler_params=_cparams(("parallel",)),
        name="band_bias",
    )(tbl_flat)


def _cmp_bias_kernel(tbl_ref, o_ref):
    t = pl.program_id(1) * TQ + lax.broadcasted_iota(jnp.int32, (TQ, LANES), 0)
    c = lax.broadcasted_iota(jnp.int32, (TQ, LANES), 1)
    o_ref[0] = _lookup_bias(tbl_ref, pl.program_id(0), jnp.maximum(t - (c * CMP_STRIDE + CMP_LEN - 1), 0))


def cmp_bias(tbl_flat, s):
    return pl.pallas_call(
        _cmp_bias_kernel,
        out_shape=jax.ShapeDtypeStruct((N_HEADS, s, LANES), F32),
        grid=(N_HEADS, s // TQ),
        in_specs=[pl.BlockSpec(memory_space=pltpu.SMEM)],
        out_specs=pl.BlockSpec((1, TQ, LANES), lambda h, i: (h, i, 0)),
        compiler_params=_cparams(("parallel", "parallel")),
        name="cmp_bias",
    )(tbl_flat)


def _compress_kernel(half_ref, pos_ref, w1_ref, w2_ref, o_ref):
    half = half_ref[0, 0, 0]
    half_w = CMP_STRIDE * HEAD_DIM
    xa = (half + pos_ref[0, 0:1, :]).astype(BF16)
    xb = (half + pos_ref[0, 1:2, :]).astype(BF16)
    ha = jnp.dot(xa, w1_ref[0, 0:half_w, :], preferred_element_type=F32)
    hb = jnp.dot(xb, w1_ref[0, half_w:2 * half_w, :], preferred_element_type=F32)
    n_half = hb.shape[0]
    hidden = ha + pltpu.roll(hb, n_half - 1, 0)
    act = jax.nn.gelu(hidden, approximate=True).astype(BF16)
    o_ref[0, 0, 0] = jnp.dot(act, w2_ref[0], preferred_element_type=F32)


def nsa_compress(halves, pos, w1, w2):
    b, _, n_kv, n_half, width = halves.shape
    return pl.pallas_call(
        _compress_kernel,
        out_shape=jax.ShapeDtypeStruct((b, 2, n_kv, n_half, HEAD_DIM), F32),
        grid=(2, b, n_kv),
        in_specs=[pl.BlockSpec((1, 1, 1, n_half, width), lambda kv, bi, h: (bi, kv, h, 0, 0)),
                  pl.BlockSpec((1, 2, width), lambda kv, bi, h: (kv, 0, 0)),
                  pl.BlockSpec((1, 2 * width, CMP_HIDDEN), lambda kv, bi, h: (kv, 0, 0)),
                  pl.BlockSpec((1, CMP_HIDDEN, HEAD_DIM), lambda kv, bi, h: (kv, 0, 0))],
        out_specs=pl.BlockSpec((1, 1, 1, n_half, HEAD_DIM), lambda kv, bi, h: (bi, kv, h, 0, 0)),
        compiler_params=_cparams(("parallel", "parallel", "parallel")),
        name="nsa_compress",
    )(halves, pos, w1, w2)


def _stack_heads(q):
    return jnp.concatenate([q[:, g * HEAD_DIM:(g + 1) * HEAD_DIM] for g in range(GROUP)], axis=0)


def _unstack_heads(o):
    return jnp.concatenate([o[g * TQ:(g + 1) * TQ, :] for g in range(GROUP)], axis=1)


def _split3(x):
    x1 = x.astype(BF16)
    r1 = x - x1.astype(F32)
    x2 = r1.astype(BF16)
    x3 = (r1 - x2.astype(F32)).astype(BF16)
    return x1, x2, x3


def _dot_nt(a, b):
    return lax.dot_general(a, b, (((1,), (1,)), ((), ())), preferred_element_type=F32)


def _rank_select(score, n_cand, n_top):
    lane = lax.broadcasted_iota(jnp.int32, score.shape, 1)
    rank = jnp.zeros(score.shape, jnp.int32)
    for j in range(n_cand):
        col = score[:, j:j + 1]
        ahead = (col > score) | ((col == score) & (j < lane))
        rank += jnp.where(ahead, 1, 0)
    picked = (rank < n_top) & (lane < n_cand)
    return jnp.where(picked, 1.0, 0.0)


def _nsa_cmp_kernel(bias_ref, q_ref, kc_ref, vc_ref, share_ref, o_ref, sel_ref):
    qi = pl.program_id(2)
    q4 = _stack_heads(q_ref[0]).astype(BF16)
    kc = kc_ref[0, 0, 0].astype(BF16)
    vc = vc_ref[0, 0, 0].astype(BF16)
    logits = _dot_nt(q4, kc) * SCALE + bias_ref[...].reshape(ROWS, LANES)
    row = lax.broadcasted_iota(jnp.int32, (ROWS, LANES), 0) & (TQ - 1)
    c = lax.broadcasted_iota(jnp.int32, (ROWS, LANES), 1)
    valid = (qi * TQ + row) - (c * CMP_STRIDE + CMP_LEN - 1) >= 0
    logits = jnp.where(valid, logits, NEG_INF)
    m = jnp.max(logits, axis=1, keepdims=True)
    p = jnp.where(valid, jnp.exp(logits - m), 0.0)
    p = p / jnp.maximum(jnp.sum(p, axis=1, keepdims=True), TINY)
    o_ref[0] = _unstack_heads(jnp.dot(p.astype(BF16), vc, preferred_element_type=F32))

    p_sum = p[0:TQ]
    for g in range(1, GROUP):
        p_sum = p_sum + p[g * TQ:(g + 1) * TQ]
    share = share_ref[...]
    importance = sum(jnp.dot(part, share, preferred_element_type=F32) for part in _split3(p_sum))
    t = qi * TQ + lax.broadcasted_iota(jnp.int32, (TQ, LANES), 0)
    blk = lax.broadcasted_iota(jnp.int32, (TQ, LANES), 1)
    cur = t // SEL_BLOCK
    forced = (blk == 0) | (blk == cur) | (blk == cur - 1)
    score = jnp.where(forced, FORCE_SCORE, jnp.where(blk <= cur, importance, NEG_INF))
    n_sel = share_ref.shape[0] * CMP_STRIDE // SEL_BLOCK
    score = jnp.where(blk < n_sel, score, BELOW_ALL)
    sel_ref[0, 0] = _rank_select(score, n_sel, min(SEL_TOP_N, n_sel)).astype(BF16)


def nsa_compressed(bias_c, proj, cmp_tokens, share):
    b, s, _ = proj.shape
    n_cmp = cmp_tokens.shape[3]
    q_blk = COL_A_Q // (GROUP * HEAD_DIM)
    return pl.pallas_call(
        _nsa_cmp_kernel,
        out_shape=(jax.ShapeDtypeStruct((b, s, MIX_WIDTH), F32),
                   jax.ShapeDtypeStruct((b, N_KV, s, LANES), BF16)),
        grid=(b, N_KV, s // TQ),
        in_specs=[pl.BlockSpec((GROUP, TQ, LANES), lambda bi, h, i: (h, i, 0)),
                  pl.BlockSpec((1, TQ, GROUP * HEAD_DIM), lambda bi, h, i: (bi, i, q_blk + h)),
                  pl.BlockSpec((1, 1, 1, n_cmp, HEAD_DIM), lambda bi, h, i: (bi, 0, h, 0, 0)),
                  pl.BlockSpec((1, 1, 1, n_cmp, HEAD_DIM), lambda bi, h, i: (bi, 1, h, 0, 0)),
                  pl.BlockSpec((n_cmp, LANES), lambda bi, h, i: (0, 0))],
        out_specs=(pl.BlockSpec((1, TQ, GROUP * HEAD_DIM), lambda bi, h, i: (bi, i, h)),
                   pl.BlockSpec((1, 1, TQ, LANES), lambda bi, h, i: (bi, h, i, 0))),
        compiler_params=_cparams(("parallel", "parallel", "parallel")),
        name="nsa_compressed",
    )(bias_c, proj, cmp_tokens, cmp_tokens, share)


def _moba_gate_kernel(q_ref, k_ref, sel_ref, km_scr):
    qi = pl.program_id(2)
    n_blk = k_ref.shape[1] // MOBA_BLOCK

    @pl.when(qi == 0)
    def _():
        km_scr[...] = jnp.zeros_like(km_scr)
        for n in range(n_blk):
            blk = k_ref[0, n * MOBA_BLOCK:(n + 1) * MOBA_BLOCK, :]
            km_scr[n:n + 1, :] = jnp.sum(blk, axis=0, keepdims=True) / MOBA_BLOCK

    q4 = _stack_heads(q_ref[0])
    q_hi, q_lo, _ = _split3(q4)
    k_hi, k_lo, _ = _split3(km_scr[...])
    gate = _dot_nt(q_hi, k_hi) + _dot_nt(q_hi, k_lo) + _dot_nt(q_lo, k_hi)
    n = lax.broadcasted_iota(jnp.int32, (ROWS, LANES), 1)
    q_blk = (qi * TQ) // MOBA_BLOCK
    gate = jnp.where(n < q_blk, gate, NEG_INF)
    gate = jnp.where(n < n_blk, gate, BELOW_ALL)
    picked = jnp.where(n < q_blk, _rank_select(gate, n_blk, min(MOBA_TOP_K, n_blk - 1)), 0.0)
    sel_ref[0, 0] = picked.astype(BF16).reshape(GROUP, TQ, LANES)


def moba_gate(proj):
    b, s, _ = proj.shape
    q_blk = COL_C_Q // (GROUP * HEAD_DIM)
    k_blk = COL_C_K // HEAD_DIM
    return pl.pallas_call(
        _moba_gate_kernel,
        out_shape=jax.ShapeDtypeStruct((b, N_KV, GROUP, s, LANES), BF16),
        grid=(b, N_KV, s // TQ),
        in_specs=[pl.BlockSpec((1, TQ, GROUP * HEAD_DIM), lambda bi, h, i: (bi, i, q_blk + h)),
                  pl.BlockSpec((1, s, HEAD_DIM), lambda bi, h, i: (bi, 0, k_blk + h))],
        out_specs=pl.BlockSpec((1, 1, GROUP, TQ, LANES), lambda bi, h, i: (bi, h, 0, i, 0)),
        scratch_shapes=[pltpu.VMEM((LANES, HEAD_DIM), F32)],
        compiler_params=_cparams(("parallel", "parallel", "arbitrary")),
        name="moba_gate",
    )(proj, proj)


def _flash_kernel(mode, *refs):
    if mode == "swa":
        c31_ref, sink_ref, band_ref, q_ref, k_ref, v_ref, o_ref, m_scr, l_scr, acc_scr = refs
    elif mode == "win":
        c31_ref, band_ref, q_ref, k_ref, v_ref, o_ref, m_scr, l_scr, acc_scr = refs
    else:
        c31_ref, band_ref, q_ref, k_ref, v_ref, msk_ref, exp_ref, o_ref, m_scr, l_scr, acc_scr = refs
    kvh = pl.program_id(1)
    qi = pl.program_id(2)
    head0 = {"sel": 0, "win": 0, "swa": N_HEADS, "moba": 2 * N_HEADS}[mode] + kvh * GROUP

    q4 = _stack_heads(q_ref[0]).astype(BF16)
    m_scr[...] = jnp.full_like(m_scr, NEG_INF)
    l_scr[...] = jnp.zeros_like(l_scr)
    acc_scr[...] = jnp.zeros_like(acc_scr)

    row = lax.broadcasted_iota(jnp.int32, (ROWS, TK), 0) & (TQ - 1)
    col = lax.broadcasted_iota(jnp.int32, (ROWS, TK), 1)
    far_bias = jnp.concatenate([jnp.full((TQ, 1), c31_ref[head0 + g], F32) for g in range(GROUP)], axis=0)

    if mode == "sel":
        msk = msk_ref[0, 0]
    elif mode == "moba":
        msk = msk_ref[0, 0].reshape(ROWS, LANES)

    def block_mask(j, always=0.0):
        e = jnp.dot(msk, exp_ref[j], preferred_element_type=F32)
        if mode == "sel":
            e = jnp.concatenate([e] * GROUP, axis=0)
        return e + always > 0.5

    def step(j, bias, valid):
        start = pl.multiple_of(j * TK, TK)
        kb = k_ref[0, pl.ds(start, TK), :].astype(BF16)
        vb = v_ref[0, pl.ds(start, TK), :].astype(BF16)
        s = _dot_nt(q4, kb) * SCALE + bias
        if valid is not None:
            s = jnp.where(valid, s, NEG_INF)
        m_prev = m_scr[...]
        m_new = jnp.maximum(m_prev, jnp.max(s, axis=1, keepdims=True))
        alpha = jnp.exp(m_prev - m_new)
        p = jnp.exp(s - m_new)
        if valid is not None:
            p = jnp.where(valid, p, 0.0)
        l_scr[...] = alpha * l_scr[...] + jnp.sum(p, axis=1, keepdims=True)
        acc_scr[...] = alpha * acc_scr[...] + jnp.dot(p.astype(BF16), vb, preferred_element_type=F32)
        m_scr[...] = m_new

    if mode in ("sel", "moba"):
        def far_body(j, carry):
            step(j, far_bias, block_mask(j))
            return carry
        lax.fori_loop(0, jnp.maximum(qi - 1, 0), far_body, 0)
    elif mode == "win":
        n_full = NSA_WINDOW // TK

        @pl.when(qi >= n_full)
        def _():
            step(qi - n_full, far_bias, col > row)

        def far_body(j, carry):
            step(j, far_bias, None)
            return carry
        lax.fori_loop(jnp.maximum(qi - n_full + 1, 0), jnp.maximum(qi - 1, 0), far_body, 0)

    @pl.when(qi >= 1)
    def _():
        bias = band_ref[:, :, 0:TK].reshape(ROWS, TK)
        if mode == "sel":
            valid = block_mask(qi - 1)
        elif mode == "moba":
            own = (qi * TQ) // MOBA_BLOCK == ((qi - 1) * TK) // MOBA_BLOCK
            valid = block_mask(qi - 1, jnp.where(own, 1.0, 0.0))
        elif mode == "swa":
            valid = col > row
        else:
            valid = None
        step(qi - 1, bias, valid)

    bias = band_ref[:, :, TK:2 * TK].reshape(ROWS, TK)
    valid = col <= row
    if mode == "sel":
        valid = valid & block_mask(qi)
    step(qi, bias, valid)

    m = m_scr[...]
    l = l_scr[...]
    acc = acc_scr[...]
    if mode == "swa":
        sink = jnp.concatenate([jnp.full((TQ, 1), sink_ref[kvh * GROUP + g], F32) for g in range(GROUP)], axis=0)
        m_fin = jnp.maximum(m, sink)
        shrink = jnp.exp(m - m_fin)
        l = l * shrink + jnp.exp(sink - m_fin)
        acc = acc * shrink
    out = acc / jnp.maximum(l, TINY)
    o_ref[0] = _unstack_heads(out).astype(o_ref.dtype)


def flash_attention(mode, proj, band, c31, *, q_col, k_col, v_col, out_dtype, mask=None, expand=None, sinks=None):
    b, s, _ = proj.shape
    q_blk = q_col // (GROUP * HEAD_DIM)
    k_blk = k_col // HEAD_DIM
    v_blk = v_col // HEAD_DIM
    head_blk = {"sel": 0, "win": 0, "swa": N_HEADS, "moba": 2 * N_HEADS}[mode] // GROUP
    smem = pl.BlockSpec(memory_space=pltpu.SMEM)
    in_specs = [smem]
    args = [c31]
    if mode == "swa":
        in_specs.append(smem)
        args.append(sinks)
    in_specs += [pl.BlockSpec((GROUP, TQ, 2 * TK), lambda bi, h, i: (head_blk + h, 0, 0)),
                 pl.BlockSpec((1, TQ, GROUP * HEAD_DIM), lambda bi, h, i: (bi, i, q_blk + h)),
                 pl.BlockSpec((1, s, HEAD_DIM), lambda bi, h, i: (bi, 0, k_blk + h)),
                 pl.BlockSpec((1, s, HEAD_DIM), lambda bi, h, i: (bi, 0, v_blk + h))]
    args += [band, proj, proj, proj]
    if mode == "sel":
        in_specs += [pl.BlockSpec((1, 1, TQ, LANES), lambda bi, h, i: (bi, h, i, 0)),
                     pl.BlockSpec(expand.shape, lambda bi, h, i: (0, 0, 0))]
        args += [mask, expand]
    elif mode == "moba":
        in_specs += [pl.BlockSpec((1, 1, GROUP, TQ, LANES), lambda bi, h, i: (bi, h, 0, i, 0)),
                     pl.BlockSpec(expand.shape, lambda bi, h, i: (0, 0, 0))]
        args += [mask, expand]
    return pl.pallas_call(
        functools.partial(_flash_kernel, mode),
        out_shape=jax.ShapeDtypeStruct((b, s, MIX_WIDTH), out_dtype),
        grid=(b, N_KV, s // TQ),
        in_specs=in_specs,
        out_specs=pl.BlockSpec((1, TQ, GROUP * HEAD_DIM), lambda bi, h, i: (bi, i, h)),
        scratch_shapes=[pltpu.VMEM((ROWS, 1), F32), pltpu.VMEM((ROWS, 1), F32), pltpu.VMEM((ROWS, HEAD_DIM), F32)],
        compiler_params=_cparams(("parallel", "parallel", "parallel")),
        name="flash_" + mode,
    )(*args)


def _nsa_combine_kernel(cmp_ref, sel_ref, win_ref, gate_ref, o_ref):
    gate = jax.nn.sigmoid(gate_ref[...])
    for h in range(N_HEADS):
        cols = slice(h * HEAD_DIM, (h + 1) * HEAD_DIM)
        out = (gate[:, 3 * h:3 * h + 1] * cmp_ref[:, cols]
               + gate[:, 3 * h + 1:3 * h + 2] * sel_ref[:, cols]
               + gate[:, 3 * h + 2:3 * h + 3] * win_ref[:, cols])
        o_ref[:, cols] = out.astype(BF16)


def nsa_combine(o_cmp, o_sel, o_win, proj, *, tm=512):
    t = o_cmp.shape[0]
    o_spec = pl.BlockSpec((tm, MIX_WIDTH), lambda i: (i, 0))
    return pl.pallas_call(
        _nsa_combine_kernel,
        out_shape=jax.ShapeDtypeStruct((t, MIX_WIDTH), BF16),
        grid=(t // tm,),
        in_specs=[o_spec, o_spec, o_spec,
                  pl.BlockSpec((tm, LANES), lambda i: (i, COL_A_GATE // LANES))],
        out_specs=o_spec,
        compiler_params=_cparams(("parallel",)),
        name="nsa_combine",
    )(o_cmp, o_sel, o_win, proj)


def _selection_share(n_half, n_sel):
    starts = np.arange(n_half)[:, None] * CMP_STRIDE
    blk = np.arange(LANES)[None, :] * SEL_BLOCK
    shared = np.clip(np.minimum(starts + CMP_LEN, blk + SEL_BLOCK) - np.maximum(starts, blk), 0, None)
    shared = shared / CMP_STRIDE
    shared[n_half - 1, :] = 0.0
    shared[:, n_sel:] = 0.0
    return shared.astype(np.float32)


def _expand_matrices(s, block):
    key = np.arange(s).reshape(s // TK, 1, TK)
    n = np.arange(LANES).reshape(1, LANES, 1)
    return (key // block == n).astype(np.float32)


def _rearranged_w_in(w):
    d = w.shape[0]
    gate_pad = COL_MERGE - COL_A_GATE - ORIG_GATE_W
    merge_start = w.shape[1] - 3 * D_MODEL
    return jnp.concatenate([
        w[:, :ORIG_QKV_A],
        w[:, ORIG_QKV_A + ORIG_GATE_W:merge_start],
        w[:, ORIG_QKV_A:ORIG_QKV_A + ORIG_GATE_W],
        jnp.zeros((d, gate_pad), w.dtype),
        w[:, merge_start:],
    ], axis=1).astype(BF16)


def kernel(x, w_in, cmp_pos, cmp_w1, cmp_w2, swa_sinks, w_branch, w_out, w_mlp_in, w_mlp_out,
           norm_mix, norm_mlp, norm_final, rel_bias):
    b, s, d = x.shape
    depth = w_in.shape[0]
    t = b * s
    n_half = s // CMP_STRIDE
    n_sel = s // SEL_BLOCK

    tbl_flat = rel_bias.reshape(-1)
    c31 = rel_bias[N_BUCKETS - 1]
    band = band_bias(tbl_flat)
    bias_c = cmp_bias(tbl_flat, s)
    share = jnp.asarray(_selection_share(n_half, n_sel), BF16)
    exp_sel = jnp.asarray(_expand_matrices(s, SEL_BLOCK), BF16)
    exp_moba = jnp.asarray(_expand_matrices(s, MOBA_BLOCK), BF16)

    xt = x.reshape(t, d)
    for layer in range(depth):
        proj2d = norm_matmul(xt, norm_mix[layer], _rearranged_w_in(w_in[layer]))
        proj = proj2d.reshape(b, s, N_PROJ)

        halves = proj[:, :, COL_A_KC:COL_A_KC + 2 * N_KV * HEAD_DIM]
        halves = halves.reshape(b, n_half, CMP_STRIDE, 2, N_KV, HEAD_DIM)
        halves = halves.transpose(0, 3, 4, 1, 2, 5).reshape(b, 2, N_KV, n_half, CMP_STRIDE * HEAD_DIM)
        cmp_tokens = nsa_compress(halves, cmp_pos[layer].reshape(2, 2, CMP_STRIDE * HEAD_DIM),
                                  cmp_w1[layer].astype(BF16), cmp_w2[layer].astype(BF16))
        o_cmp, sel_mask = nsa_compressed(bias_c, proj, cmp_tokens, share)
        o_sel = flash_attention("sel", proj, band, c31, q_col=COL_A_Q, k_col=COL_A_KS, v_col=COL_A_VS,
                                out_dtype=F32, mask=sel_mask, expand=exp_sel)
        o_win = flash_attention("win", proj, band, c31, q_col=COL_A_Q, k_col=COL_A_KW, v_col=COL_A_VW,
                                out_dtype=F32)
        o_a = nsa_combine(o_cmp.reshape(t, MIX_WIDTH), o_sel.reshape(t, MIX_WIDTH),
                          o_win.reshape(t, MIX_WIDTH), proj2d)

        o_b = flash_attention("swa", proj, band, c31, q_col=COL_B_Q, k_col=COL_B_K, v_col=COL_B_V,
                              out_dtype=BF16, sinks=swa_sinks[layer])
        moba_mask = moba_gate(proj)
        o_c = flash_attention("moba", proj, band, c31, q_col=COL_C_Q, k_col=COL_C_K, v_col=COL_C_V,
                              out_dtype=BF16, mask=moba_mask, expand=exp_moba)

        z = merge_branches(o_a, o_b.reshape(t, MIX_WIDTH), o_c.reshape(t, MIX_WIDTH),
                           w_branch[layer].astype(BF16), proj2d)
        xt = matmul_residual(z, w_out[layer].astype(BF16), xt)
        xt = mlp_block(xt, norm_mlp[layer], w_mlp_in[layer].astype(BF16), w_mlp_out[layer].astype(BF16))

    return final_norm(xt, norm_final).reshape(b, s, d)
```

```python
import functools
import math

import numpy as np
import jax
import jax.numpy as jnp
from jax import lax
from jax.experimental import pallas as pl
from jax.experimental.pallas import tpu as pltpu

F32 = jnp.float32
BF16 = jnp.bfloat16

D_MODEL = 2048
HEAD_DIM = 128
N_HEADS = 8
N_KV = 2
GROUP = N_HEADS // N_KV
MIX_WIDTH = N_HEADS * HEAD_DIM
CMP_LEN = 32
CMP_STRIDE = 16
CMP_HIDDEN = 2 * HEAD_DIM
SEL_BLOCK = 64
SEL_TOP_N = 16
NSA_WINDOW = 512
SWA_WINDOW = 128
MOBA_BLOCK = 256
MOBA_TOP_K = 3
N_BUCKETS = 32
BUCKET_EXACT = N_BUCKETS // 2
BUCKET_MAX_DIST = 128
TOTAL_HEADS = 3 * N_HEADS
D_FF = 4 * D_MODEL
RMS_EPS = 1e-6
NEG_INF = -1e30
FORCE_SCORE = 1e30
TINY = 1e-30
BELOW_ALL = -3e38
SCALE = HEAD_DIM ** -0.5

LANES = 128
BF16_ROWS = 16
VMEM_LIMIT = 56 * 1024 * 1024
TQ = 128
TK = 128
COLS = GROUP * TQ

COL_A_Q = 0
COL_A_KC = 1024
COL_A_KS = 1536
COL_A_VS = 1792
COL_A_KW = 2048
COL_A_VW = 2304
COL_B_Q = 2560
COL_B_K = 3584
COL_B_V = 3840
COL_C_Q = 4096
COL_C_K = 5120
COL_C_V = 5376
COL_A_GATE = 5632
COL_MERGE = 6144
N_PROJ = COL_MERGE + 3 * D_MODEL
ORIG_QKV_A = 2560
ORIG_GATE_W = 3 * N_HEADS

MODE_HEAD0 = {"sel": 0, "win": 0, "swa": N_HEADS, "moba": 2 * N_HEADS}


def _cparams(semantics):
    return pltpu.CompilerParams(dimension_semantics=semantics, vmem_limit_bytes=VMEM_LIMIT)


def _rms(x, gain):
    y = x * lax.rsqrt(jnp.mean(x * x, axis=-1, keepdims=True) + RMS_EPS)
    return y * gain


def _norm_matmul_kernel(x_ref, g_ref, w_ref, o_ref, h_scr):
    @pl.when(pl.program_id(1) == 0)
    def _():
        h_scr[...] = _rms(x_ref[...], g_ref[...]).astype(BF16)

    o_ref[...] = jnp.dot(h_scr[...], w_ref[...], preferred_element_type=F32)


def norm_matmul(x, gain, w, *, tm=1024, tn=512):
    t, k = x.shape
    n = w.shape[1]
    return pl.pallas_call(
        _norm_matmul_kernel,
        out_shape=jax.ShapeDtypeStruct((t, n), F32),
        grid=(t // tm, n // tn),
        in_specs=[pl.BlockSpec((tm, k), lambda i, j: (i, 0)),
                  pl.BlockSpec((1, k), lambda i, j: (0, 0)),
                  pl.BlockSpec((k, tn), lambda i, j: (0, j))],
        out_specs=pl.BlockSpec((tm, tn), lambda i, j: (i, j)),
        scratch_shapes=[pltpu.VMEM((tm, k), BF16)],
        compiler_params=_cparams(("parallel", "arbitrary")),
        name="norm_matmul",
    )(x, gain.reshape(1, k), w)


def _merge_kernel(oa_ref, ob_ref, oc_ref, wb_ref, g0_ref, g1_ref, g2_ref, z_ref):
    acc = jax.nn.sigmoid(g0_ref[...]) * jnp.dot(oa_ref[...], wb_ref[0], preferred_element_type=F32)
    acc += jax.nn.sigmoid(g1_ref[...]) * jnp.dot(ob_ref[...], wb_ref[1], preferred_element_type=F32)
    acc += jax.nn.sigmoid(g2_ref[...]) * jnp.dot(oc_ref[...], wb_ref[2], preferred_element_type=F32)
    z_ref[...] = acc.astype(BF16)


def merge_branches(o_a, o_b, o_c, w_branch, proj, *, tm=512, tn=512):
    t = o_a.shape[0]
    gate_blk = COL_MERGE // tn
    per_branch = D_MODEL // tn
    o_spec = pl.BlockSpec((tm, MIX_WIDTH), lambda i, j: (i, 0))

    def gate_spec(m):
        return pl.BlockSpec((tm, tn), lambda i, j: (i, gate_blk + m * per_branch + j))

    return pl.pallas_call(
        _merge_kernel,
        out_shape=jax.ShapeDtypeStruct((t, D_MODEL), BF16),
        grid=(t // tm, D_MODEL // tn),
        in_specs=[o_spec, o_spec, o_spec,
                  pl.BlockSpec((3, MIX_WIDTH, tn), lambda i, j: (0, 0, j)),
                  gate_spec(0), gate_spec(1), gate_spec(2)],
        out_specs=pl.BlockSpec((tm, tn), lambda i, j: (i, j)),
        compiler_params=_cparams(("parallel", "parallel")),
        name="merge_branches",
    )(o_a, o_b, o_c, w_branch, proj, proj, proj)


def _matmul_res_kernel(a_ref, w_ref, x_ref, o_ref):
    o_ref[...] = x_ref[...] + jnp.dot(a_ref[...], w_ref[...], preferred_element_type=F32)


def matmul_residual(a, w, x, *, tm=1024, tn=512):
    t, k = a.shape
    n = w.shape[1]
    return pl.pallas_call(
        _matmul_res_kernel,
        out_shape=jax.ShapeDtypeStruct((t, n), F32),
        grid=(t // tm, n // tn),
        in_specs=[pl.BlockSpec((tm, k), lambda i, j: (i, 0)),
                  pl.BlockSpec((k, tn), lambda i, j: (0, j)),
                  pl.BlockSpec((tm, tn), lambda i, j: (i, j))],
        out_specs=pl.BlockSpec((tm, tn), lambda i, j: (i, j)),
        compiler_params=_cparams(("parallel", "parallel")),
        name="matmul_residual",
    )(a, w, x)


def _mlp_kernel(x_ref, g_ref, w1_ref, w2_ref, o_ref, h_scr, acc_scr):
    f = pl.program_id(1)

    @pl.when(f == 0)
    def _():
        h_scr[...] = _rms(x_ref[...], g_ref[...]).astype(BF16)
        acc_scr[...] = jnp.zeros_like(acc_scr)

    u = jnp.dot(h_scr[...], w1_ref[...], preferred_element_type=F32)
    u = jnp.square(jnp.maximum(u, 0.0)).astype(BF16)
    acc_scr[...] += jnp.dot(u, w2_ref[...], preferred_element_type=F32)

    @pl.when(f == pl.num_programs(1) - 1)
    def _():
        o_ref[...] = x_ref[...] + acc_scr[...]


def mlp_block(x, gain, w1, w2, *, tm=512, tf=512):
    t, d = x.shape
    dff = w1.shape[1]
    return pl.pallas_call(
        _mlp_kernel,
        out_shape=jax.ShapeDtypeStruct((t, d), F32),
        grid=(t // tm, dff // tf),
        in_specs=[pl.BlockSpec((tm, d), lambda i, f: (i, 0)),
                  pl.BlockSpec((1, d), lambda i, f: (0, 0)),
                  pl.BlockSpec((d, tf), lambda i, f: (0, f)),
                  pl.BlockSpec((tf, d), lambda i, f: (f, 0))],
        out_specs=pl.BlockSpec((tm, d), lambda i, f: (i, 0)),
        scratch_shapes=[pltpu.VMEM((tm, d), BF16), pltpu.VMEM((tm, d), F32)],
        compiler_params=_cparams(("parallel", "arbitrary")),
        name="mlp_block",
    )(x, gain.reshape(1, d), w1, w2)


def _final_norm_kernel(x_ref, g_ref, o_ref):
    o_ref[...] = _rms(x_ref[...], g_ref[...])


def final_norm(x, gain, *, tm=512):
    t, d = x.shape
    return pl.pallas_call(
        _final_norm_kernel,
        out_shape=jax.ShapeDtypeStruct((t, d), F32),
        grid=(t // tm,),
        in_specs=[pl.BlockSpec((tm, d), lambda i: (i, 0)),
                  pl.BlockSpec((1, d), lambda i: (0, 0))],
        out_specs=pl.BlockSpec((tm, d), lambda i: (i, 0)),
        compiler_params=_cparams(("parallel",)),
        name="final_norm",
    )(x, gain.reshape(1, d))


def _t5_bucket(n):
    log_ratio = jnp.log(jnp.maximum(n, 1).astype(F32) / BUCKET_EXACT) / math.log(BUCKET_MAX_DIST / BUCKET_EXACT)
    large = jnp.minimum(BUCKET_EXACT + (log_ratio * (N_BUCKETS - BUCKET_EXACT)).astype(jnp.int32), N_BUCKETS - 1)
    return jnp.where(n < BUCKET_EXACT, n, large)


def _lookup_bias(tbl_ref, head, n):
    bucket = _t5_bucket(n)
    out = jnp.zeros(n.shape, F32)
    for b in range(N_BUCKETS):
        out = jnp.where(bucket == b, tbl_ref[b * TOTAL_HEADS + head], out)
    return out


def _band_bias_kernel(tbl_ref, o_ref):
    l = lax.broadcasted_iota(jnp.int32, (TK, TQ), 0)
    q = lax.broadcasted_iota(jnp.int32, (TK, TQ), 1)
    head = pl.program_id(0)
    o_ref[0] = _lookup_bias(tbl_ref, head, jnp.maximum(q + TK - l, 0))
    o_ref[1] = _lookup_bias(tbl_ref, head, jnp.maximum(q - l, 0))


def band_bias(tbl_flat):
    return pl.pallas_call(
        _band_bias_kernel,
        out_shape=jax.ShapeDtypeStruct((2, TK, TOTAL_HEADS * TQ), F32),
        grid=(TOTAL_HEADS,),
        in_specs=[pl.BlockSpec(memory_space=pltpu.SMEM)],
        out_specs=pl.BlockSpec((2, TK, TQ), lambda h: (0, 0, h)),
        compiler_params=_cparams(("parallel",)),
        name="band_bias",
    )(tbl_flat)


def _cmp_bias_kernel(tbl_ref, o_ref):
    c = lax.broadcasted_iota(jnp.int32, (LANES, TQ), 0)
    t = pl.program_id(0) * TQ + lax.broadcasted_iota(jnp.int32, (LANES, TQ), 1)
    o_ref[0] = _lookup_bias(tbl_ref, pl.program_id(1), jnp.maximum(t - (c * CMP_STRIDE + CMP_LEN - 1), 0))


def cmp_bias(tbl_flat, s):
    return pl.pallas_call(
        _cmp_bias_kernel,
        out_shape=jax.ShapeDtypeStruct((s // TQ, LANES, N_HEADS * TQ), F32),
        grid=(s // TQ, N_HEADS),
        in_specs=[pl.BlockSpec(memory_space=pltpu.SMEM)],
        out_specs=pl.BlockSpec((1, LANES, TQ), lambda i, h: (i, 0, h)),
        compiler_params=_cparams(("parallel", "parallel")),
        name="cmp_bias",
    )(tbl_flat)


def _compress_kernel(half_ref, pos_ref, w1_ref, w2_ref, o_ref):
    half = half_ref[0, 0, 0]
    half_w = CMP_STRIDE * HEAD_DIM
    xa = (half + pos_ref[0, 0:1, :]).astype(BF16)
    xb = (half + pos_ref[0, 1:2, :]).astype(BF16)
    ha = jnp.dot(xa, w1_ref[0, 0:half_w, :], preferred_element_type=F32)
    hb = jnp.dot(xb, w1_ref[0, half_w:2 * half_w, :], preferred_element_type=F32)
    n_half = hb.shape[0]
    hidden = ha + pltpu.roll(hb, n_half - 1, 0)
    act = jax.nn.gelu(hidden, approximate=True).astype(BF16)
    o_ref[0, 0, 0] = jnp.dot(act, w2_ref[0], preferred_element_type=F32)


def nsa_compress(halves, pos, w1, w2):
    b, _, n_kv, n_half, width = halves.shape
    return pl.pallas_call(
        _compress_kernel,
        out_shape=jax.ShapeDtypeStruct((b, 2, n_kv, n_half, HEAD_DIM), F32),
        grid=(2, b, n_kv),
        in_specs=[pl.BlockSpec((1, 1, 1, n_half, width), lambda kv, bi, h: (bi, kv, h, 0, 0)),
                  pl.BlockSpec((1, 2, width), lambda kv, bi, h: (kv, 0, 0)),
                  pl.BlockSpec((1, 2 * width, CMP_HIDDEN), lambda kv, bi, h: (kv, 0, 0)),
                  pl.BlockSpec((1, CMP_HIDDEN, HEAD_DIM), lambda kv, bi, h: (kv, 0, 0))],
        out_specs=pl.BlockSpec((1, 1, 1, n_half, HEAD_DIM), lambda kv, bi, h: (bi, kv, h, 0, 0)),
        compiler_params=_cparams(("parallel", "parallel", "parallel")),
        name="nsa_compress",
    )(halves, pos, w1, w2)


def _stack_heads(q):
    return jnp.concatenate([q[:, g * HEAD_DIM:(g + 1) * HEAD_DIM] for g in range(GROUP)], axis=0)


def _untranspose_heads(o_t):
    return jnp.concatenate([o_t[:, g * TQ:(g + 1) * TQ].T for g in range(GROUP)], axis=1)


def _head_row(ref, first):
    return jnp.concatenate([jnp.full((1, TQ), ref[first + g], F32) for g in range(GROUP)], axis=1)


def _split3(x):
    x1 = x.astype(BF16)
    r1 = x - x1.astype(F32)
    x2 = r1.astype(BF16)
    x3 = (r1 - x2.astype(F32)).astype(BF16)
    return x1, x2, x3


def _dot_nt(a, b):
    return lax.dot_general(a, b, (((1,), (1,)), ((), ())), preferred_element_type=F32)


def _rank_select(score, n_cand, n_top):
    idx = lax.broadcasted_iota(jnp.int32, score.shape, 0)
    rank = jnp.zeros(score.shape, jnp.int32)
    for j in range(n_cand):
        other = score[j:j + 1, :]
        ahead = (other > score) | ((other == score) & (j < idx))
        rank += jnp.where(ahead, 1, 0)
    return jnp.where((rank < n_top) & (idx < n_cand), 1.0, 0.0)


def _nsa_cmp_kernel(bias_ref, q_ref, kc_ref, vc_ref, share_ref, o_ref, sel_ref):
    qi = pl.program_id(2)
    q4 = _stack_heads(q_ref[0]).astype(BF16)
    kc = kc_ref[0, 0, 0].astype(BF16)
    vc_t = vc_ref[0, 0, 0].T.astype(BF16)
    logits = _dot_nt(kc, q4) * SCALE + bias_ref[0]
    c = lax.broadcasted_iota(jnp.int32, (LANES, COLS), 0)
    t = qi * TQ + (lax.broadcasted_iota(jnp.int32, (LANES, COLS), 1) & (TQ - 1))
    valid = t - (c * CMP_STRIDE + CMP_LEN - 1) >= 0
    logits = jnp.where(valid, logits, NEG_INF)
    m = jnp.max(logits, axis=0, keepdims=True)
    p = jnp.where(valid, jnp.exp(logits - m), 0.0)
    p = p / jnp.maximum(jnp.sum(p, axis=0, keepdims=True), TINY)
    o_ref[0] = _untranspose_heads(jnp.dot(vc_t, p.astype(BF16), preferred_element_type=F32))

    p_sum = p[:, 0:TQ]
    for g in range(1, GROUP):
        p_sum = p_sum + p[:, g * TQ:(g + 1) * TQ]
    share = share_ref[...]
    importance = sum(jnp.dot(share, part, preferred_element_type=F32) for part in _split3(p_sum))
    n_sel = share.shape[0]
    blk = lax.broadcasted_iota(jnp.int32, (n_sel, TQ), 0)
    cur = (qi * TQ + lax.broadcasted_iota(jnp.int32, (n_sel, TQ), 1)) // SEL_BLOCK
    forced = (blk == 0) | (blk == cur) | (blk == cur - 1)
    score = jnp.where(forced, FORCE_SCORE, jnp.where(blk <= cur, importance, NEG_INF))
    sel_ref[0, 0] = _rank_select(score, n_sel, min(SEL_TOP_N, n_sel))


def nsa_compressed(bias_c, proj, cmp_tokens, share):
    b, s, _ = proj.shape
    n_cmp = cmp_tokens.shape[3]
    n_sel = share.shape[0]
    q_blk = COL_A_Q // (GROUP * HEAD_DIM)
    return pl.pallas_call(
        _nsa_cmp_kernel,
        out_shape=(jax.ShapeDtypeStruct((b, s, MIX_WIDTH), F32),
                   jax.ShapeDtypeStruct((b, N_KV, n_sel, s), F32)),
        grid=(b, N_KV, s // TQ),
        in_specs=[pl.BlockSpec((1, n_cmp, COLS), lambda bi, h, i: (i, 0, h)),
                  pl.BlockSpec((1, TQ, GROUP * HEAD_DIM), lambda bi, h, i: (bi, i, q_blk + h)),
                  pl.BlockSpec((1, 1, 1, n_cmp, HEAD_DIM), lambda bi, h, i: (bi, 0, h, 0, 0)),
                  pl.BlockSpec((1, 1, 1, n_cmp, HEAD_DIM), lambda bi, h, i: (bi, 1, h, 0, 0)),
                  pl.BlockSpec((n_sel, n_cmp), lambda bi, h, i: (0, 0))],
        out_specs=(pl.BlockSpec((1, TQ, GROUP * HEAD_DIM), lambda bi, h, i: (bi, i, h)),
                   pl.BlockSpec((1, 1, n_sel, TQ), lambda bi, h, i: (bi, h, 0, i))),
        compiler_params=_cparams(("parallel", "parallel", "parallel")),
        name="nsa_compressed",
    )(bias_c, proj, cmp_tokens, cmp_tokens, share)


def _moba_gate_kernel(q_ref, k_ref, sel_ref, km_scr):
    qi = pl.program_id(2)
    n_blk = k_ref.shape[1] // MOBA_BLOCK

    @pl.when(qi == 0)
    def _():
        km_scr[...] = jnp.zeros_like(km_scr)
        for n in range(n_blk):
            blk = k_ref[0, n * MOBA_BLOCK:(n + 1) * MOBA_BLOCK, :]
            km_scr[n:n + 1, :] = jnp.sum(blk, axis=0, keepdims=True) / MOBA_BLOCK

    q4 = _stack_heads(q_ref[0])
    q_hi, q_lo, _ = _split3(q4)
    k_hi, k_lo, _ = _split3(km_scr[...])
    gate = _dot_nt(k_hi, q_hi) + _dot_nt(k_lo, q_hi) + _dot_nt(k_hi, q_lo)
    n = lax.broadcasted_iota(jnp.int32, gate.shape, 0)
    q_blk = (qi * TQ) // MOBA_BLOCK
    gate = jnp.where(n < q_blk, gate, NEG_INF)
    gate = jnp.where(n < n_blk, gate, BELOW_ALL)
    picked = _rank_select(gate, n_blk, min(MOBA_TOP_K, n_blk - 1))
    sel_ref[0, 0, 0] = jnp.where(n < q_blk, picked, 0.0)


def moba_gate(proj):
    b, s, _ = proj.shape
    q_blk = COL_C_Q // (GROUP * HEAD_DIM)
    k_blk = COL_C_K // HEAD_DIM
    return pl.pallas_call(
        _moba_gate_kernel,
        out_shape=jax.ShapeDtypeStruct((b, N_KV, s // TQ, BF16_ROWS, COLS), F32),
        grid=(b, N_KV, s // TQ),
        in_specs=[pl.BlockSpec((1, TQ, GROUP * HEAD_DIM), lambda bi, h, i: (bi, i, q_blk + h)),
                  pl.BlockSpec((1, s, HEAD_DIM), lambda bi, h, i: (bi, 0, k_blk + h))],
        out_specs=pl.BlockSpec((1, 1, 1, BF16_ROWS, COLS), lambda bi, h, i: (bi, h, i, 0, 0)),
        scratch_shapes=[pltpu.VMEM((BF16_ROWS, HEAD_DIM), F32)],
        compiler_params=_cparams(("parallel", "parallel", "arbitrary")),
        name="moba_gate",
    )(proj, proj)


def _flash_kernel(mode, *refs):
    if mode == "swa":
        c31_ref, sink_ref, band_ref, q_ref, k_ref, v_ref, o_ref, kb_scr, vt_scr, m_scr, l_scr, acc_scr = refs
    elif mode == "win":
        c31_ref, band_ref, q_ref, k_ref, v_ref, o_ref, kb_scr, vt_scr, m_scr, l_scr, acc_scr = refs
    else:
        c31_ref, band_ref, q_ref, k_ref, v_ref, msk_ref, o_ref, kb_scr, vt_scr, m_scr, l_scr, acc_scr = refs
    kvh = pl.program_id(1)
    qi = pl.program_id(2)
    head0 = MODE_HEAD0[mode] + kvh * GROUP
    n_kb = k_ref.shape[1] // TK

    @pl.when(qi == 0)
    def _():
        kb_scr[...] = k_ref[0].astype(BF16)
        for j in range(n_kb):
            vt_scr[j] = v_ref[0, j * TK:(j + 1) * TK, :].T.astype(BF16)

    q4 = _stack_heads(q_ref[0]).astype(BF16)
    m_scr[...] = jnp.full_like(m_scr, NEG_INF)
    l_scr[...] = jnp.zeros_like(l_scr)
    acc_scr[...] = jnp.zeros_like(acc_scr)

    key = lax.broadcasted_iota(jnp.int32, (TK, COLS), 0)
    qry = lax.broadcasted_iota(jnp.int32, (TK, COLS), 1) & (TQ - 1)
    far_bias = _head_row(c31_ref, head0)

    def sel_valid(j):
        halves = []
        for part in range(TK // SEL_BLOCK):
            r = msk_ref[0, 0, pl.ds(j * (TK // SEL_BLOCK) + part, 1), :]
            r = jnp.concatenate([r] * GROUP, axis=1)
            halves.append(jnp.broadcast_to(r, (SEL_BLOCK, COLS)))
        return jnp.concatenate(halves, axis=0) > 0.5

    def moba_valid(j, always=0.0):
        r = msk_ref[0, 0, 0, pl.ds((j * TK) // MOBA_BLOCK, 1), :]
        return jnp.broadcast_to(r + always, (TK, COLS)) > 0.5

    def step(j, bias, valid):
        start = pl.multiple_of(j * TK, TK)
        s = _dot_nt(kb_scr[pl.ds(start, TK), :], q4) * SCALE + bias
        if valid is not None:
            s = jnp.where(valid, s, NEG_INF)
        m_prev = m_scr[...]
        m_new = jnp.maximum(m_prev, jnp.max(s, axis=0, keepdims=True))
        alpha = jnp.exp(m_prev - m_new)
        p = jnp.exp(s - m_new)
        if valid is not None:
            p = jnp.where(valid, p, 0.0)
        l_scr[...] = alpha * l_scr[...] + jnp.sum(p, axis=0, keepdims=True)
        acc_scr[...] = alpha * acc_scr[...] + jnp.dot(vt_scr[j], p.astype(BF16), preferred_element_type=F32)
        m_scr[...] = m_new

    if mode in ("sel", "moba"):
        def far_body(j, carry):
            step(j, far_bias, sel_valid(j) if mode == "sel" else moba_valid(j))
            return carry
        lax.fori_loop(0, jnp.maximum(qi - 1, 0), far_body, 0)
    elif mode == "win":
        n_full = NSA_WINDOW // TK

        @pl.when(qi >= n_full)
        def _():
            step(qi - n_full, far_bias, key > qry)

        def far_body(j, carry):
            step(j, far_bias, None)
            return carry
        lax.fori_loop(jnp.maximum(qi - n_full + 1, 0), jnp.maximum(qi - 1, 0), far_body, 0)

    @pl.when(qi >= 1)
    def _():
        if mode == "sel":
            valid = sel_valid(qi - 1)
        elif mode == "moba":
            own = (qi * TQ) // MOBA_BLOCK == ((qi - 1) * TK) // MOBA_BLOCK
            valid = moba_valid(qi - 1, jnp.where(own, 1.0, 0.0))
        elif mode == "swa":
            valid = key > qry
        else:
            valid = None
        step(qi - 1, band_ref[0], valid)

    valid = key <= qry
    if mode == "sel":
        valid = valid & sel_valid(qi)
    step(qi, band_ref[1], valid)

    m = m_scr[...]
    l = l_scr[...]
    acc = acc_scr[...]
    if mode == "swa":
        sink = _head_row(sink_ref, kvh * GROUP)
        m_fin = jnp.maximum(m, sink)
        shrink = jnp.exp(m - m_fin)
        l = l * shrink + jnp.exp(sink - m_fin)
        acc = acc * shrink
    out = acc / jnp.maximum(l, TINY)
    o_ref[0] = _untranspose_heads(out).astype(o_ref.dtype)


def flash_attention(mode, proj, band, c31, *, q_col, k_col, v_col, out_dtype, mask=None, sinks=None):
    b, s, _ = proj.shape
    q_blk = q_col // (GROUP * HEAD_DIM)
    k_blk = k_col // HEAD_DIM
    v_blk = v_col // HEAD_DIM
    head_blk = MODE_HEAD0[mode] // GROUP
    smem = pl.BlockSpec(memory_space=pltpu.SMEM)
    in_specs = [smem]
    args = [c31]
    if mode == "swa":
        in_specs.append(smem)
        args.append(sinks)
    in_specs += [pl.BlockSpec((2, TK, COLS), lambda bi, h, i: (0, 0, head_blk + h)),
                 pl.BlockSpec((1, TQ, GROUP * HEAD_DIM), lambda bi, h, i: (bi, i, q_blk + h)),
                 pl.BlockSpec((1, s, HEAD_DIM), lambda bi, h, i: (bi, 0, k_blk + h)),
                 pl.BlockSpec((1, s, HEAD_DIM), lambda bi, h, i: (bi, 0, v_blk + h))]
    args += [band, proj, proj, proj]
    if mode == "sel":
        in_specs.append(pl.BlockSpec((1, 1, mask.shape[2], TQ), lambda bi, h, i: (bi, h, 0, i)))
        args.append(mask)
    elif mode == "moba":
        in_specs.append(pl.BlockSpec((1, 1, 1, mask.shape[3], COLS), lambda bi, h, i: (bi, h, i, 0, 0)))
        args.append(mask)
    return pl.pallas_call(
        functools.partial(_flash_kernel, mode),
        out_shape=jax.ShapeDtypeStruct((b, s, MIX_WIDTH), out_dtype),
        grid=(b, N_KV, s // TQ),
        in_specs=in_specs,
        out_specs=pl.BlockSpec((1, TQ, GROUP * HEAD_DIM), lambda bi, h, i: (bi, i, h)),
        scratch_shapes=[pltpu.VMEM((s, HEAD_DIM), BF16), pltpu.VMEM((s // TK, HEAD_DIM, TK), BF16),
                        pltpu.VMEM((1, COLS), F32), pltpu.VMEM((1, COLS), F32), pltpu.VMEM((HEAD_DIM, COLS), F32)],
        compiler_params=_cparams(("parallel", "parallel", "arbitrary")),
        name="flash_" + mode,
    )(*args)


def _nsa_combine_kernel(cmp_ref, sel_ref, win_ref, gate_ref, o_ref):
    gate = jax.nn.sigmoid(gate_ref[...])
    for h in range(N_HEADS):
        cols = slice(h * HEAD_DIM, (h + 1) * HEAD_DIM)
        out = (gate[:, 3 * h:3 * h + 1] * cmp_ref[:, cols]
               + gate[:, 3 * h + 1:3 * h + 2] * sel_ref[:, cols]
               + gate[:, 3 * h + 2:3 * h + 3] * win_ref[:, cols])
        o_ref[:, cols] = out.astype(BF16)


def nsa_combine(o_cmp, o_sel, o_win, proj, *, tm=512):
    t = o_cmp.shape[0]
    o_spec = pl.BlockSpec((tm, MIX_WIDTH), lambda i: (i, 0))
    return pl.pallas_call(
        _nsa_combine_kernel,
        out_shape=jax.ShapeDtypeStruct((t, MIX_WIDTH), BF16),
        grid=(t // tm,),
        in_specs=[o_spec, o_spec, o_spec,
                  pl.BlockSpec((tm, LANES), lambda i: (i, COL_A_GATE // LANES))],
        out_specs=o_spec,
        compiler_params=_cparams(("parallel",)),
        name="nsa_combine",
    )(o_cmp, o_sel, o_win, proj)


def _selection_share(n_sel, n_half):
    blk = np.arange(n_sel)[:, None] * SEL_BLOCK
    starts = np.arange(n_half)[None, :] * CMP_STRIDE
    shared = np.clip(np.minimum(starts + CMP_LEN, blk + SEL_BLOCK) - np.maximum(starts, blk), 0, None)
    shared = shared / CMP_STRIDE
    shared[:, n_half - 1] = 0.0
    return shared.astype(np.float32)


def _rearranged_w_in(w):
    d = w.shape[0]
    gate_pad = COL_MERGE - COL_A_GATE - ORIG_GATE_W
    merge_start = w.shape[1] - 3 * D_MODEL
    return jnp.concatenate([
        w[:, :ORIG_QKV_A],
        w[:, ORIG_QKV_A + ORIG_GATE_W:merge_start],
        w[:, ORIG_QKV_A:ORIG_QKV_A + ORIG_GATE_W],
        jnp.zeros((d, gate_pad), w.dtype),
        w[:, merge_start:],
    ], axis=1).astype(BF16)


def kernel(x, w_in, cmp_pos, cmp_w1, cmp_w2, swa_sinks, w_branch, w_out, w_mlp_in, w_mlp_out,
           norm_mix, norm_mlp, norm_final, rel_bias):
    b, s, d = x.shape
    depth = w_in.shape[0]
    t = b * s
    n_half = s // CMP_STRIDE
    n_sel = s // SEL_BLOCK

    tbl_flat = rel_bias.reshape(-1)
    c31 = rel_bias[N_BUCKETS - 1]
    band = band_bias(tbl_flat)
    bias_c = cmp_bias(tbl_flat, s)
    share = jnp.asarray(_selection_share(n_sel, n_half), BF16)

    xt = x.reshape(t, d)
    for layer in range(depth):
        proj2d = norm_matmul(xt, norm_mix[layer], _rearranged_w_in(w_in[layer]))
        proj = proj2d.reshape(b, s, N_PROJ)

        halves = proj[:, :, COL_A_KC:COL_A_KC + 2 * N_KV * HEAD_DIM]
        halves = halves.reshape(b, n_half, CMP_STRIDE, 2, N_KV, HEAD_DIM)
        halves = halves.transpose(0, 3, 4, 1, 2, 5).reshape(b, 2, N_KV, n_half, CMP_STRIDE * HEAD_DIM)
        cmp_tokens = nsa_compress(halves, cmp_pos[layer].reshape(2, 2, CMP_STRIDE * HEAD_DIM),
                                  cmp_w1[layer].astype(BF16), cmp_w2[layer].astype(BF16))
        o_cmp, sel_mask = nsa_compressed(bias_c, proj, cmp_tokens, share)
        o_sel = flash_attention("sel", proj, band, c31, q_col=COL_A_Q, k_col=COL_A_KS, v_col=COL_A_VS,
                                out_dtype=F32, mask=sel_mask)
        o_win = flash_attention("win", proj, band, c31, q_col=COL_A_Q, k_col=COL_A_KW, v_col=COL_A_VW,
                                out_dtype=F32)
        o_a = nsa_combine(o_cmp.reshape(t, MIX_WIDTH), o_sel.reshape(t, MIX_WIDTH),
                          o_win.reshape(t, MIX_WIDTH), proj2d)

        o_b = flash_attention("swa", proj, band, c31, q_col=COL_B_Q, k_col=COL_B_K, v_col=COL_B_V,
                              out_dtype=BF16, sinks=swa_sinks[layer])
        moba_mask = moba_gate(proj)
        o_c = flash_attention("moba", proj, band, c31, q_col=COL_C_Q, k_col=COL_C_K, v_col=COL_C_V,
                              out_dtype=BF16, mask=moba_mask)

        z = merge_branches(o_a, o_b.reshape(t, MIX_WIDTH), o_c.reshape(t, MIX_WIDTH),
                           w_branch[layer].astype(BF16), proj2d)
        xt = matmul_residual(z, w_out[layer].astype(BF16), xt)
        xt = mlp_block(xt, norm_mlp[layer], w_mlp_in[layer].astype(BF16), w_mlp_out[layer].astype(BF16))

    return final_norm(xt, norm_final).reshape(b, s, d)
```

```python
import functools
import math

import numpy as np
import jax
import jax.numpy as jnp
from jax import lax
from jax.experimental import pallas as pl
from jax.experimental.pallas import tpu as pltpu

F32 = jnp.float32
BF16 = jnp.bfloat16

D_MODEL = 2048
HEAD_DIM = 128
N_HEADS = 8
N_KV = 2
GROUP = N_HEADS // N_KV
MIX_WIDTH = N_HEADS * HEAD_DIM
CMP_LEN = 32
CMP_STRIDE = 16
CMP_HIDDEN = 2 * HEAD_DIM
SEL_BLOCK = 64
SEL_TOP_N = 16
NSA_WINDOW = 512
SWA_WINDOW = 128
MOBA_BLOCK = 256
MOBA_TOP_K = 3
N_BUCKETS = 32
BUCKET_EXACT = N_BUCKETS // 2
BUCKET_MAX_DIST = 128
TOTAL_HEADS = 3 * N_HEADS
D_FF = 4 * D_MODEL
RMS_EPS = 1e-6
NEG_INF = -1e30
FORCE_SCORE = 1e30
TINY = 1e-30
BELOW_ALL = -3e38
SCALE = HEAD_DIM ** -0.5

LANES = 128
BF16_ROWS = 16
VMEM_LIMIT = 56 * 1024 * 1024
TQ = 128
TK = 128
COLS = GROUP * TQ
KV_W = N_KV * HEAD_DIM
LOG2E = math.log2(math.e)
SCALE2 = SCALE * LOG2E

COL_A_Q = 0
COL_B_Q = 1024
COL_C_Q = 2048
COL_A_KC = 3072
COL_A_KS = 3584
COL_A_VS = 3840
COL_A_KW = 4096
COL_A_VW = 4352
COL_B_K = 4608
COL_B_V = 4864
COL_C_K = 5120
COL_C_V = 5376
COL_A_GATE = 5632
COL_MERGE = 6144
N_PROJ = COL_MERGE + 3 * D_MODEL
ORIG_GATE_W = 3 * N_HEADS

MIXER = {"sel": 0, "win": 0, "swa": 1, "moba": 2}


def _cparams(semantics):
    return pltpu.CompilerParams(dimension_semantics=semantics, vmem_limit_bytes=VMEM_LIMIT)


def _rms(x, gain):
    y = x * lax.rsqrt(jnp.mean(x * x, axis=-1, keepdims=True) + RMS_EPS)
    return y * gain


def _norm_matmul_kernel(x_ref, g_ref, w_ref, o_ref, h_scr):
    @pl.when(pl.program_id(1) == 0)
    def _():
        h_scr[...] = _rms(x_ref[...], g_ref[...]).astype(BF16)

    o_ref[...] = jnp.dot(h_scr[...], w_ref[...], preferred_element_type=F32)


def norm_matmul(x, gain, w, *, tm=1024, tn=512):
    t, k = x.shape
    n = w.shape[1]
    return pl.pallas_call(
        _norm_matmul_kernel,
        out_shape=jax.ShapeDtypeStruct((t, n), F32),
        grid=(t // tm, n // tn),
        in_specs=[pl.BlockSpec((tm, k), lambda i, j: (i, 0)),
                  pl.BlockSpec((1, k), lambda i, j: (0, 0)),
                  pl.BlockSpec((k, tn), lambda i, j: (0, j))],
        out_specs=pl.BlockSpec((tm, tn), lambda i, j: (i, j)),
        scratch_shapes=[pltpu.VMEM((tm, k), BF16)],
        compiler_params=_cparams(("parallel", "arbitrary")),
        name="norm_matmul",
    )(x, gain.reshape(1, k), w)


def _merge_kernel(oa_ref, ob_ref, oc_ref, wb_ref, g0_ref, g1_ref, g2_ref, z_ref):
    acc = jax.nn.sigmoid(g0_ref[...]) * jnp.dot(oa_ref[...], wb_ref[0], preferred_element_type=F32)
    acc += jax.nn.sigmoid(g1_ref[...]) * jnp.dot(ob_ref[...], wb_ref[1], preferred_element_type=F32)
    acc += jax.nn.sigmoid(g2_ref[...]) * jnp.dot(oc_ref[...], wb_ref[2], preferred_element_type=F32)
    z_ref[...] = acc.astype(BF16)


def merge_branches(o_a, o_b, o_c, w_branch, proj, *, tm=512, tn=512):
    t = o_a.shape[0]
    gate_blk = COL_MERGE // tn
    per_branch = D_MODEL // tn
    o_spec = pl.BlockSpec((tm, MIX_WIDTH), lambda i, j: (i, 0))

    def gate_spec(m):
        return pl.BlockSpec((tm, tn), lambda i, j: (i, gate_blk + m * per_branch + j))

    return pl.pallas_call(
        _merge_kernel,
        out_shape=jax.ShapeDtypeStruct((t, D_MODEL), BF16),
        grid=(t // tm, D_MODEL // tn),
        in_specs=[o_spec, o_spec, o_spec,
                  pl.BlockSpec((3, MIX_WIDTH, tn), lambda i, j: (0, 0, j)),
                  gate_spec(0), gate_spec(1), gate_spec(2)],
        out_specs=pl.BlockSpec((tm, tn), lambda i, j: (i, j)),
        compiler_params=_cparams(("parallel", "parallel")),
        name="merge_branches",
    )(o_a, o_b, o_c, w_branch, proj, proj, proj)


def _matmul_res_kernel(a_ref, w_ref, x_ref, o_ref):
    o_ref[...] = x_ref[...] + jnp.dot(a_ref[...], w_ref[...], preferred_element_type=F32)


def matmul_residual(a, w, x, *, tm=1024, tn=512):
    t, k = a.shape
    n = w.shape[1]
    return pl.pallas_call(
        _matmul_res_kernel,
        out_shape=jax.ShapeDtypeStruct((t, n), F32),
        grid=(t // tm, n // tn),
        in_specs=[pl.BlockSpec((tm, k), lambda i, j: (i, 0)),
                  pl.BlockSpec((k, tn), lambda i, j: (0, j)),
                  pl.BlockSpec((tm, tn), lambda i, j: (i, j))],
        out_specs=pl.BlockSpec((tm, tn), lambda i, j: (i, j)),
        compiler_params=_cparams(("parallel", "parallel")),
        name="matmul_residual",
    )(a, w, x)


def _mlp_kernel(x_ref, g_ref, w1_ref, w2_ref, o_ref, h_scr, acc_scr):
    f = pl.program_id(1)

    @pl.when(f == 0)
    def _():
        h_scr[...] = _rms(x_ref[...], g_ref[...]).astype(BF16)
        acc_scr[...] = jnp.zeros_like(acc_scr)

    u = jnp.dot(h_scr[...], w1_ref[...], preferred_element_type=F32)
    u = jnp.square(jnp.maximum(u, 0.0)).astype(BF16)
    acc_scr[...] += jnp.dot(u, w2_ref[...], preferred_element_type=F32)

    @pl.when(f == pl.num_programs(1) - 1)
    def _():
        o_ref[...] = x_ref[...] + acc_scr[...]


def mlp_block(x, gain, w1, w2, *, tm=512, tf=512):
    t, d = x.shape
    dff = w1.shape[1]
    return pl.pallas_call(
        _mlp_kernel,
        out_shape=jax.ShapeDtypeStruct((t, d), F32),
        grid=(t // tm, dff // tf),
        in_specs=[pl.BlockSpec((tm, d), lambda i, f: (i, 0)),
                  pl.BlockSpec((1, d), lambda i, f: (0, 0)),
                  pl.BlockSpec((d, tf), lambda i, f: (0, f)),
                  pl.BlockSpec((tf, d), lambda i, f: (f, 0))],
        out_specs=pl.BlockSpec((tm, d), lambda i, f: (i, 0)),
        scratch_shapes=[pltpu.VMEM((tm, d), BF16), pltpu.VMEM((tm, d), F32)],
        compiler_params=_cparams(("parallel", "arbitrary")),
        name="mlp_block",
    )(x, gain.reshape(1, d), w1, w2)


def _final_norm_kernel(x_ref, g_ref, o_ref):
    o_ref[...] = _rms(x_ref[...], g_ref[...])


def final_norm(x, gain, *, tm=512):
    t, d = x.shape
    return pl.pallas_call(
        _final_norm_kernel,
        out_shape=jax.ShapeDtypeStruct((t, d), F32),
        grid=(t // tm,),
        in_specs=[pl.BlockSpec((tm, d), lambda i: (i, 0)),
                  pl.BlockSpec((1, d), lambda i: (0, 0))],
        out_specs=pl.BlockSpec((tm, d), lambda i: (i, 0)),
        compiler_params=_cparams(("parallel",)),
        name="final_norm",
    )(x, gain.reshape(1, d))


def _t5_bucket(n):
    log_ratio = jnp.log(jnp.maximum(n, 1).astype(F32) / BUCKET_EXACT) / math.log(BUCKET_MAX_DIST / BUCKET_EXACT)
    large = jnp.minimum(BUCKET_EXACT + (log_ratio * (N_BUCKETS - BUCKET_EXACT)).astype(jnp.int32), N_BUCKETS - 1)
    return jnp.where(n < BUCKET_EXACT, n, large)


def _lookup_bias(tbl_ref, head, n):
    bucket = _t5_bucket(n)
    out = jnp.zeros(n.shape, F32)
    for b in range(N_BUCKETS):
        out = jnp.where(bucket == b, tbl_ref[b * TOTAL_HEADS + head], out)
    return out


def _band_bias_kernel(tbl_ref, o_ref):
    l = lax.broadcasted_iota(jnp.int32, (TK, TQ), 0)
    q = lax.broadcasted_iota(jnp.int32, (TK, TQ), 1)
    head = pl.program_id(0)
    o_ref[0] = _lookup_bias(tbl_ref, head, jnp.maximum(q + TK - l, 0)) * LOG2E
    o_ref[1] = _lookup_bias(tbl_ref, head, jnp.maximum(q - l, 0)) * LOG2E


def band_bias(tbl_flat):
    return pl.pallas_call(
        _band_bias_kernel,
        out_shape=jax.ShapeDtypeStruct((2, TK, TOTAL_HEADS * TQ), F32),
        grid=(TOTAL_HEADS,),
        in_specs=[pl.BlockSpec(memory_space=pltpu.SMEM)],
        out_specs=pl.BlockSpec((2, TK, TQ), lambda h: (0, 0, h)),
        compiler_params=_cparams(("parallel",)),
        name="band_bias",
    )(tbl_flat)


def _cmp_bias_kernel(tbl_ref, o_ref):
    c = lax.broadcasted_iota(jnp.int32, (LANES, TQ), 0)
    t = pl.program_id(0) * TQ + lax.broadcasted_iota(jnp.int32, (LANES, TQ), 1)
    o_ref[0] = _lookup_bias(tbl_ref, pl.program_id(1), jnp.maximum(t - (c * CMP_STRIDE + CMP_LEN - 1), 0))


def cmp_bias(tbl_flat, s):
    return pl.pallas_call(
        _cmp_bias_kernel,
        out_shape=jax.ShapeDtypeStruct((s // TQ, LANES, N_HEADS * TQ), F32),
        grid=(s // TQ, N_HEADS),
        in_specs=[pl.BlockSpec(memory_space=pltpu.SMEM)],
        out_specs=pl.BlockSpec((1, LANES, TQ), lambda i, h: (i, 0, h)),
        compiler_params=_cparams(("parallel", "parallel")),
        name="cmp_bias",
    )(tbl_flat)


def _compress_kernel(half_ref, pos_ref, w1_ref, w2_ref, o_ref):
    half = half_ref[0, 0, 0]
    half_w = CMP_STRIDE * HEAD_DIM
    xa = (half + pos_ref[0, 0:1, :]).astype(BF16)
    xb = (half + pos_ref[0, 1:2, :]).astype(BF16)
    ha = jnp.dot(xa, w1_ref[0, 0:half_w, :], preferred_element_type=F32)
    hb = jnp.dot(xb, w1_ref[0, half_w:2 * half_w, :], preferred_element_type=F32)
    n_half = hb.shape[0]
    hidden = ha + pltpu.roll(hb, n_half - 1, 0)
    act = jax.nn.gelu(hidden, approximate=True).astype(BF16)
    o_ref[0, 0, 0] = jnp.dot(act, w2_ref[0], preferred_element_type=F32)


def nsa_compress(halves, pos, w1, w2):
    b, _, n_kv, n_half, width = halves.shape
    return pl.pallas_call(
        _compress_kernel,
        out_shape=jax.ShapeDtypeStruct((b, 2, n_kv, n_half, HEAD_DIM), F32),
        grid=(2, b, n_kv),
        in_specs=[pl.BlockSpec((1, 1, 1, n_half, width), lambda kv, bi, h: (bi, kv, h, 0, 0)),
                  pl.BlockSpec((1, 2, width), lambda kv, bi, h: (kv, 0, 0)),
                  pl.BlockSpec((1, 2 * width, CMP_HIDDEN), lambda kv, bi, h: (kv, 0, 0)),
                  pl.BlockSpec((1, CMP_HIDDEN, HEAD_DIM), lambda kv, bi, h: (kv, 0, 0))],
        out_specs=pl.BlockSpec((1, 1, 1, n_half, HEAD_DIM), lambda kv, bi, h: (bi, kv, h, 0, 0)),
        compiler_params=_cparams(("parallel", "parallel", "parallel")),
        name="nsa_compress",
    )(halves, pos, w1, w2)


def _stack_heads(q):
    return jnp.concatenate([q[:, g * HEAD_DIM:(g + 1) * HEAD_DIM] for g in range(GROUP)], axis=0)


def _untranspose_heads(o_t):
    return jnp.concatenate([o_t[:, g * TQ:(g + 1) * TQ].T for g in range(GROUP)], axis=1)


def _head_row(ref, first):
    return jnp.concatenate([jnp.full((1, TQ), ref[first + g] * LOG2E, F32) for g in range(GROUP)], axis=1)


def _split3(x):
    x1 = x.astype(BF16)
    r1 = x - x1.astype(F32)
    x2 = r1.astype(BF16)
    x3 = (r1 - x2.astype(F32)).astype(BF16)
    return x1, x2, x3


def _dot_nt(a, b):
    return lax.dot_general(a, b, (((1,), (1,)), ((), ())), preferred_element_type=F32)


def _rank_select(score, n_cand, n_top):
    idx = lax.broadcasted_iota(jnp.int32, score.shape, 0)
    rank = jnp.zeros(score.shape, jnp.int32)
    for j in range(n_cand):
        other = score[j:j + 1, :]
        ahead = (other > score) | ((other == score) & (j < idx))
        rank += jnp.where(ahead, 1, 0)
    return jnp.where((rank < n_top) & (idx < n_cand), 1.0, 0.0)


def _nsa_cmp_kernel(bias_ref, q_ref, kc_ref, vc_ref, share_ref, o_ref, sel_ref):
    qi = pl.program_id(1)
    share = share_ref[...]
    n_sel = share.shape[0]
    c = lax.broadcasted_iota(jnp.int32, (LANES, COLS), 0)
    t = qi * TQ + (lax.broadcasted_iota(jnp.int32, (LANES, COLS), 1) & (TQ - 1))
    valid = t - (c * CMP_STRIDE + CMP_LEN - 1) >= 0
    blk = lax.broadcasted_iota(jnp.int32, (n_sel, TQ), 0)
    cur = (qi * TQ + lax.broadcasted_iota(jnp.int32, (n_sel, TQ), 1)) // SEL_BLOCK
    forced = (blk == 0) | (blk == cur) | (blk == cur - 1)
    for h in range(N_KV):
        cols = slice(h * COLS, (h + 1) * COLS)
        q4 = _stack_heads(q_ref[0, :, cols]).astype(BF16)
        kc = kc_ref[0, 0, h].astype(BF16)
        vc_t = vc_ref[0, 0, h].T.astype(BF16)
        logits = _dot_nt(kc, q4) * SCALE + bias_ref[0, :, cols]
        logits = jnp.where(valid, logits, NEG_INF)
        m = jnp.max(logits, axis=0, keepdims=True)
        p = jnp.where(valid, jnp.exp(logits - m), 0.0)
        p = p / jnp.maximum(jnp.sum(p, axis=0, keepdims=True), TINY)
        o_ref[0, :, cols] = _untranspose_heads(jnp.dot(vc_t, p.astype(BF16), preferred_element_type=F32))

        p_sum = p[:, 0:TQ]
        for g in range(1, GROUP):
            p_sum = p_sum + p[:, g * TQ:(g + 1) * TQ]
        importance = sum(jnp.dot(share, part, preferred_element_type=F32) for part in _split3(p_sum))
        score = jnp.where(forced, FORCE_SCORE, jnp.where(blk <= cur, importance, NEG_INF))
        sel_ref[0, h] = _rank_select(score, n_sel, min(SEL_TOP_N, n_sel))


def nsa_compressed(bias_c, proj, cmp_tokens, share):
    b, s, _ = proj.shape
    n_cmp = cmp_tokens.shape[3]
    n_sel = share.shape[0]
    return pl.pallas_call(
        _nsa_cmp_kernel,
        out_shape=(jax.ShapeDtypeStruct((b, s, MIX_WIDTH), F32),
                   jax.ShapeDtypeStruct((b, N_KV, n_sel, s), F32)),
        grid=(b, s // TQ),
        in_specs=[pl.BlockSpec((1, n_cmp, N_KV * COLS), lambda bi, i: (i, 0, 0)),
                  pl.BlockSpec((1, TQ, MIX_WIDTH), lambda bi, i: (bi, i, COL_A_Q // MIX_WIDTH)),
                  pl.BlockSpec((1, 1, N_KV, n_cmp, HEAD_DIM), lambda bi, i: (bi, 0, 0, 0, 0)),
                  pl.BlockSpec((1, 1, N_KV, n_cmp, HEAD_DIM), lambda bi, i: (bi, 1, 0, 0, 0)),
                  pl.BlockSpec((n_sel, n_cmp), lambda bi, i: (0, 0))],
        out_specs=(pl.BlockSpec((1, TQ, MIX_WIDTH), lambda bi, i: (bi, i, 0)),
                   pl.BlockSpec((1, N_KV, n_sel, TQ), lambda bi, i: (bi, 0, 0, i))),
        compiler_params=_cparams(("parallel", "parallel")),
        name="nsa_compressed",
    )(bias_c, proj, cmp_tokens, cmp_tokens, share)


def _moba_gate_kernel(q_ref, k_ref, sel_ref, km_scr):
    qi = pl.program_id(1)
    n_blk = k_ref.shape[1] // MOBA_BLOCK

    @pl.when(qi == 0)
    def _():
        km_scr[...] = jnp.zeros_like(km_scr)
        for h in range(N_KV):
            for n in range(n_blk):
                blk = k_ref[0, n * MOBA_BLOCK:(n + 1) * MOBA_BLOCK, h * HEAD_DIM:(h + 1) * HEAD_DIM]
                km_scr[h, n:n + 1, :] = jnp.sum(blk, axis=0, keepdims=True) / MOBA_BLOCK

    n = lax.broadcasted_iota(jnp.int32, (BF16_ROWS, COLS), 0)
    q_blk = (qi * TQ) // MOBA_BLOCK
    for h in range(N_KV):
        q4 = _stack_heads(q_ref[0, :, h * COLS:(h + 1) * COLS])
        q_hi, q_lo, _ = _split3(q4)
        k_hi, k_lo, _ = _split3(km_scr[h])
        gate = _dot_nt(k_hi, q_hi) + _dot_nt(k_lo, q_hi) + _dot_nt(k_hi, q_lo)
        gate = jnp.where(n < q_blk, gate, NEG_INF)
        gate = jnp.where(n < n_blk, gate, BELOW_ALL)
        picked = _rank_select(gate, n_blk, min(MOBA_TOP_K, n_blk - 1))
        sel_ref[0, h, 0] = jnp.where(n < q_blk, picked, 0.0)


def moba_gate(proj):
    b, s, _ = proj.shape
    return pl.pallas_call(
        _moba_gate_kernel,
        out_shape=jax.ShapeDtypeStruct((b, N_KV, s // TQ, BF16_ROWS, COLS), F32),
        grid=(b, s // TQ),
        in_specs=[pl.BlockSpec((1, TQ, MIX_WIDTH), lambda bi, i: (bi, i, COL_C_Q // MIX_WIDTH)),
                  pl.BlockSpec((1, s, KV_W), lambda bi, i: (bi, 0, COL_C_K // KV_W))],
        out_specs=pl.BlockSpec((1, N_KV, 1, BF16_ROWS, COLS), lambda bi, i: (bi, 0, i, 0, 0)),
        scratch_shapes=[pltpu.VMEM((N_KV, BF16_ROWS, HEAD_DIM), F32)],
        compiler_params=_cparams(("parallel", "arbitrary")),
        name="moba_gate",
    )(proj, proj)


def _flash_kernel(mode, *refs):
    n_in = 5 if mode == "win" else 6
    ins, o_ref, scr = refs[:n_in], refs[n_in], refs[n_in + 1:]
    if mode == "swa":
        c31_ref, sink_ref, band_ref, q_ref, k_ref, v_ref = ins
    elif mode == "win":
        c31_ref, band_ref, q_ref, k_ref, v_ref = ins
    else:
        c31_ref, band_ref, q_ref, k_ref, v_ref, msk_ref = ins
    kb_scr, vt_scr, m_scr, l_scr, acc_scr = (scr[i * N_KV:(i + 1) * N_KV] for i in range(5))
    qi = pl.program_id(1)
    n_kb = k_ref.shape[1] // TK
    head0 = MIXER[mode] * N_HEADS

    @pl.when(qi == 0)
    def _():
        for h in range(N_KV):
            cols = slice(h * HEAD_DIM, (h + 1) * HEAD_DIM)
            kb_scr[h][...] = k_ref[0, :, cols].astype(BF16)
            for j in range(n_kb):
                vt_scr[h][j] = v_ref[0, j * TK:(j + 1) * TK, cols].T.astype(BF16)

    q4 = [_stack_heads(q_ref[0, :, h * COLS:(h + 1) * COLS]).astype(BF16) for h in range(N_KV)]
    far_bias = [_head_row(c31_ref, head0 + h * GROUP) for h in range(N_KV)]

    def positions(n_keys):
        key = lax.broadcasted_iota(jnp.int32, (n_keys, COLS), 0)
        qry = lax.broadcasted_iota(jnp.int32, (n_keys, COLS), 1) & (TQ - 1)
        return key, qry

    def sel_valid(h, first_blk, n_blocks):
        per = TK // SEL_BLOCK
        rows = []
        for part in range(n_blocks * per):
            r = msk_ref[0, h, pl.ds(first_blk * per + part, 1), :]
            r = jnp.concatenate([r] * GROUP, axis=1)
            rows.append(jnp.broadcast_to(r, (SEL_BLOCK, COLS)))
        return jnp.concatenate(rows, axis=0) > 0.5

    def moba_row(h, key_blk):
        return msk_ref[0, h, 0, pl.ds((key_blk * TK) // MOBA_BLOCK, 1), :]

    def step(k0, n_blocks, bias, valid, first=False):
        n_keys = n_blocks * TK
        start = pl.multiple_of(k0 * TK, TK)
        for h in range(N_KV):
            s = _dot_nt(kb_scr[h][pl.ds(start, n_keys), :], q4[h]) * SCALE2 + bias[h]
            if valid is not None:
                s = jnp.where(valid[h], s, NEG_INF)
            m_new = jnp.max(s, axis=0, keepdims=True)
            if not first:
                m_prev = m_scr[h][...]
                m_new = jnp.maximum(m_prev, m_new)
                alpha = jnp.exp2(m_prev - m_new)
            p = jnp.exp2(s - m_new)
            p_sum = jnp.sum(p, axis=0, keepdims=True)
            pb = p.astype(BF16)
            pv = jnp.dot(vt_scr[h][k0], pb[0:TK], preferred_element_type=F32)
            for i in range(1, n_blocks):
                pv += jnp.dot(vt_scr[h][k0 + i], pb[i * TK:(i + 1) * TK], preferred_element_type=F32)
            if first:
                l_scr[h][...] = p_sum
                acc_scr[h][...] = pv
            else:
                l_scr[h][...] = alpha * l_scr[h][...] + p_sum
                acc_scr[h][...] = alpha * acc_scr[h][...] + pv
            m_scr[h][...] = m_new

    @pl.when(qi == 0)
    def _():
        key, qry = positions(TK)
        causal = key <= qry
        if mode == "sel":
            valid = [causal & sel_valid(h, 0, 1) for h in range(N_KV)]
        else:
            valid = [causal] * N_KV
        step(0, 1, [band_ref[1, :, h * COLS:(h + 1) * COLS] for h in range(N_KV)], valid, first=True)

    @pl.when(qi >= 1)
    def _():
        key, qry = positions(2 * TK)
        causal = key - TK <= qry
        if mode == "sel":
            valid = [causal & sel_valid(h, qi - 1, 2) for h in range(N_KV)]
        elif mode == "moba":
            own = (qi * TQ) // MOBA_BLOCK == ((qi - 1) * TK) // MOBA_BLOCK
            own_f = jnp.where(own, 1.0, 0.0)
            valid = [causal & ((key >= TK) | (jnp.broadcast_to(moba_row(h, qi - 1) + own_f, (2 * TK, COLS)) > 0.5))
                     for h in range(N_KV)]
        elif mode == "swa":
            valid = [causal & (key > qry)] * N_KV
        else:
            valid = [causal] * N_KV
        bias = [band_ref[:, :, h * COLS:(h + 1) * COLS].reshape(2 * TK, COLS) for h in range(N_KV)]
        step(qi - 1, 2, bias, valid, first=True)

    if mode in ("sel", "moba"):
        n_far = jnp.maximum(qi - 1, 0)

        def far_valid(k0, n_blocks):
            if mode == "sel":
                return [sel_valid(h, k0, n_blocks) for h in range(N_KV)]
            return [jnp.broadcast_to(moba_row(h, k0), (n_blocks * TK, COLS)) > 0.5 for h in range(N_KV)]

        def pair_body(pair, carry):
            step(2 * pair, 2, far_bias, far_valid(2 * pair, 2))
            return carry
        lax.fori_loop(0, n_far >> 1, pair_body, 0)

        @pl.when((n_far & 1) == 1)
        def _():
            step(n_far - 1, 1, far_bias, far_valid(n_far - 1, 1))
    elif mode == "win":
        n_full = NSA_WINDOW // TK

        @pl.when(qi >= 3)
        def _():
            step(qi - 3, 2, far_bias, None)

        @pl.when(qi == 2)
        def _():
            step(0, 1, far_bias, None)

        @pl.when(qi >= n_full)
        def _():
            key, qry = positions(TK)
            step(qi - n_full, 1, far_bias, [key > qry] * N_KV)

    for h in range(N_KV):
        m = m_scr[h][...]
        l = l_scr[h][...]
        acc = acc_scr[h][...]
        if mode == "swa":
            sink = _head_row(sink_ref, h * GROUP)
            m_fin = jnp.maximum(m, sink)
            shrink = jnp.exp2(m - m_fin)
            l = l * shrink + jnp.exp2(sink - m_fin)
            acc = acc * shrink
        out = acc / jnp.maximum(l, TINY)
        o_ref[0, :, h * COLS:(h + 1) * COLS] = _untranspose_heads(out).astype(o_ref.dtype)


def flash_attention(mode, proj, band, c31, *, q_col, k_col, v_col, out_dtype, mask=None, sinks=None):
    b, s, _ = proj.shape
    smem = pl.BlockSpec(memory_space=pltpu.SMEM)
    in_specs = [smem]
    args = [c31]
    if mode == "swa":
        in_specs.append(smem)
        args.append(sinks)
    in_specs += [pl.BlockSpec((2, TK, N_KV * COLS), lambda bi, i: (0, 0, MIXER[mode])),
                 pl.BlockSpec((1, TQ, MIX_WIDTH), lambda bi, i: (bi, i, q_col // MIX_WIDTH)),
                 pl.BlockSpec((1, s, KV_W), lambda bi, i: (bi, 0, k_col // KV_W)),
                 pl.BlockSpec((1, s, KV_W), lambda bi, i: (bi, 0, v_col // KV_W))]
    args += [band, proj, proj, proj]
    if mode == "sel":
        in_specs.append(pl.BlockSpec((1, N_KV, mask.shape[2], TQ), lambda bi, i: (bi, 0, 0, i)))
        args.append(mask)
    elif mode == "moba":
        in_specs.append(pl.BlockSpec((1, N_KV, 1, mask.shape[3], COLS), lambda bi, i: (bi, 0, i, 0, 0)))
        args.append(mask)
    per_head = [pltpu.VMEM((s, HEAD_DIM), BF16), pltpu.VMEM((s // TK, HEAD_DIM, TK), BF16),
                pltpu.VMEM((1, COLS), F32), pltpu.VMEM((1, COLS), F32), pltpu.VMEM((HEAD_DIM, COLS), F32)]
    return pl.pallas_call(
        functools.partial(_flash_kernel, mode),
        out_shape=jax.ShapeDtypeStruct((b, s, MIX_WIDTH), out_dtype),
        grid=(b, s // TQ),
        in_specs=in_specs,
        out_specs=pl.BlockSpec((1, TQ, MIX_WIDTH), lambda bi, i: (bi, i, 0)),
        scratch_shapes=[shape for shape in per_head for _ in range(N_KV)],
        compiler_params=_cparams(("parallel", "arbitrary")),
        name="flash_" + mode,
    )(*args)


def _nsa_combine_kernel(cmp_ref, sel_ref, win_ref, gate_ref, o_ref):
    gate = jax.nn.sigmoid(gate_ref[...])
    for h in range(N_HEADS):
        cols = slice(h * HEAD_DIM, (h + 1) * HEAD_DIM)
        out = (gate[:, 3 * h:3 * h + 1] * cmp_ref[:, cols]
               + gate[:, 3 * h + 1:3 * h + 2] * sel_ref[:, cols]
               + gate[:, 3 * h + 2:3 * h + 3] * win_ref[:, cols])
        o_ref[:, cols] = out.astype(BF16)


def nsa_combine(o_cmp, o_sel, o_win, proj, *, tm=512):
    t = o_cmp.shape[0]
    o_spec = pl.BlockSpec((tm, MIX_WIDTH), lambda i: (i, 0))
    return pl.pallas_call(
        _nsa_combine_kernel,
        out_shape=jax.ShapeDtypeStruct((t, MIX_WIDTH), BF16),
        grid=(t // tm,),
        in_specs=[o_spec, o_spec, o_spec,
                  pl.BlockSpec((tm, LANES), lambda i: (i, COL_A_GATE // LANES))],
        out_specs=o_spec,
        compiler_params=_cparams(("parallel",)),
        name="nsa_combine",
    )(o_cmp, o_sel, o_win, proj)


def _selection_share(n_sel, n_half):
    blk = np.arange(n_sel)[:, None] * SEL_BLOCK
    starts = np.arange(n_half)[None, :] * CMP_STRIDE
    shared = np.clip(np.minimum(starts + CMP_LEN, blk + SEL_BLOCK) - np.maximum(starts, blk), 0, None)
    shared = shared / CMP_STRIDE
    shared[:, n_half - 1] = 0.0
    return shared.astype(np.float32)


def _rearranged_w_in(w):
    d = w.shape[0]
    q_w, kv_w = MIX_WIDTH, KV_W
    a_q = 0
    a_kv = a_q + q_w
    a_gate = a_kv + 6 * kv_w
    b_q = a_gate + ORIG_GATE_W
    b_kv = b_q + q_w
    c_q = b_kv + 2 * kv_w
    c_kv = c_q + q_w
    merge = c_kv + 2 * kv_w
    gate_pad = COL_MERGE - COL_A_GATE - ORIG_GATE_W
    return jnp.concatenate([
        w[:, a_q:a_q + q_w], w[:, b_q:b_q + q_w], w[:, c_q:c_q + q_w],
        w[:, a_kv:a_gate], w[:, b_kv:c_q], w[:, c_kv:merge],
        w[:, a_gate:b_q], jnp.zeros((d, gate_pad), w.dtype),
        w[:, merge:],
    ], axis=1).astype(BF16)


def kernel(x, w_in, cmp_pos, cmp_w1, cmp_w2, swa_sinks, w_branch, w_out, w_mlp_in, w_mlp_out,
           norm_mix, norm_mlp, norm_final, rel_bias):
    b, s, d = x.shape
    depth = w_in.shape[0]
    t = b * s
    n_half = s // CMP_STRIDE
    n_sel = s // SEL_BLOCK

    tbl_flat = rel_bias.reshape(-1)
    c31 = rel_bias[N_BUCKETS - 1]
    band = band_bias(tbl_flat)
    bias_c = cmp_bias(tbl_flat, s)
    share = jnp.asarray(_selection_share(n_sel, n_half), BF16)

    xt = x.reshape(t, d)
    for layer in range(depth):
        proj2d = norm_matmul(xt, norm_mix[layer], _rearranged_w_in(w_in[layer]))
        proj = proj2d.reshape(b, s, N_PROJ)

        halves = proj[:, :, COL_A_KC:COL_A_KC + 2 * KV_W]
        halves = halves.reshape(b, n_half, CMP_STRIDE, 2, N_KV, HEAD_DIM)
        halves = halves.transpose(0, 3, 4, 1, 2, 5).reshape(b, 2, N_KV, n_half, CMP_STRIDE * HEAD_DIM)
        cmp_tokens = nsa_compress(halves, cmp_pos[layer].reshape(2, 2, CMP_STRIDE * HEAD_DIM),
                                  cmp_w1[layer].astype(BF16), cmp_w2[layer].astype(BF16))
        o_cmp, sel_mask = nsa_compressed(bias_c, proj, cmp_tokens, share)
        o_sel = flash_attention("sel", proj, band, c31, q_col=COL_A_Q, k_col=COL_A_KS, v_col=COL_A_VS,
                                out_dtype=F32, mask=sel_mask)
        o_win = flash_attention("win", proj, band, c31, q_col=COL_A_Q, k_col=COL_A_KW, v_col=COL_A_VW,
                                out_dtype=F32)
        o_a = nsa_combine(o_cmp.reshape(t, MIX_WIDTH), o_sel.reshape(t, MIX_WIDTH),
                          o_win.reshape(t, MIX_WIDTH), proj2d)

        o_b = flash_attention("swa", proj, band, c31, q_col=COL_B_Q, k_col=COL_B_K, v_col=COL_B_V,
                              out_dtype=BF16, sinks=swa_sinks[layer])
        moba_mask = moba_gate(proj)
        o_c = flash_attention("moba", proj, band, c31, q_col=COL_C_Q, k_col=COL_C_K, v_col=COL_C_V,
                              out_dtype=BF16, mask=moba_mask)

        z = merge_branches(o_a, o_b.reshape(t, MIX_WIDTH), o_c.reshape(t, MIX_WIDTH),
                           w_branch[layer].astype(BF16), proj2d)
        xt = matmul_residual(z, w_out[layer].astype(BF16), xt)
        xt = mlp_block(xt, norm_mlp[layer], w_mlp_in[layer].astype(BF16), w_mlp_out[layer].astype(BF16))

    return final_norm(xt, norm_final).reshape(b, s, d)
```

```python
import functools
import math

import numpy as np
import jax
import jax.numpy as jnp
from jax import lax
from jax.experimental import pallas as pl
from jax.experimental.pallas import tpu as pltpu

F32 = jnp.float32
BF16 = jnp.bfloat16

D_MODEL = 2048
HEAD_DIM = 128
N_HEADS = 8
N_KV = 2
GROUP = N_HEADS // N_KV
MIX_WIDTH = N_HEADS * HEAD_DIM
CMP_LEN = 32
CMP_STRIDE = 16
CMP_HIDDEN = 2 * HEAD_DIM
SEL_BLOCK = 64
SEL_TOP_N = 16
NSA_WINDOW = 512
SWA_WINDOW = 128
MOBA_BLOCK = 256
MOBA_TOP_K = 3
N_BUCKETS = 32
BUCKET_EXACT = N_BUCKETS // 2
BUCKET_MAX_DIST = 128
TOTAL_HEADS = 3 * N_HEADS
D_FF = 4 * D_MODEL
RMS_EPS = 1e-6
NEG_INF = -1e30
FORCE_SCORE = 1e30
TINY = 1e-30
BELOW_ALL = -3e38
SCALE = HEAD_DIM ** -0.5

LANES = 128
BF16_ROWS = 16
VMEM_LIMIT = 56 * 1024 * 1024
TQ = 128
TK = 128
COLS = GROUP * TQ
KV_W = N_KV * HEAD_DIM
LOG2E = math.log2(math.e)
SCALE2 = SCALE * LOG2E

COL_A_Q = 0
COL_B_Q = 1024
COL_C_Q = 2048
COL_A_KC = 3072
COL_A_KS = 3584
COL_A_VS = 3840
COL_A_KW = 4096
COL_A_VW = 4352
COL_B_K = 4608
COL_B_V = 4864
COL_C_K = 5120
COL_C_V = 5376
COL_A_GATE = 5632
COL_MERGE = 6144
N_PROJ = COL_MERGE + 3 * D_MODEL
ORIG_GATE_W = 3 * N_HEADS

MIXER = {"sel": 0, "win": 0, "swa": 1, "moba": 2}


def _cparams(semantics):
    return pltpu.CompilerParams(dimension_semantics=semantics, vmem_limit_bytes=VMEM_LIMIT)


def _rms(x, gain):
    y = x * lax.rsqrt(jnp.mean(x * x, axis=-1, keepdims=True) + RMS_EPS)
    return y * gain


def _norm_matmul_kernel(x_ref, g_ref, w_ref, o_ref, h_scr):
    @pl.when(pl.program_id(1) == 0)
    def _():
        h_scr[...] = _rms(x_ref[...], g_ref[...]).astype(BF16)

    o_ref[...] = jnp.dot(h_scr[...], w_ref[...], preferred_element_type=F32)


def norm_matmul(x, gain, w, *, tm=1024, tn=512):
    t, k = x.shape
    n = w.shape[1]
    return pl.pallas_call(
        _norm_matmul_kernel,
        out_shape=jax.ShapeDtypeStruct((t, n), F32),
        grid=(t // tm, n // tn),
        in_specs=[pl.BlockSpec((tm, k), lambda i, j: (i, 0)),
                  pl.BlockSpec((1, k), lambda i, j: (0, 0)),
                  pl.BlockSpec((k, tn), lambda i, j: (0, j))],
        out_specs=pl.BlockSpec((tm, tn), lambda i, j: (i, j)),
        scratch_shapes=[pltpu.VMEM((tm, k), BF16)],
        compiler_params=_cparams(("parallel", "arbitrary")),
        name="norm_matmul",
    )(x, gain.reshape(1, k), w)


def _merge_kernel(oa_ref, ob_ref, oc_ref, wb_ref, g0_ref, g1_ref, g2_ref, z_ref, w_scr):
    @pl.when(pl.program_id(1) == 0)
    def _():
        w_scr[...] = wb_ref[0].astype(BF16)

    acc = jax.nn.sigmoid(g0_ref[...]) * jnp.dot(oa_ref[...], w_scr[0], preferred_element_type=F32)
    acc += jax.nn.sigmoid(g1_ref[...]) * jnp.dot(ob_ref[...], w_scr[1], preferred_element_type=F32)
    acc += jax.nn.sigmoid(g2_ref[...]) * jnp.dot(oc_ref[...], w_scr[2], preferred_element_type=F32)
    z_ref[...] = acc.astype(BF16)


def merge_branches(o_a, o_b, o_c, w_branch, layer, proj, *, tm=512, tn=512):
    t = o_a.shape[0]
    gate_blk = COL_MERGE // tn
    per_branch = D_MODEL // tn
    o_spec = pl.BlockSpec((tm, MIX_WIDTH), lambda j, i: (i, 0))

    def gate_spec(m):
        return pl.BlockSpec((tm, tn), lambda j, i: (i, gate_blk + m * per_branch + j))

    return pl.pallas_call(
        _merge_kernel,
        out_shape=jax.ShapeDtypeStruct((t, D_MODEL), BF16),
        grid=(D_MODEL // tn, t // tm),
        in_specs=[o_spec, o_spec, o_spec,
                  pl.BlockSpec((1, 3, MIX_WIDTH, tn), lambda j, i: (layer, 0, 0, j)),
                  gate_spec(0), gate_spec(1), gate_spec(2)],
        out_specs=pl.BlockSpec((tm, tn), lambda j, i: (i, j)),
        scratch_shapes=[pltpu.VMEM((3, MIX_WIDTH, tn), BF16)],
        compiler_params=_cparams(("parallel", "arbitrary")),
        name="merge_branches",
    )(o_a, o_b, o_c, w_branch, proj, proj, proj)


def _matmul_res_kernel(a_ref, w_ref, x_ref, o_ref, w_scr):
    @pl.when(pl.program_id(1) == 0)
    def _():
        w_scr[...] = w_ref[0].astype(BF16)

    o_ref[...] = x_ref[...] + jnp.dot(a_ref[...], w_scr[...], preferred_element_type=F32)


def matmul_residual(a, w, layer, x, *, tm=1024, tn=512):
    t, k = a.shape
    n = w.shape[2]
    return pl.pallas_call(
        _matmul_res_kernel,
        out_shape=jax.ShapeDtypeStruct((t, n), F32),
        grid=(n // tn, t // tm),
        in_specs=[pl.BlockSpec((tm, k), lambda j, i: (i, 0)),
                  pl.BlockSpec((1, k, tn), lambda j, i: (layer, 0, j)),
                  pl.BlockSpec((tm, tn), lambda j, i: (i, j))],
        out_specs=pl.BlockSpec((tm, tn), lambda j, i: (i, j)),
        scratch_shapes=[pltpu.VMEM((k, tn), BF16)],
        compiler_params=_cparams(("parallel", "arbitrary")),
        name="matmul_residual",
    )(a, w, x)


def _mlp_kernel(x_ref, g_ref, w1_ref, w2_ref, o_ref, h_scr):
    @pl.when(pl.program_id(1) == 0)
    def _():
        x = x_ref[...]
        h_scr[...] = _rms(x, g_ref[...]).astype(BF16)
        o_ref[...] = x

    u = jnp.dot(h_scr[...], w1_ref[0].astype(BF16), preferred_element_type=F32)
    u = jnp.square(jnp.maximum(u, 0.0)).astype(BF16)
    o_ref[...] += jnp.dot(u, w2_ref[0].astype(BF16), preferred_element_type=F32)


def mlp_block(x, gain, w1, w2, layer, *, tm=1024, tf=512):
    t, d = x.shape
    dff = w1.shape[2]
    return pl.pallas_call(
        _mlp_kernel,
        out_shape=jax.ShapeDtypeStruct((t, d), F32),
        grid=(t // tm, dff // tf),
        in_specs=[pl.BlockSpec((tm, d), lambda i, f: (i, 0), pipeline_mode=pl.Buffered(1)),
                  pl.BlockSpec((1, d), lambda i, f: (0, 0)),
                  pl.BlockSpec((1, d, tf), lambda i, f: (layer, 0, f)),
                  pl.BlockSpec((1, tf, d), lambda i, f: (layer, f, 0))],
        out_specs=pl.BlockSpec((tm, d), lambda i, f: (i, 0)),
        scratch_shapes=[pltpu.VMEM((tm, d), BF16)],
        compiler_params=_cparams(("parallel", "arbitrary")),
        name="mlp_block",
    )(x, gain.reshape(1, d), w1, w2)


def _final_norm_kernel(x_ref, g_ref, o_ref):
    o_ref[...] = _rms(x_ref[...], g_ref[...])


def final_norm(x, gain, *, tm=512):
    t, d = x.shape
    return pl.pallas_call(
        _final_norm_kernel,
        out_shape=jax.ShapeDtypeStruct((t, d), F32),
        grid=(t // tm,),
        in_specs=[pl.BlockSpec((tm, d), lambda i: (i, 0)),
                  pl.BlockSpec((1, d), lambda i: (0, 0))],
        out_specs=pl.BlockSpec((tm, d), lambda i: (i, 0)),
        compiler_params=_cparams(("parallel",)),
        name="final_norm",
    )(x, gain.reshape(1, d))


def _t5_bucket(n):
    log_ratio = jnp.log(jnp.maximum(n, 1).astype(F32) / BUCKET_EXACT) / math.log(BUCKET_MAX_DIST / BUCKET_EXACT)
    large = jnp.minimum(BUCKET_EXACT + (log_ratio * (N_BUCKETS - BUCKET_EXACT)).astype(jnp.int32), N_BUCKETS - 1)
    return jnp.where(n < BUCKET_EXACT, n, large)


def _lookup_bias(tbl_ref, head, n):
    bucket = _t5_bucket(n)
    out = jnp.zeros(n.shape, F32)
    for b in range(N_BUCKETS):
        out = jnp.where(bucket == b, tbl_ref[b * TOTAL_HEADS + head], out)
    return out


def _band_bias_kernel(tbl_ref, o_ref):
    l = lax.broadcasted_iota(jnp.int32, (TK, TQ), 0)
    q = lax.broadcasted_iota(jnp.int32, (TK, TQ), 1)
    head = pl.program_id(0)
    prev = _lookup_bias(tbl_ref, head, jnp.maximum(q + TK - l, 0)) * LOG2E
    diag = _lookup_bias(tbl_ref, head, jnp.maximum(q - l, 0)) * LOG2E
    sliding = (head >= MIXER["swa"] * N_HEADS) & (head < (MIXER["swa"] + 1) * N_HEADS)
    o_ref[0] = jnp.where(sliding & (q + TK - l >= SWA_WINDOW), NEG_INF, prev)
    o_ref[1] = jnp.where(l > q, NEG_INF, diag)


def band_bias(tbl_flat):
    return pl.pallas_call(
        _band_bias_kernel,
        out_shape=jax.ShapeDtypeStruct((2, TK, TOTAL_HEADS * TQ), F32),
        grid=(TOTAL_HEADS,),
        in_specs=[pl.BlockSpec(memory_space=pltpu.SMEM)],
        out_specs=pl.BlockSpec((2, TK, TQ), lambda h: (0, 0, h)),
        compiler_params=_cparams(("parallel",)),
        name="band_bias",
    )(tbl_flat)


def _cmp_bias_kernel(tbl_ref, o_ref):
    c = lax.broadcasted_iota(jnp.int32, (LANES, TQ), 0)
    t = pl.program_id(0) * TQ + lax.broadcasted_iota(jnp.int32, (LANES, TQ), 1)
    o_ref[0] = _lookup_bias(tbl_ref, pl.program_id(1), jnp.maximum(t - (c * CMP_STRIDE + CMP_LEN - 1), 0))


def cmp_bias(tbl_flat, s):
    return pl.pallas_call(
        _cmp_bias_kernel,
        out_shape=jax.ShapeDtypeStruct((s // TQ, LANES, N_HEADS * TQ), F32),
        grid=(s // TQ, N_HEADS),
        in_specs=[pl.BlockSpec(memory_space=pltpu.SMEM)],
        out_specs=pl.BlockSpec((1, LANES, TQ), lambda i, h: (i, 0, h)),
        compiler_params=_cparams(("parallel", "parallel")),
        name="cmp_bias",
    )(tbl_flat)


def _compress_kernel(half_ref, pos_ref, w1_ref, w2_ref, o_ref):
    half = half_ref[0, 0, 0]
    half_w = CMP_STRIDE * HEAD_DIM
    xa = (half + pos_ref[0, 0:1, :]).astype(BF16)
    xb = (half + pos_ref[0, 1:2, :]).astype(BF16)
    ha = jnp.dot(xa, w1_ref[0, 0, 0:half_w, :].astype(BF16), preferred_element_type=F32)
    hb = jnp.dot(xb, w1_ref[0, 0, half_w:2 * half_w, :].astype(BF16), preferred_element_type=F32)
    n_half = hb.shape[0]
    hidden = ha + pltpu.roll(hb, n_half - 1, 0)
    act = jax.nn.gelu(hidden, approximate=True).astype(BF16)
    o_ref[0, 0, 0] = jnp.dot(act, w2_ref[0, 0].astype(BF16), preferred_element_type=F32)


def nsa_compress(halves, pos, w1, w2, layer):
    b, _, n_kv, n_half, width = halves.shape
    return pl.pallas_call(
        _compress_kernel,
        out_shape=jax.ShapeDtypeStruct((b, 2, n_kv, n_half, HEAD_DIM), F32),
        grid=(2, b, n_kv),
        in_specs=[pl.BlockSpec((1, 1, 1, n_half, width), lambda kv, bi, h: (bi, kv, h, 0, 0)),
                  pl.BlockSpec((1, 2, width), lambda kv, bi, h: (kv, 0, 0)),
                  pl.BlockSpec((1, 1, 2 * width, CMP_HIDDEN), lambda kv, bi, h: (layer, kv, 0, 0)),
                  pl.BlockSpec((1, 1, CMP_HIDDEN, HEAD_DIM), lambda kv, bi, h: (layer, kv, 0, 0))],
        out_specs=pl.BlockSpec((1, 1, 1, n_half, HEAD_DIM), lambda kv, bi, h: (bi, kv, h, 0, 0)),
        compiler_params=_cparams(("parallel", "parallel", "parallel")),
        name="nsa_compress",
    )(halves, pos, w1, w2)


def _stack_heads(q):
    return jnp.concatenate([q[:, g * HEAD_DIM:(g + 1) * HEAD_DIM] for g in range(GROUP)], axis=0)


def _untranspose_heads(o_t):
    return jnp.concatenate([o_t[:, g * TQ:(g + 1) * TQ].T for g in range(GROUP)], axis=1)


def _head_row(ref, first):
    return jnp.concatenate([jnp.full((1, TQ), ref[first + g] * LOG2E, F32) for g in range(GROUP)], axis=1)


def _split3(x):
    x1 = x.astype(BF16)
    r1 = x - x1.astype(F32)
    x2 = r1.astype(BF16)
    x3 = (r1 - x2.astype(F32)).astype(BF16)
    return x1, x2, x3


def _dot_nt(a, b):
    return lax.dot_general(a, b, (((1,), (1,)), ((), ())), preferred_element_type=F32)


def _rank_select(score, n_cand, n_top):
    idx = lax.broadcasted_iota(jnp.int32, score.shape, 0)
    rank = jnp.zeros(score.shape, jnp.int32)
    for j in range(n_cand):
        other = score[j:j + 1, :]
        ahead = (other > score) | ((other == score) & (j < idx))
        rank += jnp.where(ahead, 1, 0)
    return jnp.where((rank < n_top) & (idx < n_cand), 1.0, 0.0)


def _nsa_cmp_kernel(bias_ref, q_ref, kc_ref, vc_ref, share_ref, o_ref, sel_ref):
    qi = pl.program_id(1)
    share = share_ref[...]
    n_sel = share.shape[0]
    c = lax.broadcasted_iota(jnp.int32, (LANES, COLS), 0)
    t = qi * TQ + (lax.broadcasted_iota(jnp.int32, (LANES, COLS), 1) & (TQ - 1))
    valid = t - (c * CMP_STRIDE + CMP_LEN - 1) >= 0
    blk = lax.broadcasted_iota(jnp.int32, (n_sel, TQ), 0)
    cur = (qi * TQ + lax.broadcasted_iota(jnp.int32, (n_sel, TQ), 1)) // SEL_BLOCK
    forced = (blk == 0) | (blk == cur) | (blk == cur - 1)
    for h in range(N_KV):
        cols = slice(h * COLS, (h + 1) * COLS)
        q4 = _stack_heads(q_ref[0, :, cols]).astype(BF16)
        kc = kc_ref[0, 0, h].astype(BF16)
        vc_t = vc_ref[0, 0, h].T.astype(BF16)
        logits = _dot_nt(kc, q4) * SCALE + bias_ref[0, :, cols]
        logits = jnp.where(valid, logits, NEG_INF)
        m = jnp.max(logits, axis=0, keepdims=True)
        p = jnp.where(valid, jnp.exp(logits - m), 0.0)
        p = p / jnp.maximum(jnp.sum(p, axis=0, keepdims=True), TINY)
        o_ref[0, :, cols] = _untranspose_heads(jnp.dot(vc_t, p.astype(BF16), preferred_element_type=F32))

        p_sum = p[:, 0:TQ]
        for g in range(1, GROUP):
            p_sum = p_sum + p[:, g * TQ:(g + 1) * TQ]
        importance = sum(jnp.dot(share, part, preferred_element_type=F32) for part in _split3(p_sum))
        score = jnp.where(forced, FORCE_SCORE, jnp.where(blk <= cur, importance, NEG_INF))
        sel_ref[0, h] = _rank_select(score, n_sel, min(SEL_TOP_N, n_sel))


def nsa_compressed(bias_c, proj, cmp_tokens, share):
    b, s, _ = proj.shape
    n_cmp = cmp_tokens.shape[3]
    n_sel = share.shape[0]
    return pl.pallas_call(
        _nsa_cmp_kernel,
        out_shape=(jax.ShapeDtypeStruct((b, s, MIX_WIDTH), F32),
                   jax.ShapeDtypeStruct((b, N_KV, n_sel, s), F32)),
        grid=(b, s // TQ),
        in_specs=[pl.BlockSpec((1, n_cmp, N_KV * COLS), lambda bi, i: (i, 0, 0)),
                  pl.BlockSpec((1, TQ, MIX_WIDTH), lambda bi, i: (bi, i, COL_A_Q // MIX_WIDTH)),
                  pl.BlockSpec((1, 1, N_KV, n_cmp, HEAD_DIM), lambda bi, i: (bi, 0, 0, 0, 0)),
                  pl.BlockSpec((1, 1, N_KV, n_cmp, HEAD_DIM), lambda bi, i: (bi, 1, 0, 0, 0)),
                  pl.BlockSpec((n_sel, n_cmp), lambda bi, i: (0, 0))],
        out_specs=(pl.BlockSpec((1, TQ, MIX_WIDTH), lambda bi, i: (bi, i, 0)),
                   pl.BlockSpec((1, N_KV, n_sel, TQ), lambda bi, i: (bi, 0, 0, i))),
        compiler_params=_cparams(("parallel", "parallel")),
        name="nsa_compressed",
    )(bias_c, proj, cmp_tokens, cmp_tokens, share)


def _moba_gate_kernel(q_ref, k_ref, sel_ref, km_scr):
    qi = pl.program_id(1)
    n_blk = k_ref.shape[1] // MOBA_BLOCK

    @pl.when(qi == 0)
    def _():
        km_scr[...] = jnp.zeros_like(km_scr)
        for h in range(N_KV):
            for n in range(n_blk):
                blk = k_ref[0, n * MOBA_BLOCK:(n + 1) * MOBA_BLOCK, h * HEAD_DIM:(h + 1) * HEAD_DIM]
                km_scr[h, n:n + 1, :] = jnp.sum(blk, axis=0, keepdims=True) / MOBA_BLOCK

    n = lax.broadcasted_iota(jnp.int32, (BF16_ROWS, COLS), 0)
    q_blk = (qi * TQ) // MOBA_BLOCK
    for h in range(N_KV):
        q4 = _stack_heads(q_ref[0, :, h * COLS:(h + 1) * COLS])
        q_hi, q_lo, _ = _split3(q4)
        k_hi, k_lo, _ = _split3(km_scr[h])
        gate = _dot_nt(k_hi, q_hi) + _dot_nt(k_lo, q_hi) + _dot_nt(k_hi, q_lo)
        gate = jnp.where(n < q_blk, gate, NEG_INF)
        gate = jnp.where(n < n_blk, gate, BELOW_ALL)
        picked = _rank_select(gate, n_blk, min(MOBA_TOP_K, n_blk - 1))
        sel_ref[0, h, 0] = jnp.where(n < q_blk, picked, 0.0)


def moba_gate(proj):
    b, s, _ = proj.shape
    return pl.pallas_call(
        _moba_gate_kernel,
        out_shape=jax.ShapeDtypeStruct((b, N_KV, s // TQ, BF16_ROWS, COLS), F32),
        grid=(b, s // TQ),
        in_specs=[pl.BlockSpec((1, TQ, MIX_WIDTH), lambda bi, i: (bi, i, COL_C_Q // MIX_WIDTH)),
                  pl.BlockSpec((1, s, KV_W), lambda bi, i: (bi, 0, COL_C_K // KV_W))],
        out_specs=pl.BlockSpec((1, N_KV, 1, BF16_ROWS, COLS), lambda bi, i: (bi, 0, i, 0, 0)),
        scratch_shapes=[pltpu.VMEM((N_KV, BF16_ROWS, HEAD_DIM), F32)],
        compiler_params=_cparams(("parallel", "arbitrary")),
        name="moba_gate",
    )(proj, proj)


def _flash_kernel(mode, *refs):
    n_in = 5 if mode == "win" else 6
    ins, o_ref, scr = refs[:n_in], refs[n_in], refs[n_in + 1:]
    if mode == "swa":
        c31_ref, sink_ref, band_ref, q_ref, k_ref, v_ref = ins
    elif mode == "win":
        c31_ref, band_ref, q_ref, k_ref, v_ref = ins
    else:
        c31_ref, band_ref, q_ref, k_ref, v_ref, msk_ref = ins
    kb_scr, vt_scr, m_scr, l_scr, acc_scr, alpha_scr, pb_scr = (scr[i * N_KV:(i + 1) * N_KV] for i in range(7))
    pend_ref = scr[7 * N_KV]
    s_scr = scr[7 * N_KV + 1:8 * N_KV + 1]
    mb_scr = scr[8 * N_KV + 1:]
    qi = pl.program_id(1)
    n_kb = k_ref.shape[1] // TK
    head0 = MIXER[mode] * N_HEADS

    @pl.when(qi == 0)
    def _():
        for h in range(N_KV):
            cols = slice(h * HEAD_DIM, (h + 1) * HEAD_DIM)
            kb_scr[h][...] = k_ref[0, :, cols].astype(BF16)
            for j in range(n_kb):
                vt_scr[h][j] = v_ref[0, j * TK:(j + 1) * TK, cols].T.astype(BF16)

    q4 = [_stack_heads(q_ref[0, :, h * COLS:(h + 1) * COLS]).astype(BF16) for h in range(N_KV)]
    far_bias = [_head_row(c31_ref, head0 + h * GROUP) for h in range(N_KV)]

    def penalty(keep):
        return (keep - 1.0) * (-NEG_INF)

    def expand(rows, seg):
        return jnp.concatenate([jnp.broadcast_to(r, (seg, COLS)) for r in rows], axis=0)

    def sel_rows(h, first_blk, n_blocks):
        per = TK // SEL_BLOCK
        rows = []
        for part in range(n_blocks * per):
            r = msk_ref[0, h, pl.ds(first_blk * per + part, 1), :]
            rows.append(jnp.concatenate([penalty(r)] * GROUP, axis=1))
        return rows

    def moba_row(h, key_blk):
        return msk_ref[0, h, 0, pl.ds((key_blk * TK) // MOBA_BLOCK, 1), :]

    def far_bias_tile(h, k0, n_blocks):
        if mode == "sel":
            return expand([far_bias[h] + r for r in sel_rows(h, k0, n_blocks)], SEL_BLOCK)
        if mode == "moba":
            return expand([far_bias[h] + penalty(moba_row(h, k0))], n_blocks * TK)
        return far_bias[h]

    def scores(h, k0, n_blocks, bias):
        start = pl.multiple_of(k0 * TK, TK)
        return _dot_nt(kb_scr[h][pl.ds(start, n_blocks * TK), :], q4[h]) * SCALE2 + bias

    def flush():
        k0 = pend_ref[0]
        for h in range(N_KV):
            v_t = jnp.concatenate([vt_scr[h][k0], vt_scr[h][k0 + 1]], axis=1)
            pv = jnp.dot(v_t, pb_scr[h][...], preferred_element_type=F32)
            acc_scr[h][...] = alpha_scr[h][...] * acc_scr[h][...] + pv

    def softmax(h, s, m_blk, first):
        n_keys = s.shape[0]
        if first:
            m_new = m_blk
        else:
            m_prev = m_scr[h][...]
            m_new = jnp.maximum(m_prev, m_blk)
            alpha = jnp.exp2(m_prev - m_new)
        p = jnp.exp2(s - m_new)
        p_sum = jnp.sum(p, axis=0, keepdims=True)
        pb_scr[h][0:n_keys, :] = p.astype(BF16)
        if n_keys == TK:
            pb_scr[h][TK:2 * TK, :] = jnp.zeros((TK, COLS), BF16)
        if first:
            l_scr[h][...] = p_sum
            alpha_scr[h][...] = jnp.zeros((1, COLS), F32)
            acc_scr[h][...] = jnp.zeros((HEAD_DIM, COLS), F32)
        else:
            l_scr[h][...] = alpha * l_scr[h][...] + p_sum
            alpha_scr[h][...] = alpha
        m_scr[h][...] = m_new

    def step(k0, n_blocks, bias, first=False):
        if not first:
            flush()
        for h in range(N_KV):
            s = scores(h, k0, n_blocks, bias[h])
            softmax(h, s, jnp.max(s, axis=0, keepdims=True), first)
        pend_ref[0] = k0

    @pl.when(qi == 0)
    def _():
        bias = [band_ref[1, :, h * COLS:(h + 1) * COLS] for h in range(N_KV)]
        if mode == "sel":
            bias = [bias[h] + expand(sel_rows(h, 0, 1), SEL_BLOCK) for h in range(N_KV)]
        step(0, 1, bias, first=True)

    @pl.when(qi >= 1)
    def _():
        bias = [band_ref[:, :, h * COLS:(h + 1) * COLS].reshape(2 * TK, COLS) for h in range(N_KV)]
        if mode == "sel":
            bias = [bias[h] + expand(sel_rows(h, qi - 1, 2), SEL_BLOCK) for h in range(N_KV)]
        elif mode == "moba":
            own = (qi * TQ) // MOBA_BLOCK == ((qi - 1) * TK) // MOBA_BLOCK
            own_f = jnp.where(own, 1.0, 0.0)
            bias = [bias[h] + expand([penalty(jnp.minimum(moba_row(h, qi - 1) + own_f, 1.0)),
                                      jnp.zeros((1, COLS), F32)], TK) for h in range(N_KV)]
        step(qi - 1, 2, bias, first=True)

    masked_rows = jnp.full((TK, COLS), NEG_INF, F32)
    if mode in ("sel", "moba"):
        n_far = jnp.maximum(qi - 1, 0)
        n_pairs = n_far >> 1

        def produce(pair):
            for h in range(N_KV):
                s = scores(h, 2 * pair, 2, far_bias_tile(h, 2 * pair, 2))
                s_scr[h][...] = s
                mb_scr[h][...] = jnp.max(s, axis=0, keepdims=True)

        @pl.when(n_pairs > 0)
        def _():
            produce(0)

        def pair_body(pair, carry):
            flush()
            for h in range(N_KV):
                softmax(h, s_scr[h][...], mb_scr[h][...], False)
            pend_ref[0] = 2 * pair
            produce(jnp.minimum(pair + 1, n_pairs - 1))
            return carry
        lax.fori_loop(0, n_pairs, pair_body, 0)

        @pl.when((n_far & 1) == 1)
        def _():
            step(n_far - 1, 2, [jnp.concatenate([far_bias_tile(h, n_far - 1, 1), masked_rows], axis=0)
                                for h in range(N_KV)])
    elif mode == "win":
        n_full = NSA_WINDOW // TK

        @pl.when(qi >= 3)
        def _():
            step(qi - 3, 2, far_bias)

        @pl.when(qi == 2)
        def _():
            step(0, 2, [jnp.concatenate([jnp.broadcast_to(far_bias[h], (TK, COLS)), masked_rows], axis=0)
                        for h in range(N_KV)])

        @pl.when(qi >= n_full)
        def _():
            key = lax.broadcasted_iota(jnp.int32, (TK, COLS), 0)
            qry = lax.broadcasted_iota(jnp.int32, (TK, COLS), 1) & (TQ - 1)
            step(qi - n_full, 2, [jnp.concatenate([jnp.where(key > qry, far_bias[h], NEG_INF), masked_rows], axis=0)
                                  for h in range(N_KV)])

    flush()
    for h in range(N_KV):
        m = m_scr[h][...]
        l = l_scr[h][...]
        acc = acc_scr[h][...]
        if mode == "swa":
            sink = _head_row(sink_ref, h * GROUP)
            m_fin = jnp.maximum(m, sink)
            shrink = jnp.exp2(m - m_fin)
            l = l * shrink + jnp.exp2(sink - m_fin)
            acc = acc * shrink
        out = acc / jnp.maximum(l, TINY)
        o_ref[0, :, h * COLS:(h + 1) * COLS] = _untranspose_heads(out).astype(o_ref.dtype)


def flash_attention(mode, proj, band, c31, *, q_col, k_col, v_col, out_dtype, mask=None, sinks=None):
    b, s, _ = proj.shape
    smem = pl.BlockSpec(memory_space=pltpu.SMEM)
    in_specs = [smem]
    args = [c31]
    if mode == "swa":
        in_specs.append(smem)
        args.append(sinks)
    in_specs += [pl.BlockSpec((2, TK, N_KV * COLS), lambda bi, i: (0, 0, MIXER[mode])),
                 pl.BlockSpec((1, TQ, MIX_WIDTH), lambda bi, i: (bi, i, q_col // MIX_WIDTH)),
                 pl.BlockSpec((1, s, KV_W), lambda bi, i: (bi, 0, k_col // KV_W)),
                 pl.BlockSpec((1, s, KV_W), lambda bi, i: (bi, 0, v_col // KV_W))]
    args += [band, proj, proj, proj]
    if mode == "sel":
        in_specs.append(pl.BlockSpec((1, N_KV, mask.shape[2], TQ), lambda bi, i: (bi, 0, 0, i)))
        args.append(mask)
    elif mode == "moba":
        in_specs.append(pl.BlockSpec((1, N_KV, 1, mask.shape[3], COLS), lambda bi, i: (bi, 0, i, 0, 0)))
        args.append(mask)
    per_head = [pltpu.VMEM((s, HEAD_DIM), BF16), pltpu.VMEM((s // TK, HEAD_DIM, TK), BF16),
                pltpu.VMEM((1, COLS), F32), pltpu.VMEM((1, COLS), F32), pltpu.VMEM((HEAD_DIM, COLS), F32),
                pltpu.VMEM((1, COLS), F32), pltpu.VMEM((2 * TK, COLS), BF16)]
    scratch = [shape for shape in per_head for _ in range(N_KV)] + [pltpu.SMEM((1,), jnp.int32)]
    if mode in ("sel", "moba"):
        scratch += [pltpu.VMEM((2 * TK, COLS), F32)] * N_KV + [pltpu.VMEM((1, COLS), F32)] * N_KV
    return pl.pallas_call(
        functools.partial(_flash_kernel, mode),
        out_shape=jax.ShapeDtypeStruct((b, s, MIX_WIDTH), out_dtype),
        grid=(b, s // TQ),
        in_specs=in_specs,
        out_specs=pl.BlockSpec((1, TQ, MIX_WIDTH), lambda bi, i: (bi, i, 0)),
        scratch_shapes=scratch,
        compiler_params=_cparams(("parallel", "arbitrary")),
        name="flash_" + mode,
    )(*args)


def _nsa_combine_kernel(cmp_ref, sel_ref, win_ref, gate_ref, o_ref):
    gate = jax.nn.sigmoid(gate_ref[...])
    for h in range(N_HEADS):
        cols = slice(h * HEAD_DIM, (h + 1) * HEAD_DIM)
        out = (gate[:, 3 * h:3 * h + 1] * cmp_ref[:, cols]
               + gate[:, 3 * h + 1:3 * h + 2] * sel_ref[:, cols]
               + gate[:, 3 * h + 2:3 * h + 3] * win_ref[:, cols])
        o_ref[:, cols] = out.astype(BF16)


def nsa_combine(o_cmp, o_sel, o_win, proj, *, tm=512):
    t = o_cmp.shape[0]
    o_spec = pl.BlockSpec((tm, MIX_WIDTH), lambda i: (i, 0))
    return pl.pallas_call(
        _nsa_combine_kernel,
        out_shape=jax.ShapeDtypeStruct((t, MIX_WIDTH), BF16),
        grid=(t // tm,),
        in_specs=[o_spec, o_spec, o_spec,
                  pl.BlockSpec((tm, LANES), lambda i: (i, COL_A_GATE // LANES))],
        out_specs=o_spec,
        compiler_params=_cparams(("parallel",)),
        name="nsa_combine",
    )(o_cmp, o_sel, o_win, proj)


def _selection_share(n_sel, n_half):
    blk = np.arange(n_sel)[:, None] * SEL_BLOCK
    starts = np.arange(n_half)[None, :] * CMP_STRIDE
    shared = np.clip(np.minimum(starts + CMP_LEN, blk + SEL_BLOCK) - np.maximum(starts, blk), 0, None)
    shared = shared / CMP_STRIDE
    shared[:, n_half - 1] = 0.0
    return shared.astype(np.float32)


def _rearranged_w_in(w):
    d = w.shape[0]
    q_w, kv_w = MIX_WIDTH, KV_W
    a_q = 0
    a_kv = a_q + q_w
    a_gate = a_kv + 6 * kv_w
    b_q = a_gate + ORIG_GATE_W
    b_kv = b_q + q_w
    c_q = b_kv + 2 * kv_w
    c_kv = c_q + q_w
    merge = c_kv + 2 * kv_w
    gate_pad = COL_MERGE - COL_A_GATE - ORIG_GATE_W
    return jnp.concatenate([
        w[:, a_q:a_q + q_w], w[:, b_q:b_q + q_w], w[:, c_q:c_q + q_w],
        w[:, a_kv:a_gate], w[:, b_kv:c_q], w[:, c_kv:merge],
        w[:, a_gate:b_q], jnp.zeros((d, gate_pad), w.dtype),
        w[:, merge:],
    ], axis=1).astype(BF16)


def kernel(x, w_in, cmp_pos, cmp_w1, cmp_w2, swa_sinks, w_branch, w_out, w_mlp_in, w_mlp_out,
           norm_mix, norm_mlp, norm_final, rel_bias):
    b, s, d = x.shape
    depth = w_in.shape[0]
    t = b * s
    n_half = s // CMP_STRIDE
    n_sel = s // SEL_BLOCK

    tbl_flat = rel_bias.reshape(-1)
    c31 = rel_bias[N_BUCKETS - 1]
    band = band_bias(tbl_flat)
    bias_c = cmp_bias(tbl_flat, s)
    share = jnp.asarray(_selection_share(n_sel, n_half), BF16)

    xt = x.reshape(t, d)
    for layer in range(depth):
        proj2d = norm_matmul(xt, norm_mix[layer], _rearranged_w_in(w_in[layer]))
        proj = proj2d.reshape(b, s, N_PROJ)

        halves = proj[:, :, COL_A_KC:COL_A_KC + 2 * KV_W]
        halves = halves.reshape(b, n_half, CMP_STRIDE, 2, N_KV, HEAD_DIM)
        halves = halves.transpose(0, 3, 4, 1, 2, 5).reshape(b, 2, N_KV, n_half, CMP_STRIDE * HEAD_DIM)
        cmp_tokens = nsa_compress(halves, cmp_pos[layer].reshape(2, 2, CMP_STRIDE * HEAD_DIM),
                                  cmp_w1, cmp_w2, layer)
        o_cmp, sel_mask = nsa_compressed(bias_c, proj, cmp_tokens, share)
        o_sel = flash_attention("sel", proj, band, c31, q_col=COL_A_Q, k_col=COL_A_KS, v_col=COL_A_VS,
                                out_dtype=F32, mask=sel_mask)
        o_win = flash_attention("win", proj, band, c31, q_col=COL_A_Q, k_col=COL_A_KW, v_col=COL_A_VW,
                                out_dtype=F32)
        o_a = nsa_combine(o_cmp.reshape(t, MIX_WIDTH), o_sel.reshape(t, MIX_WIDTH),
                          o_win.reshape(t, MIX_WIDTH), proj2d)

        o_b = flash_attention("swa", proj, band, c31, q_col=COL_B_Q, k_col=COL_B_K, v_col=COL_B_V,
                              out_dtype=BF16, sinks=swa_sinks[layer])
        moba_mask = moba_gate(proj)
        o_c = flash_attention("moba", proj, band, c31, q_col=COL_C_Q, k_col=COL_C_K, v_col=COL_C_V,
                              out_dtype=BF16, mask=moba_mask)

        z = merge_branches(o_a, o_b.reshape(t, MIX_WIDTH), o_c.reshape(t, MIX_WIDTH), w_branch, layer, proj2d)
        xt = matmul_residual(z, w_out, layer, xt)
        xt = mlp_block(xt, norm_mlp[layer], w_mlp_in, w_mlp_out, layer)

    return final_norm(xt, norm_final).reshape(b, s, d)
```

```python
import functools
import math

import numpy as np
import jax
import jax.numpy as jnp
from jax import lax
from jax.experimental import pallas as pl
from jax.experimental.pallas import tpu as pltpu

F32 = jnp.float32
BF16 = jnp.bfloat16

D_MODEL = 2048
HEAD_DIM = 128
N_HEADS = 8
N_KV = 2
GROUP = N_HEADS // N_KV
MIX_WIDTH = N_HEADS * HEAD_DIM
CMP_LEN = 32
CMP_STRIDE = 16
CMP_HIDDEN = 2 * HEAD_DIM
SEL_BLOCK = 64
SEL_TOP_N = 16
NSA_WINDOW = 512
SWA_WINDOW = 128
MOBA_BLOCK = 256
MOBA_TOP_K = 3
N_BUCKETS = 32
BUCKET_EXACT = N_BUCKETS // 2
BUCKET_MAX_DIST = 128
TOTAL_HEADS = 3 * N_HEADS
D_FF = 4 * D_MODEL
RMS_EPS = 1e-6
NEG_INF = -1e30
FORCE_SCORE = 1e30
TINY = 1e-30
BELOW_ALL = -3e38
SCALE = HEAD_DIM ** -0.5

LANES = 128
BF16_ROWS = 16
VMEM_LIMIT = 56 * 1024 * 1024
TQ = 128
TK = 128
COLS = GROUP * TQ
KV_W = N_KV * HEAD_DIM
LOG2E = math.log2(math.e)
SCALE2 = SCALE * LOG2E

COL_A_Q = 0
COL_B_Q = 1024
COL_C_Q = 2048
COL_A_KC = 3072
COL_A_KS = 3584
COL_A_VS = 3840
COL_A_KW = 4096
COL_A_VW = 4352
COL_B_K = 4608
COL_B_V = 4864
COL_C_K = 5120
COL_C_V = 5376
COL_A_GATE = 5632
COL_MERGE = 6144
N_PROJ = COL_MERGE + 3 * D_MODEL
ORIG_GATE_W = 3 * N_HEADS

MIXER = {"sel": 0, "win": 0, "swa": 1, "moba": 2}


def _cparams(semantics):
    return pltpu.CompilerParams(dimension_semantics=semantics, vmem_limit_bytes=VMEM_LIMIT)


def _rms(x, gain):
    y = x * lax.rsqrt(jnp.mean(x * x, axis=-1, keepdims=True) + RMS_EPS)
    return y * gain


def _norm_matmul_kernel(x_ref, g_ref, w_ref, o_ref, h_scr):
    @pl.when(pl.program_id(1) == 0)
    def _():
        h_scr[...] = _rms(x_ref[...], g_ref[...]).astype(BF16)

    o_ref[...] = jnp.dot(h_scr[...], w_ref[0], preferred_element_type=F32)


def norm_matmul(x, gain, w, layer, *, tm=1024, tn=1024):
    t, k = x.shape
    n = w.shape[2]
    return pl.pallas_call(
        _norm_matmul_kernel,
        out_shape=jax.ShapeDtypeStruct((t, n), F32),
        grid=(t // tm, n // tn),
        in_specs=[pl.BlockSpec((tm, k), lambda i, j: (i, 0)),
                  pl.BlockSpec((1, k), lambda i, j: (0, 0)),
                  pl.BlockSpec((1, k, tn), lambda i, j: (layer, 0, j))],
        out_specs=pl.BlockSpec((tm, tn), lambda i, j: (i, j)),
        scratch_shapes=[pltpu.VMEM((tm, k), BF16)],
        compiler_params=_cparams(("parallel", "arbitrary")),
        name="norm_matmul",
    )(x, gain.reshape(1, k), w)


def _merge_kernel(oa_ref, ob_ref, oc_ref, wb_ref, g0_ref, g1_ref, g2_ref, z_ref, w_scr):
    @pl.when(pl.program_id(1) == 0)
    def _():
        w_scr[...] = wb_ref[0].astype(BF16)

    acc = jax.nn.sigmoid(g0_ref[...]) * jnp.dot(oa_ref[...], w_scr[0], preferred_element_type=F32)
    acc += jax.nn.sigmoid(g1_ref[...]) * jnp.dot(ob_ref[...], w_scr[1], preferred_element_type=F32)
    acc += jax.nn.sigmoid(g2_ref[...]) * jnp.dot(oc_ref[...], w_scr[2], preferred_element_type=F32)
    z_ref[...] = acc.astype(BF16)


def merge_branches(o_a, o_b, o_c, w_branch, layer, proj, *, tm=512, tn=512):
    t = o_a.shape[0]
    gate_blk = COL_MERGE // tn
    per_branch = D_MODEL // tn
    o_spec = pl.BlockSpec((tm, MIX_WIDTH), lambda j, i: (i, 0))

    def gate_spec(m):
        return pl.BlockSpec((tm, tn), lambda j, i: (i, gate_blk + m * per_branch + j))

    return pl.pallas_call(
        _merge_kernel,
        out_shape=jax.ShapeDtypeStruct((t, D_MODEL), BF16),
        grid=(D_MODEL // tn, t // tm),
        in_specs=[o_spec, o_spec, o_spec,
                  pl.BlockSpec((1, 3, MIX_WIDTH, tn), lambda j, i: (layer, 0, 0, j)),
                  gate_spec(0), gate_spec(1), gate_spec(2)],
        out_specs=pl.BlockSpec((tm, tn), lambda j, i: (i, j)),
        scratch_shapes=[pltpu.VMEM((3, MIX_WIDTH, tn), BF16)],
        compiler_params=_cparams(("parallel", "arbitrary")),
        name="merge_branches",
    )(o_a, o_b, o_c, w_branch, proj, proj, proj)


def _matmul_res_kernel(a_ref, w_ref, x_ref, o_ref, w_scr):
    @pl.when(pl.program_id(1) == 0)
    def _():
        w_scr[...] = w_ref[0].astype(BF16)

    o_ref[...] = x_ref[...] + jnp.dot(a_ref[...], w_scr[...], preferred_element_type=F32)


def matmul_residual(a, w, layer, x, *, tm=1024, tn=1024):
    t, k = a.shape
    n = w.shape[2]
    return pl.pallas_call(
        _matmul_res_kernel,
        out_shape=jax.ShapeDtypeStruct((t, n), F32),
        grid=(n // tn, t // tm),
        in_specs=[pl.BlockSpec((tm, k), lambda j, i: (i, 0)),
                  pl.BlockSpec((1, k, tn), lambda j, i: (layer, 0, j)),
                  pl.BlockSpec((tm, tn), lambda j, i: (i, j))],
        out_specs=pl.BlockSpec((tm, tn), lambda j, i: (i, j)),
        scratch_shapes=[pltpu.VMEM((k, tn), BF16)],
        compiler_params=_cparams(("parallel", "arbitrary")),
        name="matmul_residual",
    )(a, w, x)


def _mlp_kernel(x_ref, g_ref, w1_ref, w2_ref, o_ref, h_scr):
    @pl.when(pl.program_id(1) == 0)
    def _():
        x = x_ref[...]
        h_scr[...] = _rms(x, g_ref[...]).astype(BF16)
        o_ref[...] = x

    u = jnp.dot(h_scr[...], w1_ref[0].astype(BF16), preferred_element_type=F32)
    u = jnp.square(jnp.maximum(u, 0.0)).astype(BF16)
    o_ref[...] += jnp.dot(u, w2_ref[0].astype(BF16), preferred_element_type=F32)


def mlp_block(x, gain, w1, w2, layer, *, tm=1024, tf=512):
    t, d = x.shape
    dff = w1.shape[2]
    return pl.pallas_call(
        _mlp_kernel,
        out_shape=jax.ShapeDtypeStruct((t, d), F32),
        grid=(t // tm, dff // tf),
        in_specs=[pl.BlockSpec((tm, d), lambda i, f: (i, 0), pipeline_mode=pl.Buffered(1)),
                  pl.BlockSpec((1, d), lambda i, f: (0, 0)),
                  pl.BlockSpec((1, d, tf), lambda i, f: (layer, 0, f)),
                  pl.BlockSpec((1, tf, d), lambda i, f: (layer, f, 0))],
        out_specs=pl.BlockSpec((tm, d), lambda i, f: (i, 0)),
        scratch_shapes=[pltpu.VMEM((tm, d), BF16)],
        compiler_params=_cparams(("parallel", "arbitrary")),
        name="mlp_block",
    )(x, gain.reshape(1, d), w1, w2)


def _final_norm_kernel(x_ref, g_ref, o_ref):
    o_ref[...] = _rms(x_ref[...], g_ref[...])


def final_norm(x, gain, *, tm=512):
    t, d = x.shape
    return pl.pallas_call(
        _final_norm_kernel,
        out_shape=jax.ShapeDtypeStruct((t, d), F32),
        grid=(t // tm,),
        in_specs=[pl.BlockSpec((tm, d), lambda i: (i, 0)),
                  pl.BlockSpec((1, d), lambda i: (0, 0))],
        out_specs=pl.BlockSpec((tm, d), lambda i: (i, 0)),
        compiler_params=_cparams(("parallel",)),
        name="final_norm",
    )(x, gain.reshape(1, d))


def _t5_bucket(n):
    log_ratio = jnp.log(jnp.maximum(n, 1).astype(F32) / BUCKET_EXACT) / math.log(BUCKET_MAX_DIST / BUCKET_EXACT)
    large = jnp.minimum(BUCKET_EXACT + (log_ratio * (N_BUCKETS - BUCKET_EXACT)).astype(jnp.int32), N_BUCKETS - 1)
    return jnp.where(n < BUCKET_EXACT, n, large)


def _lookup_bias(tbl_ref, head, n):
    bucket = _t5_bucket(n)
    out = jnp.zeros(n.shape, F32)
    for b in range(N_BUCKETS):
        out = jnp.where(bucket == b, tbl_ref[b * TOTAL_HEADS + head], out)
    return out


def _band_bias_kernel(tbl_ref, o_ref):
    l = lax.broadcasted_iota(jnp.int32, (TK, TQ), 0)
    q = lax.broadcasted_iota(jnp.int32, (TK, TQ), 1)
    head = pl.program_id(0)
    prev = _lookup_bias(tbl_ref, head, jnp.maximum(q + TK - l, 0)) * LOG2E
    diag = _lookup_bias(tbl_ref, head, jnp.maximum(q - l, 0)) * LOG2E
    sliding = (head >= MIXER["swa"] * N_HEADS) & (head < (MIXER["swa"] + 1) * N_HEADS)
    o_ref[0] = jnp.where(sliding & (q + TK - l >= SWA_WINDOW), NEG_INF, prev)
    o_ref[1] = jnp.where(l > q, NEG_INF, diag)


def band_bias(tbl_flat):
    return pl.pallas_call(
        _band_bias_kernel,
        out_shape=jax.ShapeDtypeStruct((2, TK, TOTAL_HEADS * TQ), F32),
        grid=(TOTAL_HEADS,),
        in_specs=[pl.BlockSpec(memory_space=pltpu.SMEM)],
        out_specs=pl.BlockSpec((2, TK, TQ), lambda h: (0, 0, h)),
        compiler_params=_cparams(("parallel",)),
        name="band_bias",
    )(tbl_flat)


def _cmp_bias_kernel(tbl_ref, o_ref):
    c = lax.broadcasted_iota(jnp.int32, (LANES, TQ), 0)
    t = pl.program_id(0) * TQ + lax.broadcasted_iota(jnp.int32, (LANES, TQ), 1)
    o_ref[0] = _lookup_bias(tbl_ref, pl.program_id(1), jnp.maximum(t - (c * CMP_STRIDE + CMP_LEN - 1), 0))


def cmp_bias(tbl_flat, s):
    return pl.pallas_call(
        _cmp_bias_kernel,
        out_shape=jax.ShapeDtypeStruct((s // TQ, LANES, N_HEADS * TQ), F32),
        grid=(s // TQ, N_HEADS),
        in_specs=[pl.BlockSpec(memory_space=pltpu.SMEM)],
        out_specs=pl.BlockSpec((1, LANES, TQ), lambda i, h: (i, 0, h)),
        compiler_params=_cparams(("parallel", "parallel")),
        name="cmp_bias",
    )(tbl_flat)


def _compress_kernel(x_ref, pos_ref, w1_ref, w2_ref, o_ref):
    n_half = x_ref.shape[1] // CMP_STRIDE
    half = jnp.concatenate([x_ref[0, pl.ds(l, n_half, stride=CMP_STRIDE), :] for l in range(CMP_STRIDE)], axis=1)
    half_w = CMP_STRIDE * HEAD_DIM
    xa = (half + pos_ref[0, 0:1, :]).astype(BF16)
    xb = (half + pos_ref[0, 1:2, :]).astype(BF16)
    ha = jnp.dot(xa, w1_ref[0, 0, 0:half_w, :].astype(BF16), preferred_element_type=F32)
    hb = jnp.dot(xb, w1_ref[0, 0, half_w:2 * half_w, :].astype(BF16), preferred_element_type=F32)
    hidden = ha + pltpu.roll(hb, n_half - 1, 0)
    act = jax.nn.gelu(hidden, approximate=True).astype(BF16)
    o_ref[0, 0, 0] = jnp.dot(act, w2_ref[0, 0].astype(BF16), preferred_element_type=F32)


def nsa_compress(proj, pos, w1, w2, layer):
    b, s, _ = proj.shape
    n_kv, n_half, width = N_KV, s // CMP_STRIDE, CMP_STRIDE * HEAD_DIM
    col_blk = COL_A_KC // HEAD_DIM
    return pl.pallas_call(
        _compress_kernel,
        out_shape=jax.ShapeDtypeStruct((b, 2, n_kv, n_half, HEAD_DIM), F32),
        grid=(2, b, n_kv),
        in_specs=[pl.BlockSpec((1, s, HEAD_DIM), lambda kv, bi, h: (bi, 0, col_blk + kv * n_kv + h)),
                  pl.BlockSpec((1, 2, width), lambda kv, bi, h: (kv, 0, 0)),
                  pl.BlockSpec((1, 1, 2 * width, CMP_HIDDEN), lambda kv, bi, h: (layer, kv, 0, 0)),
                  pl.BlockSpec((1, 1, CMP_HIDDEN, HEAD_DIM), lambda kv, bi, h: (layer, kv, 0, 0))],
        out_specs=pl.BlockSpec((1, 1, 1, n_half, HEAD_DIM), lambda kv, bi, h: (bi, kv, h, 0, 0)),
        compiler_params=_cparams(("parallel", "parallel", "parallel")),
        name="nsa_compress",
    )(proj, pos, w1, w2)


def _stack_heads(q):
    return jnp.concatenate([q[:, g * HEAD_DIM:(g + 1) * HEAD_DIM] for g in range(GROUP)], axis=0)


def _untranspose_heads(o_t):
    return jnp.concatenate([o_t[:, g * TQ:(g + 1) * TQ].T for g in range(GROUP)], axis=1)


def _head_row(ref, first):
    return jnp.concatenate([jnp.full((1, TQ), ref[first + g] * LOG2E, F32) for g in range(GROUP)], axis=1)


def _split3(x):
    x1 = x.astype(BF16)
    r1 = x - x1.astype(F32)
    x2 = r1.astype(BF16)
    x3 = (r1 - x2.astype(F32)).astype(BF16)
    return x1, x2, x3


def _dot_nt(a, b):
    return lax.dot_general(a, b, (((1,), (1,)), ((), ())), preferred_element_type=F32)


def _rank_select(score, n_cand, n_top):
    idx = lax.broadcasted_iota(jnp.int32, score.shape, 0)
    rank = jnp.zeros(score.shape, jnp.int32)
    for j in range(n_cand):
        other = score[j:j + 1, :]
        ahead = (other > score) | ((other == score) & (j < idx))
        rank += jnp.where(ahead, 1, 0)
    return jnp.where((rank < n_top) & (idx < n_cand), 1.0, 0.0)


def _nsa_cmp_kernel(bias_ref, q_ref, kc_ref, vc_ref, share_ref, o_ref, sel_ref):
    qi = pl.program_id(1)
    share = share_ref[...]
    n_sel = share.shape[0]
    c = lax.broadcasted_iota(jnp.int32, (LANES, COLS), 0)
    t = qi * TQ + (lax.broadcasted_iota(jnp.int32, (LANES, COLS), 1) & (TQ - 1))
    valid = t - (c * CMP_STRIDE + CMP_LEN - 1) >= 0
    blk = lax.broadcasted_iota(jnp.int32, (n_sel, TQ), 0)
    cur = (qi * TQ + lax.broadcasted_iota(jnp.int32, (n_sel, TQ), 1)) // SEL_BLOCK
    forced = (blk == 0) | (blk == cur) | (blk == cur - 1)
    for h in range(N_KV):
        cols = slice(h * COLS, (h + 1) * COLS)
        q4 = _stack_heads(q_ref[0, :, cols]).astype(BF16)
        kc = kc_ref[0, 0, h].astype(BF16)
        vc_t = vc_ref[0, 0, h].T.astype(BF16)
        logits = _dot_nt(kc, q4) * SCALE + bias_ref[0, :, cols]
        logits = jnp.where(valid, logits, NEG_INF)
        m = jnp.max(logits, axis=0, keepdims=True)
        p = jnp.where(valid, jnp.exp(logits - m), 0.0)
        p = p / jnp.maximum(jnp.sum(p, axis=0, keepdims=True), TINY)
        o_ref[0, :, cols] = _untranspose_heads(jnp.dot(vc_t, p.astype(BF16), preferred_element_type=F32))

        p_sum = p[:, 0:TQ]
        for g in range(1, GROUP):
            p_sum = p_sum + p[:, g * TQ:(g + 1) * TQ]
        importance = sum(jnp.dot(share, part, preferred_element_type=F32) for part in _split3(p_sum))
        score = jnp.where(forced, FORCE_SCORE, jnp.where(blk <= cur, importance, NEG_INF))
        sel_ref[0, h] = _rank_select(score, n_sel, min(SEL_TOP_N, n_sel))


def nsa_compressed(bias_c, proj, cmp_tokens, share):
    b, s, _ = proj.shape
    n_cmp = cmp_tokens.shape[3]
    n_sel = share.shape[0]
    return pl.pallas_call(
        _nsa_cmp_kernel,
        out_shape=(jax.ShapeDtypeStruct((b, s, MIX_WIDTH), F32),
                   jax.ShapeDtypeStruct((b, N_KV, n_sel, s), F32)),
        grid=(b, s // TQ),
        in_specs=[pl.BlockSpec((1, n_cmp, N_KV * COLS), lambda bi, i: (i, 0, 0)),
                  pl.BlockSpec((1, TQ, MIX_WIDTH), lambda bi, i: (bi, i, COL_A_Q // MIX_WIDTH)),
                  pl.BlockSpec((1, 1, N_KV, n_cmp, HEAD_DIM), lambda bi, i: (bi, 0, 0, 0, 0)),
                  pl.BlockSpec((1, 1, N_KV, n_cmp, HEAD_DIM), lambda bi, i: (bi, 1, 0, 0, 0)),
                  pl.BlockSpec((n_sel, n_cmp), lambda bi, i: (0, 0))],
        out_specs=(pl.BlockSpec((1, TQ, MIX_WIDTH), lambda bi, i: (bi, i, 0)),
                   pl.BlockSpec((1, N_KV, n_sel, TQ), lambda bi, i: (bi, 0, 0, i))),
        compiler_params=_cparams(("parallel", "parallel")),
        name="nsa_compressed",
    )(bias_c, proj, cmp_tokens, cmp_tokens, share)


def _moba_gate_kernel(q_ref, k_ref, sel_ref, km_scr):
    qi = pl.program_id(1)
    n_blk = k_ref.shape[1] // MOBA_BLOCK

    @pl.when(qi == 0)
    def _():
        km_scr[...] = jnp.zeros_like(km_scr)
        for h in range(N_KV):
            for n in range(n_blk):
                blk = k_ref[0, n * MOBA_BLOCK:(n + 1) * MOBA_BLOCK, h * HEAD_DIM:(h + 1) * HEAD_DIM]
                km_scr[h, n:n + 1, :] = jnp.sum(blk, axis=0, keepdims=True) / MOBA_BLOCK

    n = lax.broadcasted_iota(jnp.int32, (BF16_ROWS, COLS), 0)
    q_blk = (qi * TQ) // MOBA_BLOCK
    for h in range(N_KV):
        q4 = _stack_heads(q_ref[0, :, h * COLS:(h + 1) * COLS])
        q_hi, q_lo, _ = _split3(q4)
        k_hi, k_lo, _ = _split3(km_scr[h])
        gate = _dot_nt(k_hi, q_hi) + _dot_nt(k_lo, q_hi) + _dot_nt(k_hi, q_lo)
        gate = jnp.where(n < q_blk, gate, NEG_INF)
        gate = jnp.where(n < n_blk, gate, BELOW_ALL)
        picked = _rank_select(gate, n_blk, min(MOBA_TOP_K, n_blk - 1))
        sel_ref[0, h, 0] = jnp.where(n < q_blk, picked, 0.0)


def moba_gate(proj):
    b, s, _ = proj.shape
    return pl.pallas_call(
        _moba_gate_kernel,
        out_shape=jax.ShapeDtypeStruct((b, N_KV, s // TQ, BF16_ROWS, COLS), F32),
        grid=(b, s // TQ),
        in_specs=[pl.BlockSpec((1, TQ, MIX_WIDTH), lambda bi, i: (bi, i, COL_C_Q // MIX_WIDTH)),
                  pl.BlockSpec((1, s, KV_W), lambda bi, i: (bi, 0, COL_C_K // KV_W))],
        out_specs=pl.BlockSpec((1, N_KV, 1, BF16_ROWS, COLS), lambda bi, i: (bi, 0, i, 0, 0)),
        scratch_shapes=[pltpu.VMEM((N_KV, BF16_ROWS, HEAD_DIM), F32)],
        compiler_params=_cparams(("parallel", "arbitrary")),
        name="moba_gate",
    )(proj, proj)


def _flash_kernel(mode, *refs):
    n_in = 5 if mode == "win" else 6
    ins, o_ref, scr = refs[:n_in], refs[n_in], refs[n_in + 1:]
    if mode == "swa":
        c31_ref, sink_ref, band_ref, q_ref, k_ref, v_ref = ins
    elif mode == "win":
        c31_ref, band_ref, q_ref, k_ref, v_ref = ins
    else:
        c31_ref, band_ref, q_ref, k_ref, v_ref, msk_ref = ins
    kb_scr, vt_scr, m_scr, l_scr, acc_scr, alpha_scr, pb_scr = (scr[i * N_KV:(i + 1) * N_KV] for i in range(7))
    pend_ref = scr[7 * N_KV]
    s_scr = scr[7 * N_KV + 1:8 * N_KV + 1]
    mb_scr = scr[8 * N_KV + 1:]
    qi = pl.program_id(1)
    n_kb = k_ref.shape[1] // TK
    head0 = MIXER[mode] * N_HEADS

    @pl.when(qi == 0)
    def _():
        for h in range(N_KV):
            cols = slice(h * HEAD_DIM, (h + 1) * HEAD_DIM)
            kb_scr[h][...] = k_ref[0, :, cols].astype(BF16)
            for j in range(n_kb):
                vt_scr[h][j] = v_ref[0, j * TK:(j + 1) * TK, cols].T.astype(BF16)

    q4 = [(_stack_heads(q_ref[0, :, h * COLS:(h + 1) * COLS]) * SCALE2).astype(BF16) for h in range(N_KV)]
    far_bias = [_head_row(c31_ref, head0 + h * GROUP) for h in range(N_KV)]

    def penalty(keep):
        return (keep - 1.0) * (-NEG_INF)

    def expand(rows, seg):
        return jnp.concatenate([jnp.broadcast_to(r, (seg, COLS)) for r in rows], axis=0)

    def sel_rows(h, first_blk, n_blocks):
        per = TK // SEL_BLOCK
        rows = []
        for part in range(n_blocks * per):
            r = msk_ref[0, h, pl.ds(first_blk * per + part, 1), :]
            rows.append(jnp.concatenate([penalty(r)] * GROUP, axis=1))
        return rows

    def moba_row(h, key_blk):
        return msk_ref[0, h, 0, pl.ds((key_blk * TK) // MOBA_BLOCK, 1), :]

    def far_bias_tile(h, k0, n_blocks):
        if mode == "sel":
            return expand([far_bias[h] + r for r in sel_rows(h, k0, n_blocks)], SEL_BLOCK)
        if mode == "moba":
            return expand([far_bias[h] + penalty(moba_row(h, k0))], n_blocks * TK)
        return far_bias[h]

    def scores(h, k0, n_blocks, bias):
        start = pl.multiple_of(k0 * TK, TK)
        return _dot_nt(kb_scr[h][pl.ds(start, n_blocks * TK), :], q4[h]) + bias

    def flush():
        k0 = pend_ref[0]
        for h in range(N_KV):
            v_t = jnp.concatenate([vt_scr[h][k0], vt_scr[h][k0 + 1]], axis=1)
            pv = jnp.dot(v_t, pb_scr[h][...], preferred_element_type=F32)
            acc_scr[h][...] = alpha_scr[h][...] * acc_scr[h][...] + pv

    def softmax(h, s, m_blk, first):
        n_keys = s.shape[0]
        if first:
            m_new = m_blk
        else:
            m_prev = m_scr[h][...]
            m_new = jnp.maximum(m_prev, m_blk)
            alpha = jnp.exp2(m_prev - m_new)
        p = jnp.exp2(s - m_new)
        p_sum = jnp.sum(p, axis=0, keepdims=True)
        pb_scr[h][0:n_keys, :] = p.astype(BF16)
        if n_keys == TK:
            pb_scr[h][TK:2 * TK, :] = jnp.zeros((TK, COLS), BF16)
        if first:
            l_scr[h][...] = p_sum
            alpha_scr[h][...] = jnp.zeros((1, COLS), F32)
            acc_scr[h][...] = jnp.zeros((HEAD_DIM, COLS), F32)
        else:
            l_scr[h][...] = alpha * l_scr[h][...] + p_sum
            alpha_scr[h][...] = alpha
        m_scr[h][...] = m_new

    def step(k0, n_blocks, bias, first=False):
        if not first:
            flush()
        for h in range(N_KV):
            s = scores(h, k0, n_blocks, bias[h])
            softmax(h, s, jnp.max(s, axis=0, keepdims=True), first)
        pend_ref[0] = k0

    def produce(pair):
        for h in range(N_KV):
            s = scores(h, 2 * pair, 2, far_bias_tile(h, 2 * pair, 2))
            s_scr[h][...] = s
            mb_scr[h][...] = jnp.max(s, axis=0, keepdims=True)

    @pl.when(qi == 0)
    def _():
        bias = [band_ref[1, :, h * COLS:(h + 1) * COLS] for h in range(N_KV)]
        if mode == "sel":
            bias = [bias[h] + expand(sel_rows(h, 0, 1), SEL_BLOCK) for h in range(N_KV)]
        step(0, 1, bias, first=True)

    @pl.when(qi >= 1)
    def _():
        bias = [band_ref[:, :, h * COLS:(h + 1) * COLS].reshape(2 * TK, COLS) for h in range(N_KV)]
        if mode == "sel":
            bias = [bias[h] + expand(sel_rows(h, qi - 1, 2), SEL_BLOCK) for h in range(N_KV)]
        elif mode == "moba":
            own = (qi * TQ) // MOBA_BLOCK == ((qi - 1) * TK) // MOBA_BLOCK
            own_f = jnp.where(own, 1.0, 0.0)
            bias = [bias[h] + expand([penalty(jnp.minimum(moba_row(h, qi - 1) + own_f, 1.0)),
                                      jnp.zeros((1, COLS), F32)], TK) for h in range(N_KV)]
        step(qi - 1, 2, bias, first=True)
        if mode in ("sel", "moba"):
            produce(0)

    masked_rows = jnp.full((TK, COLS), NEG_INF, F32)
    if mode in ("sel", "moba"):
        n_far = jnp.maximum(qi - 1, 0)
        n_pairs = n_far >> 1

        def pair_body(pair, carry):
            flush()
            for h in range(N_KV):
                softmax(h, s_scr[h][...], mb_scr[h][...], False)
            pend_ref[0] = 2 * pair
            produce(jnp.minimum(pair + 1, n_pairs - 1))
            return carry
        lax.fori_loop(0, n_pairs, pair_body, 0)

        @pl.when((n_far & 1) == 1)
        def _():
            step(n_far - 1, 2, [jnp.concatenate([far_bias_tile(h, n_far - 1, 1), masked_rows], axis=0)
                                for h in range(N_KV)])
    elif mode == "win":
        n_full = NSA_WINDOW // TK

        @pl.when(qi >= 3)
        def _():
            step(qi - 3, 2, far_bias)

        @pl.when(qi == 2)
        def _():
            step(0, 2, [jnp.concatenate([jnp.broadcast_to(far_bias[h], (TK, COLS)), masked_rows], axis=0)
                        for h in range(N_KV)])

        @pl.when(qi >= n_full)
        def _():
            key = lax.broadcasted_iota(jnp.int32, (TK, COLS), 0)
            qry = lax.broadcasted_iota(jnp.int32, (TK, COLS), 1) & (TQ - 1)
            step(qi - n_full, 2, [jnp.concatenate([jnp.where(key > qry, far_bias[h], NEG_INF), masked_rows], axis=0)
                                  for h in range(N_KV)])

    flush()
    for h in range(N_KV):
        m = m_scr[h][...]
        l = l_scr[h][...]
        acc = acc_scr[h][...]
        if mode == "swa":
            sink = _head_row(sink_ref, h * GROUP)
            m_fin = jnp.maximum(m, sink)
            shrink = jnp.exp2(m - m_fin)
            l = l * shrink + jnp.exp2(sink - m_fin)
            acc = acc * shrink
        out = acc / jnp.maximum(l, TINY)
        o_ref[0, :, h * COLS:(h + 1) * COLS] = _untranspose_heads(out).astype(o_ref.dtype)


def flash_attention(mode, proj, band, c31, *, q_col, k_col, v_col, out_dtype, mask=None, sinks=None):
    b, s, _ = proj.shape
    smem = pl.BlockSpec(memory_space=pltpu.SMEM)
    in_specs = [smem]
    args = [c31]
    if mode == "swa":
        in_specs.append(smem)
        args.append(sinks)
    in_specs += [pl.BlockSpec((2, TK, N_KV * COLS), lambda bi, i: (0, 0, MIXER[mode])),
                 pl.BlockSpec((1, TQ, MIX_WIDTH), lambda bi, i: (bi, i, q_col // MIX_WIDTH)),
                 pl.BlockSpec((1, s, KV_W), lambda bi, i: (bi, 0, k_col // KV_W)),
                 pl.BlockSpec((1, s, KV_W), lambda bi, i: (bi, 0, v_col // KV_W))]
    args += [band, proj, proj, proj]
    if mode == "sel":
        in_specs.append(pl.BlockSpec((1, N_KV, mask.shape[2], TQ), lambda bi, i: (bi, 0, 0, i)))
        args.append(mask)
    elif mode == "moba":
        in_specs.append(pl.BlockSpec((1, N_KV, 1, mask.shape[3], COLS), lambda bi, i: (bi, 0, i, 0, 0)))
        args.append(mask)
    per_head = [pltpu.VMEM((s, HEAD_DIM), BF16), pltpu.VMEM((s // TK, HEAD_DIM, TK), BF16),
                pltpu.VMEM((1, COLS), F32), pltpu.VMEM((1, COLS), F32), pltpu.VMEM((HEAD_DIM, COLS), F32),
                pltpu.VMEM((1, COLS), F32), pltpu.VMEM((2 * TK, COLS), BF16)]
    scratch = [shape for shape in per_head for _ in range(N_KV)] + [pltpu.SMEM((1,), jnp.int32)]
    if mode in ("sel", "moba"):
        scratch += [pltpu.VMEM((2 * TK, COLS), F32)] * N_KV + [pltpu.VMEM((1, COLS), F32)] * N_KV
    return pl.pallas_call(
        functools.partial(_flash_kernel, mode),
        out_shape=jax.ShapeDtypeStruct((b, s, MIX_WIDTH), out_dtype),
        grid=(b, s // TQ),
        in_specs=in_specs,
        out_specs=pl.BlockSpec((1, TQ, MIX_WIDTH), lambda bi, i: (bi, i, 0)),
        scratch_shapes=scratch,
        compiler_params=_cparams(("parallel", "arbitrary")),
        name="flash_" + mode,
    )(*args)


def _nsa_combine_kernel(cmp_ref, sel_ref, win_ref, gate_ref, o_ref):
    gate = jax.nn.sigmoid(gate_ref[...])
    for h in range(N_HEADS):
        cols = slice(h * HEAD_DIM, (h + 1) * HEAD_DIM)
        out = (gate[:, 3 * h:3 * h + 1] * cmp_ref[:, cols]
               + gate[:, 3 * h + 1:3 * h + 2] * sel_ref[:, cols]
               + gate[:, 3 * h + 2:3 * h + 3] * win_ref[:, cols])
        o_ref[:, cols] = out.astype(BF16)


def nsa_combine(o_cmp, o_sel, o_win, proj, *, tm=512):
    t = o_cmp.shape[0]
    o_spec = pl.BlockSpec((tm, MIX_WIDTH), lambda i: (i, 0))
    return pl.pallas_call(
        _nsa_combine_kernel,
        out_shape=jax.ShapeDtypeStruct((t, MIX_WIDTH), BF16),
        grid=(t // tm,),
        in_specs=[o_spec, o_spec, o_spec,
                  pl.BlockSpec((tm, LANES), lambda i: (i, COL_A_GATE // LANES))],
        out_specs=o_spec,
        compiler_params=_cparams(("parallel",)),
        name="nsa_combine",
    )(o_cmp, o_sel, o_win, proj)


def _selection_share(n_sel, n_half):
    blk = np.arange(n_sel)[:, None] * SEL_BLOCK
    starts = np.arange(n_half)[None, :] * CMP_STRIDE
    shared = np.clip(np.minimum(starts + CMP_LEN, blk + SEL_BLOCK) - np.maximum(starts, blk), 0, None)
    shared = shared / CMP_STRIDE
    shared[:, n_half - 1] = 0.0
    return shared.astype(np.float32)


def _rearranged_w_in(w):
    depth, d = w.shape[0], w.shape[1]
    q_w, kv_w = MIX_WIDTH, KV_W
    a_q = 0
    a_kv = a_q + q_w
    a_gate = a_kv + 6 * kv_w
    b_q = a_gate + ORIG_GATE_W
    b_kv = b_q + q_w
    c_q = b_kv + 2 * kv_w
    c_kv = c_q + q_w
    merge = c_kv + 2 * kv_w
    gate_pad = COL_MERGE - COL_A_GATE - ORIG_GATE_W
    return jnp.concatenate([
        w[..., a_q:a_q + q_w], w[..., b_q:b_q + q_w], w[..., c_q:c_q + q_w],
        w[..., a_kv:a_gate], w[..., b_kv:c_q], w[..., c_kv:merge],
        w[..., a_gate:b_q], jnp.zeros((depth, d, gate_pad), w.dtype),
        w[..., merge:],
    ], axis=2).astype(BF16)


def kernel(x, w_in, cmp_pos, cmp_w1, cmp_w2, swa_sinks, w_branch, w_out, w_mlp_in, w_mlp_out,
           norm_mix, norm_mlp, norm_final, rel_bias):
    b, s, d = x.shape
    depth = w_in.shape[0]
    t = b * s
    n_half = s // CMP_STRIDE
    n_sel = s // SEL_BLOCK

    tbl_flat = rel_bias.reshape(-1)
    c31 = rel_bias[N_BUCKETS - 1]
    band = band_bias(tbl_flat)
    bias_c = cmp_bias(tbl_flat, s)
    share = jnp.asarray(_selection_share(n_sel, n_half), BF16)

    w_in_cols = _rearranged_w_in(w_in)
    xt = x.reshape(t, d)
    for layer in range(depth):
        proj2d = norm_matmul(xt, norm_mix[layer], w_in_cols, layer)
        proj = proj2d.reshape(b, s, N_PROJ)

        cmp_tokens = nsa_compress(proj, cmp_pos[layer].reshape(2, 2, CMP_STRIDE * HEAD_DIM), cmp_w1, cmp_w2, layer)
        o_cmp, sel_mask = nsa_compressed(bias_c, proj, cmp_tokens, share)
        o_sel = flash_attention("sel", proj, band, c31, q_col=COL_A_Q, k_col=COL_A_KS, v_col=COL_A_VS,
                                out_dtype=F32, mask=sel_mask)
        o_win = flash_attention("win", proj, band, c31, q_col=COL_A_Q, k_col=COL_A_KW, v_col=COL_A_VW,
                                out_dtype=F32)
        o_a = nsa_combine(o_cmp.reshape(t, MIX_WIDTH), o_sel.reshape(t, MIX_WIDTH),
                          o_win.reshape(t, MIX_WIDTH), proj2d)

        o_b = flash_attention("swa", proj, band, c31, q_col=COL_B_Q, k_col=COL_B_K, v_col=COL_B_V,
                              out_dtype=BF16, sinks=swa_sinks[layer])
        moba_mask = moba_gate(proj)
        o_c = flash_attention("moba", proj, band, c31, q_col=COL_C_Q, k_col=COL_C_K, v_col=COL_C_V,
                              out_dtype=BF16, mask=moba_mask)

        z = merge_branches(o_a, o_b.reshape(t, MIX_WIDTH), o_c.reshape(t, MIX_WIDTH), w_branch, layer, proj2d)
        xt = matmul_residual(z, w_out, layer, xt)
        xt = mlp_block(xt, norm_mlp[layer], w_mlp_in, w_mlp_out, layer)

    return final_norm(xt, norm_final).reshape(b, s, d)
```

```python
import functools
import math

import numpy as np
import jax
import jax.numpy as jnp
from jax import lax
from jax.experimental import pallas as pl
from jax.experimental.pallas import tpu as pltpu

F32 = jnp.float32
BF16 = jnp.bfloat16

D_MODEL = 2048
HEAD_DIM = 128
N_HEADS = 8
N_KV = 2
GROUP = N_HEADS // N_KV
MIX_WIDTH = N_HEADS * HEAD_DIM
CMP_LEN = 32
CMP_STRIDE = 16
CMP_HIDDEN = 2 * HEAD_DIM
SEL_BLOCK = 64
SEL_TOP_N = 16
NSA_WINDOW = 512
SWA_WINDOW = 128
MOBA_BLOCK = 256
MOBA_TOP_K = 3
N_BUCKETS = 32
BUCKET_EXACT = N_BUCKETS // 2
BUCKET_MAX_DIST = 128
TOTAL_HEADS = 3 * N_HEADS
D_FF = 4 * D_MODEL
RMS_EPS = 1e-6
NEG_INF = -1e30
FORCE_SCORE = 1e30
TINY = 1e-30
BELOW_ALL = -3e38
SCALE = HEAD_DIM ** -0.5

LANES = 128
BF16_ROWS = 16
VMEM_LIMIT = 56 * 1024 * 1024
TQ = 128
TK = 128
COLS = GROUP * TQ
KV_W = N_KV * HEAD_DIM
LOG2E = math.log2(math.e)
SCALE2 = SCALE * LOG2E

COL_A_Q = 0
COL_B_Q = 1024
COL_C_Q = 2048
COL_A_KC = 3072
COL_A_KS = 3584
COL_A_VS = 3840
COL_A_KW = 4096
COL_A_VW = 4352
COL_B_K = 4608
COL_B_V = 4864
COL_C_K = 5120
COL_C_V = 5376
COL_A_GATE = 5632
COL_MERGE = 6144
N_PROJ = COL_MERGE + 3 * D_MODEL
ORIG_GATE_W = 3 * N_HEADS

MIXER = {"sel": 0, "win": 0, "swa": 1, "moba": 2}


def _cparams(semantics):
    return pltpu.CompilerParams(dimension_semantics=semantics, vmem_limit_bytes=VMEM_LIMIT)


def _rms(x, gain):
    y = x * lax.rsqrt(jnp.mean(x * x, axis=-1, keepdims=True) + RMS_EPS)
    return y * gain


def _norm_matmul_kernel(x_ref, g_ref, w_ref, o_ref, h_scr):
    @pl.when(pl.program_id(1) == 0)
    def _():
        h_scr[...] = _rms(x_ref[...], g_ref[...]).astype(BF16)

    o_ref[...] = jnp.dot(h_scr[...], w_ref[0], preferred_element_type=F32)


def norm_matmul(x, gain, w, layer, *, tm=1024, tn=1024):
    t, k = x.shape
    n = w.shape[2]
    return pl.pallas_call(
        _norm_matmul_kernel,
        out_shape=jax.ShapeDtypeStruct((t, n), F32),
        grid=(t // tm, n // tn),
        in_specs=[pl.BlockSpec((tm, k), lambda i, j: (i, 0)),
                  pl.BlockSpec((1, k), lambda i, j: (0, 0)),
                  pl.BlockSpec((1, k, tn), lambda i, j: (layer, 0, j))],
        out_specs=pl.BlockSpec((tm, tn), lambda i, j: (i, j)),
        scratch_shapes=[pltpu.VMEM((tm, k), BF16)],
        compiler_params=_cparams(("parallel", "arbitrary")),
        name="norm_matmul",
    )(x, gain.reshape(1, k), w)


def _merge_kernel(oa_ref, ob_ref, oc_ref, wb_ref, g0_ref, g1_ref, g2_ref, z_ref, w_scr):
    @pl.when(pl.program_id(1) == 0)
    def _():
        w_scr[...] = wb_ref[0].astype(BF16)

    acc = jax.nn.sigmoid(g0_ref[...]) * jnp.dot(oa_ref[...], w_scr[0], preferred_element_type=F32)
    acc += jax.nn.sigmoid(g1_ref[...]) * jnp.dot(ob_ref[...], w_scr[1], preferred_element_type=F32)
    acc += jax.nn.sigmoid(g2_ref[...]) * jnp.dot(oc_ref[...], w_scr[2], preferred_element_type=F32)
    z_ref[...] = acc.astype(BF16)


def merge_branches(o_a, o_b, o_c, w_branch, layer, proj, *, tm=512, tn=1024):
    t = o_a.shape[0]
    gate_blk = COL_MERGE // tn
    per_branch = D_MODEL // tn
    o_spec = pl.BlockSpec((tm, MIX_WIDTH), lambda j, i: (i, 0))

    def gate_spec(m):
        return pl.BlockSpec((tm, tn), lambda j, i: (i, gate_blk + m * per_branch + j))

    return pl.pallas_call(
        _merge_kernel,
        out_shape=jax.ShapeDtypeStruct((t, D_MODEL), BF16),
        grid=(D_MODEL // tn, t // tm),
        in_specs=[o_spec, o_spec, o_spec,
                  pl.BlockSpec((1, 3, MIX_WIDTH, tn), lambda j, i: (layer, 0, 0, j)),
                  gate_spec(0), gate_spec(1), gate_spec(2)],
        out_specs=pl.BlockSpec((tm, tn), lambda j, i: (i, j)),
        scratch_shapes=[pltpu.VMEM((3, MIX_WIDTH, tn), BF16)],
        compiler_params=_cparams(("parallel", "arbitrary")),
        name="merge_branches",
    )(o_a, o_b, o_c, w_branch, proj, proj, proj)


def _matmul_res_kernel(a_ref, w_ref, x_ref, o_ref, w_scr):
    @pl.when(pl.program_id(1) == 0)
    def _():
        w_scr[...] = w_ref[0].astype(BF16)

    o_ref[...] = x_ref[...] + jnp.dot(a_ref[...], w_scr[...], preferred_element_type=F32)


def matmul_residual(a, w, layer, x, *, tm=1024, tn=1024):
    t, k = a.shape
    n = w.shape[2]
    return pl.pallas_call(
        _matmul_res_kernel,
        out_shape=jax.ShapeDtypeStruct((t, n), F32),
        grid=(n // tn, t // tm),
        in_specs=[pl.BlockSpec((tm, k), lambda j, i: (i, 0)),
                  pl.BlockSpec((1, k, tn), lambda j, i: (layer, 0, j)),
                  pl.BlockSpec((tm, tn), lambda j, i: (i, j))],
        out_specs=pl.BlockSpec((tm, tn), lambda j, i: (i, j)),
        scratch_shapes=[pltpu.VMEM((k, tn), BF16)],
        compiler_params=_cparams(("parallel", "arbitrary")),
        name="matmul_residual",
    )(a, w, x)


def _mlp_kernel(x_ref, g_ref, w1_ref, w2_ref, o_ref, h_scr):
    @pl.when(pl.program_id(1) == 0)
    def _():
        x = x_ref[...]
        h_scr[...] = _rms(x, g_ref[...]).astype(BF16)
        o_ref[...] = x

    u = jnp.dot(h_scr[...], w1_ref[0].astype(BF16), preferred_element_type=F32)
    u = jnp.square(jnp.maximum(u, 0.0)).astype(BF16)
    o_ref[...] += jnp.dot(u, w2_ref[0].astype(BF16), preferred_element_type=F32)


def mlp_block(x, gain, w1, w2, layer, *, tm=1024, tf=512):
    t, d = x.shape
    dff = w1.shape[2]
    return pl.pallas_call(
        _mlp_kernel,
        out_shape=jax.ShapeDtypeStruct((t, d), F32),
        grid=(t // tm, dff // tf),
        in_specs=[pl.BlockSpec((tm, d), lambda i, f: (i, 0), pipeline_mode=pl.Buffered(1)),
                  pl.BlockSpec((1, d), lambda i, f: (0, 0)),
                  pl.BlockSpec((1, d, tf), lambda i, f: (layer, 0, f)),
                  pl.BlockSpec((1, tf, d), lambda i, f: (layer, f, 0))],
        out_specs=pl.BlockSpec((tm, d), lambda i, f: (i, 0)),
        scratch_shapes=[pltpu.VMEM((tm, d), BF16)],
        compiler_params=_cparams(("parallel", "arbitrary")),
        name="mlp_block",
    )(x, gain.reshape(1, d), w1, w2)


def _final_norm_kernel(x_ref, g_ref, o_ref):
    o_ref[...] = _rms(x_ref[...], g_ref[...])


def final_norm(x, gain, *, tm=512):
    t, d = x.shape
    return pl.pallas_call(
        _final_norm_kernel,
        out_shape=jax.ShapeDtypeStruct((t, d), F32),
        grid=(t // tm,),
        in_specs=[pl.BlockSpec((tm, d), lambda i: (i, 0)),
                  pl.BlockSpec((1, d), lambda i: (0, 0))],
        out_specs=pl.BlockSpec((tm, d), lambda i: (i, 0)),
        compiler_params=_cparams(("parallel",)),
        name="final_norm",
    )(x, gain.reshape(1, d))


def _t5_bucket(n):
    log_ratio = jnp.log(jnp.maximum(n, 1).astype(F32) / BUCKET_EXACT) / math.log(BUCKET_MAX_DIST / BUCKET_EXACT)
    large = jnp.minimum(BUCKET_EXACT + (log_ratio * (N_BUCKETS - BUCKET_EXACT)).astype(jnp.int32), N_BUCKETS - 1)
    return jnp.where(n < BUCKET_EXACT, n, large)


def _lookup_bias(tbl_ref, head, n):
    bucket = _t5_bucket(n)
    out = jnp.zeros(n.shape, F32)
    for b in range(N_BUCKETS):
        out = jnp.where(bucket == b, tbl_ref[b * TOTAL_HEADS + head], out)
    return out


def _band_bias_kernel(tbl_ref, o_ref):
    l = lax.broadcasted_iota(jnp.int32, (TK, TQ), 0)
    q = lax.broadcasted_iota(jnp.int32, (TK, TQ), 1)
    head = pl.program_id(0)
    prev = _lookup_bias(tbl_ref, head, jnp.maximum(q + TK - l, 0)) * LOG2E
    diag = _lookup_bias(tbl_ref, head, jnp.maximum(q - l, 0)) * LOG2E
    sliding = (head >= MIXER["swa"] * N_HEADS) & (head < (MIXER["swa"] + 1) * N_HEADS)
    o_ref[0] = jnp.where(sliding & (q + TK - l >= SWA_WINDOW), NEG_INF, prev)
    o_ref[1] = jnp.where(l > q, NEG_INF, diag)


def band_bias(tbl_flat):
    return pl.pallas_call(
        _band_bias_kernel,
        out_shape=jax.ShapeDtypeStruct((2, TK, TOTAL_HEADS * TQ), F32),
        grid=(TOTAL_HEADS,),
        in_specs=[pl.BlockSpec(memory_space=pltpu.SMEM)],
        out_specs=pl.BlockSpec((2, TK, TQ), lambda h: (0, 0, h)),
        compiler_params=_cparams(("parallel",)),
        name="band_bias",
    )(tbl_flat)


def _cmp_bias_kernel(tbl_ref, o_ref):
    c = lax.broadcasted_iota(jnp.int32, (LANES, TQ), 0)
    t = pl.program_id(0) * TQ + lax.broadcasted_iota(jnp.int32, (LANES, TQ), 1)
    o_ref[0] = _lookup_bias(tbl_ref, pl.program_id(1), jnp.maximum(t - (c * CMP_STRIDE + CMP_LEN - 1), 0))


def cmp_bias(tbl_flat, s):
    return pl.pallas_call(
        _cmp_bias_kernel,
        out_shape=jax.ShapeDtypeStruct((s // TQ, LANES, N_HEADS * TQ), F32),
        grid=(s // TQ, N_HEADS),
        in_specs=[pl.BlockSpec(memory_space=pltpu.SMEM)],
        out_specs=pl.BlockSpec((1, LANES, TQ), lambda i, h: (i, 0, h)),
        compiler_params=_cparams(("parallel", "parallel")),
        name="cmp_bias",
    )(tbl_flat)


def _compress_kernel(x_ref, pos_ref, w1_ref, w2_ref, o_ref):
    n_half = x_ref.shape[1] // CMP_STRIDE
    half = jnp.concatenate([x_ref[0, pl.ds(l, n_half, stride=CMP_STRIDE), :] for l in range(CMP_STRIDE)], axis=1)
    half_w = CMP_STRIDE * HEAD_DIM
    xa = (half + pos_ref[0, 0:1, :]).astype(BF16)
    xb = (half + pos_ref[0, 1:2, :]).astype(BF16)
    ha = jnp.dot(xa, w1_ref[0, 0, 0:half_w, :].astype(BF16), preferred_element_type=F32)
    hb = jnp.dot(xb, w1_ref[0, 0, half_w:2 * half_w, :].astype(BF16), preferred_element_type=F32)
    hidden = ha + pltpu.roll(hb, n_half - 1, 0)
    act = jax.nn.gelu(hidden, approximate=True).astype(BF16)
    o_ref[0, 0, 0] = jnp.dot(act, w2_ref[0, 0].astype(BF16), preferred_element_type=F32)


def nsa_compress(proj, pos, w1, w2, layer):
    b, s, _ = proj.shape
    n_kv, n_half, width = N_KV, s // CMP_STRIDE, CMP_STRIDE * HEAD_DIM
    col_blk = COL_A_KC // HEAD_DIM
    return pl.pallas_call(
        _compress_kernel,
        out_shape=jax.ShapeDtypeStruct((b, 2, n_kv, n_half, HEAD_DIM), F32),
        grid=(2, b, n_kv),
        in_specs=[pl.BlockSpec((1, s, HEAD_DIM), lambda kv, bi, h: (bi, 0, col_blk + kv * n_kv + h)),
                  pl.BlockSpec((1, 2, width), lambda kv, bi, h: (kv, 0, 0)),
                  pl.BlockSpec((1, 1, 2 * width, CMP_HIDDEN), lambda kv, bi, h: (layer, kv, 0, 0)),
                  pl.BlockSpec((1, 1, CMP_HIDDEN, HEAD_DIM), lambda kv, bi, h: (layer, kv, 0, 0))],
        out_specs=pl.BlockSpec((1, 1, 1, n_half, HEAD_DIM), lambda kv, bi, h: (bi, kv, h, 0, 0)),
        compiler_params=_cparams(("parallel", "parallel", "parallel")),
        name="nsa_compress",
    )(proj, pos, w1, w2)


def _stack_heads(q):
    return jnp.concatenate([q[:, g * HEAD_DIM:(g + 1) * HEAD_DIM] for g in range(GROUP)], axis=0)


def _untranspose_heads(o_t):
    return jnp.concatenate([o_t[:, g * TQ:(g + 1) * TQ].T for g in range(GROUP)], axis=1)


def _head_row(ref, first):
    return jnp.concatenate([jnp.full((1, TQ), ref[first + g] * LOG2E, F32) for g in range(GROUP)], axis=1)


def _split3(x):
    x1 = x.astype(BF16)
    r1 = x - x1.astype(F32)
    x2 = r1.astype(BF16)
    x3 = (r1 - x2.astype(F32)).astype(BF16)
    return x1, x2, x3


def _dot_nt(a, b):
    return lax.dot_general(a, b, (((1,), (1,)), ((), ())), preferred_element_type=F32)


def _rank_select(score, n_cand, n_top):
    idx = lax.broadcasted_iota(jnp.int32, score.shape, 0)
    rank = jnp.zeros(score.shape, jnp.int32)
    for j in range(n_cand):
        other = score[j:j + 1, :]
        ahead = (other > score) | ((other == score) & (j < idx))
        rank += jnp.where(ahead, 1, 0)
    return jnp.where((rank < n_top) & (idx < n_cand), 1.0, 0.0)


def _nsa_cmp_kernel(bias_ref, q_ref, kc_ref, vc_ref, share_ref, o_ref, sel_ref):
    qi = pl.program_id(1)
    share = share_ref[...]
    n_sel = share.shape[0]
    c = lax.broadcasted_iota(jnp.int32, (LANES, COLS), 0)
    t = qi * TQ + (lax.broadcasted_iota(jnp.int32, (LANES, COLS), 1) & (TQ - 1))
    valid = t - (c * CMP_STRIDE + CMP_LEN - 1) >= 0
    blk = lax.broadcasted_iota(jnp.int32, (n_sel, TQ), 0)
    cur = (qi * TQ + lax.broadcasted_iota(jnp.int32, (n_sel, TQ), 1)) // SEL_BLOCK
    forced = (blk == 0) | (blk == cur) | (blk == cur - 1)
    for h in range(N_KV):
        cols = slice(h * COLS, (h + 1) * COLS)
        q4 = _stack_heads(q_ref[0, :, cols]).astype(BF16)
        kc = kc_ref[0, 0, h].astype(BF16)
        vc_t = vc_ref[0, 0, h].T.astype(BF16)
        logits = _dot_nt(kc, q4) * SCALE + bias_ref[0, :, cols]
        logits = jnp.where(valid, logits, NEG_INF)
        m = jnp.max(logits, axis=0, keepdims=True)
        p = jnp.where(valid, jnp.exp(logits - m), 0.0)
        p = p / jnp.maximum(jnp.sum(p, axis=0, keepdims=True), TINY)
        o_ref[0, :, cols] = _untranspose_heads(jnp.dot(vc_t, p.astype(BF16), preferred_element_type=F32))

        p_sum = p[:, 0:TQ]
        for g in range(1, GROUP):
            p_sum = p_sum + p[:, g * TQ:(g + 1) * TQ]
        importance = sum(jnp.dot(share, part, preferred_element_type=F32) for part in _split3(p_sum))
        score = jnp.where(forced, FORCE_SCORE, jnp.where(blk <= cur, importance, NEG_INF))
        sel_ref[0, h] = _rank_select(score, n_sel, min(SEL_TOP_N, n_sel))


def nsa_compressed(bias_c, proj, cmp_tokens, share):
    b, s, _ = proj.shape
    n_cmp = cmp_tokens.shape[3]
    n_sel = share.shape[0]
    return pl.pallas_call(
        _nsa_cmp_kernel,
        out_shape=(jax.ShapeDtypeStruct((b, s, MIX_WIDTH), F32),
                   jax.ShapeDtypeStruct((b, N_KV, n_sel, s), F32)),
        grid=(b, s // TQ),
        in_specs=[pl.BlockSpec((1, n_cmp, N_KV * COLS), lambda bi, i: (i, 0, 0)),
                  pl.BlockSpec((1, TQ, MIX_WIDTH), lambda bi, i: (bi, i, COL_A_Q // MIX_WIDTH)),
                  pl.BlockSpec((1, 1, N_KV, n_cmp, HEAD_DIM), lambda bi, i: (bi, 0, 0, 0, 0)),
                  pl.BlockSpec((1, 1, N_KV, n_cmp, HEAD_DIM), lambda bi, i: (bi, 1, 0, 0, 0)),
                  pl.BlockSpec((n_sel, n_cmp), lambda bi, i: (0, 0))],
        out_specs=(pl.BlockSpec((1, TQ, MIX_WIDTH), lambda bi, i: (bi, i, 0)),
                   pl.BlockSpec((1, N_KV, n_sel, TQ), lambda bi, i: (bi, 0, 0, i))),
        compiler_params=_cparams(("parallel", "parallel")),
        name="nsa_compressed",
    )(bias_c, proj, cmp_tokens, cmp_tokens, share)


def _moba_gate_kernel(q_ref, k_ref, sel_ref, km_scr):
    qi = pl.program_id(1)
    n_blk = k_ref.shape[1] // MOBA_BLOCK

    @pl.when(qi == 0)
    def _():
        km_scr[...] = jnp.zeros_like(km_scr)
        for h in range(N_KV):
            for n in range(n_blk):
                blk = k_ref[0, n * MOBA_BLOCK:(n + 1) * MOBA_BLOCK, h * HEAD_DIM:(h + 1) * HEAD_DIM]
                km_scr[h, n:n + 1, :] = jnp.sum(blk, axis=0, keepdims=True) / MOBA_BLOCK

    n = lax.broadcasted_iota(jnp.int32, (BF16_ROWS, COLS), 0)
    q_blk = (qi * TQ) // MOBA_BLOCK
    for h in range(N_KV):
        q4 = _stack_heads(q_ref[0, :, h * COLS:(h + 1) * COLS])
        q_hi, q_lo, _ = _split3(q4)
        k_hi, k_lo, _ = _split3(km_scr[h])
        gate = _dot_nt(k_hi, q_hi) + _dot_nt(k_lo, q_hi) + _dot_nt(k_hi, q_lo)
        gate = jnp.where(n < q_blk, gate, NEG_INF)
        gate = jnp.where(n < n_blk, gate, BELOW_ALL)
        picked = _rank_select(gate, n_blk, min(MOBA_TOP_K, n_blk - 1))
        sel_ref[0, h, 0] = jnp.where(n < q_blk, picked, 0.0)


def moba_gate(proj):
    b, s, _ = proj.shape
    return pl.pallas_call(
        _moba_gate_kernel,
        out_shape=jax.ShapeDtypeStruct((b, N_KV, s // TQ, BF16_ROWS, COLS), F32),
        grid=(b, s // TQ),
        in_specs=[pl.BlockSpec((1, TQ, MIX_WIDTH), lambda bi, i: (bi, i, COL_C_Q // MIX_WIDTH)),
                  pl.BlockSpec((1, s, KV_W), lambda bi, i: (bi, 0, COL_C_K // KV_W))],
        out_specs=pl.BlockSpec((1, N_KV, 1, BF16_ROWS, COLS), lambda bi, i: (bi, 0, i, 0, 0)),
        scratch_shapes=[pltpu.VMEM((N_KV, BF16_ROWS, HEAD_DIM), F32)],
        compiler_params=_cparams(("parallel", "arbitrary")),
        name="moba_gate",
    )(proj, proj)


def _flash_kernel(mode, *refs):
    n_in = 5 if mode == "win" else 6
    ins, o_ref, scr = refs[:n_in], refs[n_in], refs[n_in + 1:]
    if mode == "swa":
        c31_ref, sink_ref, band_ref, q_ref, k_ref, v_ref = ins
    elif mode == "win":
        c31_ref, band_ref, q_ref, k_ref, v_ref = ins
    else:
        c31_ref, band_ref, q_ref, k_ref, v_ref, msk_ref = ins
    kb_scr, vt_scr, m_scr, l_scr, acc_scr, alpha_scr, pb_scr = (scr[i * N_KV:(i + 1) * N_KV] for i in range(7))
    pend_ref = scr[7 * N_KV]
    s_scr = scr[7 * N_KV + 1:8 * N_KV + 1]
    mb_scr = scr[8 * N_KV + 1:]
    qi = pl.program_id(1)
    n_kb = k_ref.shape[1] // TK
    head0 = MIXER[mode] * N_HEADS

    @pl.when(qi == 0)
    def _():
        for h in range(N_KV):
            cols = slice(h * HEAD_DIM, (h + 1) * HEAD_DIM)
            kb_scr[h][...] = k_ref[0, :, cols].astype(BF16)
            for j in range(n_kb):
                vt_scr[h][j] = v_ref[0, j * TK:(j + 1) * TK, cols].T.astype(BF16)

    q4 = [(_stack_heads(q_ref[0, :, h * COLS:(h + 1) * COLS]) * SCALE2).astype(BF16) for h in range(N_KV)]
    far_bias = [_head_row(c31_ref, head0 + h * GROUP) for h in range(N_KV)]

    def penalty(keep):
        return (keep - 1.0) * (-NEG_INF)

    def expand(rows, seg):
        return jnp.concatenate([jnp.broadcast_to(r, (seg, COLS)) for r in rows], axis=0)

    def sel_rows(h, first_blk, n_blocks):
        per = TK // SEL_BLOCK
        rows = []
        for part in range(n_blocks * per):
            r = msk_ref[0, h, pl.ds(first_blk * per + part, 1), :]
            rows.append(jnp.concatenate([penalty(r)] * GROUP, axis=1))
        return rows

    def moba_row(h, key_blk):
        return msk_ref[0, h, 0, pl.ds((key_blk * TK) // MOBA_BLOCK, 1), :]

    n_far = jnp.maximum(qi - 1, 0)

    def far_bias_tile(h, k0):
        tail = jnp.where(k0 + 1 >= n_far, NEG_INF, 0.0)
        if mode == "sel":
            rows = [far_bias[h] + r for r in sel_rows(h, k0, 2)]
            per = TK // SEL_BLOCK
            return expand(rows[:per] + [r + tail for r in rows[per:]], SEL_BLOCK)
        row = far_bias[h] + penalty(moba_row(h, k0))
        return expand([row, row + tail], TK)

    def scores(h, k0, n_blocks, bias):
        start = pl.multiple_of(k0 * TK, TK)
        return _dot_nt(kb_scr[h][pl.ds(start, n_blocks * TK), :], q4[h]) + bias

    def flush():
        k0 = pend_ref[0]
        for h in range(N_KV):
            v_t = jnp.concatenate([vt_scr[h][k0], vt_scr[h][k0 + 1]], axis=1)
            pv = jnp.dot(v_t, pb_scr[h][...], preferred_element_type=F32)
            acc_scr[h][...] = alpha_scr[h][...] * acc_scr[h][...] + pv

    def softmax(h, s, m_blk, first):
        n_keys = s.shape[0]
        if first:
            m_new = m_blk
        else:
            m_prev = m_scr[h][...]
            m_new = jnp.maximum(m_prev, m_blk)
            alpha = jnp.exp2(m_prev - m_new)
        p = jnp.exp2(s - m_new)
        p_sum = jnp.sum(p, axis=0, keepdims=True)
        pb_scr[h][0:n_keys, :] = p.astype(BF16)
        if n_keys == TK:
            pb_scr[h][TK:2 * TK, :] = jnp.zeros((TK, COLS), BF16)
        if first:
            l_scr[h][...] = p_sum
            alpha_scr[h][...] = jnp.zeros((1, COLS), F32)
            acc_scr[h][...] = jnp.zeros((HEAD_DIM, COLS), F32)
        else:
            l_scr[h][...] = alpha * l_scr[h][...] + p_sum
            alpha_scr[h][...] = alpha
        m_scr[h][...] = m_new

    def step(k0, n_blocks, bias, first=False):
        if not first:
            flush()
        for h in range(N_KV):
            s = scores(h, k0, n_blocks, bias[h])
            softmax(h, s, jnp.max(s, axis=0, keepdims=True), first)
        pend_ref[0] = k0

    def produce(pair):
        for h in range(N_KV):
            s = scores(h, 2 * pair, 2, far_bias_tile(h, 2 * pair))
            s_scr[h][...] = s
            mb_scr[h][...] = jnp.max(s, axis=0, keepdims=True)

    @pl.when(qi == 0)
    def _():
        bias = [band_ref[1, :, h * COLS:(h + 1) * COLS] for h in range(N_KV)]
        if mode == "sel":
            bias = [bias[h] + expand(sel_rows(h, 0, 1), SEL_BLOCK) for h in range(N_KV)]
        step(0, 1, bias, first=True)

    @pl.when(qi >= 1)
    def _():
        bias = [band_ref[:, :, h * COLS:(h + 1) * COLS].reshape(2 * TK, COLS) for h in range(N_KV)]
        if mode == "sel":
            bias = [bias[h] + expand(sel_rows(h, qi - 1, 2), SEL_BLOCK) for h in range(N_KV)]
        elif mode == "moba":
            own = (qi * TQ) // MOBA_BLOCK == ((qi - 1) * TK) // MOBA_BLOCK
            own_f = jnp.where(own, 1.0, 0.0)
            bias = [bias[h] + expand([penalty(jnp.minimum(moba_row(h, qi - 1) + own_f, 1.0)),
                                      jnp.zeros((1, COLS), F32)], TK) for h in range(N_KV)]
        step(qi - 1, 2, bias, first=True)
        if mode in ("sel", "moba"):
            produce(0)

    masked_rows = jnp.full((TK, COLS), NEG_INF, F32)
    if mode in ("sel", "moba"):
        n_pairs = (n_far + 1) >> 1

        def pair_body(pair, carry):
            flush()
            for h in range(N_KV):
                softmax(h, s_scr[h][...], mb_scr[h][...], False)
            pend_ref[0] = 2 * pair
            produce(jnp.minimum(pair + 1, n_pairs - 1))
            return carry
        lax.fori_loop(0, n_pairs, pair_body, 0)
    elif mode == "win":
        n_full = NSA_WINDOW // TK

        @pl.when(qi >= 3)
        def _():
            step(qi - 3, 2, far_bias)

        @pl.when(qi == 2)
        def _():
            step(0, 2, [jnp.concatenate([jnp.broadcast_to(far_bias[h], (TK, COLS)), masked_rows], axis=0)
                        for h in range(N_KV)])

        @pl.when(qi >= n_full)
        def _():
            key = lax.broadcasted_iota(jnp.int32, (TK, COLS), 0)
            qry = lax.broadcasted_iota(jnp.int32, (TK, COLS), 1) & (TQ - 1)
            step(qi - n_full, 2, [jnp.concatenate([jnp.where(key > qry, far_bias[h], NEG_INF), masked_rows], axis=0)
                                  for h in range(N_KV)])

    flush()
    for h in range(N_KV):
        m = m_scr[h][...]
        l = l_scr[h][...]
        acc = acc_scr[h][...]
        if mode == "swa":
            sink = _head_row(sink_ref, h * GROUP)
            m_fin = jnp.maximum(m, sink)
            shrink = jnp.exp2(m - m_fin)
            l = l * shrink + jnp.exp2(sink - m_fin)
            acc = acc * shrink
        out = acc / jnp.maximum(l, TINY)
        o_ref[0, :, h * COLS:(h + 1) * COLS] = _untranspose_heads(out).astype(o_ref.dtype)


def flash_attention(mode, proj, band, c31, *, q_col, k_col, v_col, out_dtype, mask=None, sinks=None):
    b, s, _ = proj.shape
    smem = pl.BlockSpec(memory_space=pltpu.SMEM)
    in_specs = [smem]
    args = [c31]
    if mode == "swa":
        in_specs.append(smem)
        args.append(sinks)
    in_specs += [pl.BlockSpec((2, TK, N_KV * COLS), lambda bi, i: (0, 0, MIXER[mode])),
                 pl.BlockSpec((1, TQ, MIX_WIDTH), lambda bi, i: (bi, i, q_col // MIX_WIDTH)),
                 pl.BlockSpec((1, s, KV_W), lambda bi, i: (bi, 0, k_col // KV_W)),
                 pl.BlockSpec((1, s, KV_W), lambda bi, i: (bi, 0, v_col // KV_W))]
    args += [band, proj, proj, proj]
    if mode == "sel":
        in_specs.append(pl.BlockSpec((1, N_KV, mask.shape[2], TQ), lambda bi, i: (bi, 0, 0, i)))
        args.append(mask)
    elif mode == "moba":
        in_specs.append(pl.BlockSpec((1, N_KV, 1, mask.shape[3], COLS), lambda bi, i: (bi, 0, i, 0, 0)))
        args.append(mask)
    per_head = [pltpu.VMEM((s, HEAD_DIM), BF16), pltpu.VMEM((s // TK, HEAD_DIM, TK), BF16),
                pltpu.VMEM((1, COLS), F32), pltpu.VMEM((1, COLS), F32), pltpu.VMEM((HEAD_DIM, COLS), F32),
                pltpu.VMEM((1, COLS), F32), pltpu.VMEM((2 * TK, COLS), BF16)]
    scratch = [shape for shape in per_head for _ in range(N_KV)] + [pltpu.SMEM((1,), jnp.int32)]
    if mode in ("sel", "moba"):
        scratch += [pltpu.VMEM((2 * TK, COLS), F32)] * N_KV + [pltpu.VMEM((1, COLS), F32)] * N_KV
    return pl.pallas_call(
        functools.partial(_flash_kernel, mode),
        out_shape=jax.ShapeDtypeStruct((b, s, MIX_WIDTH), out_dtype),
        grid=(b, s // TQ),
        in_specs=in_specs,
        out_specs=pl.BlockSpec((1, TQ, MIX_WIDTH), lambda bi, i: (bi, i, 0)),
        scratch_shapes=scratch,
        compiler_params=_cparams(("parallel", "arbitrary")),
        name="flash_" + mode,
    )(*args)


def _nsa_combine_kernel(cmp_ref, sel_ref, win_ref, gate_ref, o_ref):
    gate = jax.nn.sigmoid(gate_ref[...])
    for h in range(N_HEADS):
        cols = slice(h * HEAD_DIM, (h + 1) * HEAD_DIM)
        out = (gate[:, 3 * h:3 * h + 1] * cmp_ref[:, cols]
               + gate[:, 3 * h + 1:3 * h + 2] * sel_ref[:, cols]
               + gate[:, 3 * h + 2:3 * h + 3] * win_ref[:, cols])
        o_ref[:, cols] = out.astype(BF16)


def nsa_combine(o_cmp, o_sel, o_win, proj, *, tm=512):
    t = o_cmp.shape[0]
    o_spec = pl.BlockSpec((tm, MIX_WIDTH), lambda i: (i, 0))
    return pl.pallas_call(
        _nsa_combine_kernel,
        out_shape=jax.ShapeDtypeStruct((t, MIX_WIDTH), BF16),
        grid=(t // tm,),
        in_specs=[o_spec, o_spec, o_spec,
                  pl.BlockSpec((tm, LANES), lambda i: (i, COL_A_GATE // LANES))],
        out_specs=o_spec,
        compiler_params=_cparams(("parallel",)),
        name="nsa_combine",
    )(o_cmp, o_sel, o_win, proj)


def _selection_share(n_sel, n_half):
    blk = np.arange(n_sel)[:, None] * SEL_BLOCK
    starts = np.arange(n_half)[None, :] * CMP_STRIDE
    shared = np.clip(np.minimum(starts + CMP_LEN, blk + SEL_BLOCK) - np.maximum(starts, blk), 0, None)
    shared = shared / CMP_STRIDE
    shared[:, n_half - 1] = 0.0
    return shared.astype(np.float32)


REGROUP_TILE = 512


def _regroup_tables():
    q_w, kv_w = MIX_WIDTH, KV_W
    a_kv = q_w
    a_gate = a_kv + 6 * kv_w
    b_q = a_gate + ORIG_GATE_W
    b_kv = b_q + q_w
    c_q = b_kv + 2 * kv_w
    c_kv = c_q + q_w
    merge = c_kv + 2 * kv_w
    segments = [(COL_A_Q, q_w, 0), (COL_B_Q, q_w, b_q), (COL_C_Q, q_w, c_q), (COL_A_KC, 6 * kv_w, a_kv),
                (COL_B_K, 2 * kv_w, b_kv), (COL_C_K, 2 * kv_w, c_kv), (COL_A_GATE, COL_MERGE - COL_A_GATE, a_gate),
                (COL_MERGE, 3 * D_MODEL, merge)]
    main, tail, kind = [], [], []
    for new0, width, orig0 in segments:
        for off in range(0, width, REGROUP_TILE):
            shift = (orig0 + off) % LANES
            aligned = orig0 + off - shift
            assert aligned % REGROUP_TILE == 0 and shift in (0, ORIG_GATE_W)
            main.append(aligned // REGROUP_TILE)
            tail.append((aligned + REGROUP_TILE) // LANES if shift else aligned // LANES)
            kind.append(2 if new0 == COL_A_GATE else (1 if shift else 0))
    return (np.asarray(main, np.int32), np.asarray(tail, np.int32), np.asarray(kind, np.int32))


def _regroup_kernel(main_tbl, tail_tbl, kind_tbl, main_ref, tail_ref, o_ref):
    kind = kind_tbl[pl.program_id(2)]
    main = main_ref[0]

    @pl.when(kind == 0)
    def _():
        o_ref[0] = main.astype(BF16)

    @pl.when(kind == 1)
    def _():
        shifted = jnp.concatenate([main[:, ORIG_GATE_W:], tail_ref[0][:, :ORIG_GATE_W]], axis=1)
        o_ref[0] = shifted.astype(BF16)

    @pl.when(kind == 2)
    def _():
        lane = lax.broadcasted_iota(jnp.int32, main.shape, 1)
        o_ref[0] = jnp.where(lane < ORIG_GATE_W, main, 0.0).astype(BF16)


def regroup_w_in(w, *, tr=1024):
    depth, d, _ = w.shape
    tables = [jnp.asarray(t) for t in _regroup_tables()]
    return pl.pallas_call(
        _regroup_kernel,
        out_shape=jax.ShapeDtypeStruct((depth, d, N_PROJ), BF16),
        grid_spec=pltpu.PrefetchScalarGridSpec(
            num_scalar_prefetch=3,
            grid=(depth, d // tr, N_PROJ // REGROUP_TILE),
            in_specs=[pl.BlockSpec((1, tr, REGROUP_TILE), lambda l, i, j, mt, tt, kt: (l, i, mt[j])),
                      pl.BlockSpec((1, tr, LANES), lambda l, i, j, mt, tt, kt: (l, i, tt[j]))],
            out_specs=pl.BlockSpec((1, tr, REGROUP_TILE), lambda l, i, j, mt, tt, kt: (l, i, j))),
        compiler_params=_cparams(("parallel", "parallel", "parallel")),
        name="regroup_w_in",
    )(*tables, w, w)


def kernel(x, w_in, cmp_pos, cmp_w1, cmp_w2, swa_sinks, w_branch, w_out, w_mlp_in, w_mlp_out,
           norm_mix, norm_mlp, norm_final, rel_bias):
    b, s, d = x.shape
    depth = w_in.shape[0]
    t = b * s
    n_half = s // CMP_STRIDE
    n_sel = s // SEL_BLOCK

    tbl_flat = rel_bias.reshape(-1)
    c31 = rel_bias[N_BUCKETS - 1]
    band = band_bias(tbl_flat)
    bias_c = cmp_bias(tbl_flat, s)
    share = jnp.asarray(_selection_share(n_sel, n_half), BF16)

    w_in_cols = regroup_w_in(w_in)
    xt = x.reshape(t, d)
    for layer in range(depth):
        proj2d = norm_matmul(xt, norm_mix[layer], w_in_cols, layer)
        proj = proj2d.reshape(b, s, N_PROJ)

        cmp_tokens = nsa_compress(proj, cmp_pos[layer].reshape(2, 2, CMP_STRIDE * HEAD_DIM), cmp_w1, cmp_w2, layer)
        o_cmp, sel_mask = nsa_compressed(bias_c, proj, cmp_tokens, share)
        o_sel = flash_attention("sel", proj, band, c31, q_col=COL_A_Q, k_col=COL_A_KS, v_col=COL_A_VS,
                                out_dtype=F32, mask=sel_mask)
        o_win = flash_attention("win", proj, band, c31, q_col=COL_A_Q, k_col=COL_A_KW, v_col=COL_A_VW,
                                out_dtype=F32)
        o_a = nsa_combine(o_cmp.reshape(t, MIX_WIDTH), o_sel.reshape(t, MIX_WIDTH),
                          o_win.reshape(t, MIX_WIDTH), proj2d)

        o_b = flash_attention("swa", proj, band, c31, q_col=COL_B_Q, k_col=COL_B_K, v_col=COL_B_V,
                              out_dtype=BF16, sinks=swa_sinks[layer])
        moba_mask = moba_gate(proj)
        o_c = flash_attention("moba", proj, band, c31, q_col=COL_C_Q, k_col=COL_C_K, v_col=COL_C_V,
                              out_dtype=BF16, mask=moba_mask)

        z = merge_branches(o_a, o_b.reshape(t, MIX_WIDTH), o_c.reshape(t, MIX_WIDTH), w_branch, layer, proj2d)
        xt = matmul_residual(z, w_out, layer, xt)
        xt = mlp_block(xt, norm_mlp[layer], w_mlp_in, w_mlp_out, layer)

    return final_norm(xt, norm_final).reshape(b, s, d)
```

```python
import functools
import math

import numpy as np
import jax
import jax.numpy as jnp
from jax import lax
from jax.experimental import pallas as pl
from jax.experimental.pallas import tpu as pltpu

F32 = jnp.float32
BF16 = jnp.bfloat16

D_MODEL = 2048
HEAD_DIM = 128
N_HEADS = 8
N_KV = 2
GROUP = N_HEADS // N_KV
MIX_WIDTH = N_HEADS * HEAD_DIM
CMP_LEN = 32
CMP_STRIDE = 16
CMP_HIDDEN = 2 * HEAD_DIM
SEL_BLOCK = 64
SEL_TOP_N = 16
NSA_WINDOW = 512
SWA_WINDOW = 128
MOBA_BLOCK = 256
MOBA_TOP_K = 3
N_BUCKETS = 32
BUCKET_EXACT = N_BUCKETS // 2
BUCKET_MAX_DIST = 128
TOTAL_HEADS = 3 * N_HEADS
D_FF = 4 * D_MODEL
RMS_EPS = 1e-6
NEG_INF = -1e30
FORCE_SCORE = 1e30
TINY = 1e-30
BELOW_ALL = -3e38
SCALE = HEAD_DIM ** -0.5

LANES = 128
BF16_ROWS = 16
VMEM_LIMIT = 56 * 1024 * 1024
TQ = 128
TK = 128
COLS = GROUP * TQ
KV_W = N_KV * HEAD_DIM
LOG2E = math.log2(math.e)
SCALE2 = SCALE * LOG2E

COL_A_Q = 0
COL_B_Q = 1024
COL_C_Q = 2048
COL_A_KC = 3072
COL_A_KS = 3584
COL_A_VS = 3840
COL_A_KW = 4096
COL_A_VW = 4352
COL_B_K = 4608
COL_B_V = 4864
COL_C_K = 5120
COL_C_V = 5376
COL_A_GATE = 5632
COL_MERGE = 6144
N_PROJ = COL_MERGE + 3 * D_MODEL
ORIG_GATE_W = 3 * N_HEADS

MIXER = {"sel": 0, "win": 0, "swa": 1, "moba": 2}


def _cparams(semantics):
    return pltpu.CompilerParams(dimension_semantics=semantics, vmem_limit_bytes=VMEM_LIMIT)


def _rms(x, gain):
    y = x * lax.rsqrt(jnp.mean(x * x, axis=-1, keepdims=True) + RMS_EPS)
    return y * gain


def _norm_matmul_kernel(x_ref, g_ref, w_ref, o_ref, h_scr):
    @pl.when(pl.program_id(1) == 0)
    def _():
        h_scr[...] = _rms(x_ref[...], g_ref[...]).astype(BF16)

    o_ref[...] = jnp.dot(h_scr[...], w_ref[0], preferred_element_type=F32)


def norm_matmul(x, gain, w, layer, *, tm=1024, tn=1024):
    t, k = x.shape
    n = w.shape[2]
    return pl.pallas_call(
        _norm_matmul_kernel,
        out_shape=jax.ShapeDtypeStruct((t, n), F32),
        grid=(t // tm, n // tn),
        in_specs=[pl.BlockSpec((tm, k), lambda i, j: (i, 0)),
                  pl.BlockSpec((1, k), lambda i, j: (0, 0)),
                  pl.BlockSpec((1, k, tn), lambda i, j: (layer, 0, j))],
        out_specs=pl.BlockSpec((tm, tn), lambda i, j: (i, j)),
        scratch_shapes=[pltpu.VMEM((tm, k), BF16)],
        compiler_params=_cparams(("parallel", "arbitrary")),
        name="norm_matmul",
    )(x, gain.reshape(1, k), w)


def _merge_kernel(oa_ref, ob_ref, oc_ref, wb_ref, g0_ref, g1_ref, g2_ref, z_ref, w_scr):
    @pl.when(pl.program_id(1) == 0)
    def _():
        w_scr[...] = wb_ref[0].astype(BF16)

    acc = jax.nn.sigmoid(g0_ref[...]) * jnp.dot(oa_ref[...], w_scr[0], preferred_element_type=F32)
    acc += jax.nn.sigmoid(g1_ref[...]) * jnp.dot(ob_ref[...], w_scr[1], preferred_element_type=F32)
    acc += jax.nn.sigmoid(g2_ref[...]) * jnp.dot(oc_ref[...], w_scr[2], preferred_element_type=F32)
    z_ref[...] = acc.astype(BF16)


def merge_branches(o_a, o_b, o_c, w_branch, layer, proj, *, tm=512, tn=1024):
    t = o_a.shape[0]
    gate_blk = COL_MERGE // tn
    per_branch = D_MODEL // tn
    o_spec = pl.BlockSpec((tm, MIX_WIDTH), lambda j, i: (i, 0))

    def gate_spec(m):
        return pl.BlockSpec((tm, tn), lambda j, i: (i, gate_blk + m * per_branch + j))

    return pl.pallas_call(
        _merge_kernel,
        out_shape=jax.ShapeDtypeStruct((t, D_MODEL), BF16),
        grid=(D_MODEL // tn, t // tm),
        in_specs=[o_spec, o_spec, o_spec,
                  pl.BlockSpec((1, 3, MIX_WIDTH, tn), lambda j, i: (layer, 0, 0, j)),
                  gate_spec(0), gate_spec(1), gate_spec(2)],
        out_specs=pl.BlockSpec((tm, tn), lambda j, i: (i, j)),
        scratch_shapes=[pltpu.VMEM((3, MIX_WIDTH, tn), BF16)],
        compiler_params=_cparams(("parallel", "arbitrary")),
        name="merge_branches",
    )(o_a, o_b, o_c, w_branch, proj, proj, proj)


def _matmul_res_kernel(a_ref, w_ref, x_ref, o_ref, w_scr):
    @pl.when(pl.program_id(1) == 0)
    def _():
        w_scr[...] = w_ref[0].astype(BF16)

    o_ref[...] = x_ref[...] + jnp.dot(a_ref[...], w_scr[...], preferred_element_type=F32)


def matmul_residual(a, w, layer, x, *, tm=1024, tn=1024):
    t, k = a.shape
    n = w.shape[2]
    return pl.pallas_call(
        _matmul_res_kernel,
        out_shape=jax.ShapeDtypeStruct((t, n), F32),
        grid=(n // tn, t // tm),
        in_specs=[pl.BlockSpec((tm, k), lambda j, i: (i, 0)),
                  pl.BlockSpec((1, k, tn), lambda j, i: (layer, 0, j)),
                  pl.BlockSpec((tm, tn), lambda j, i: (i, j))],
        out_specs=pl.BlockSpec((tm, tn), lambda j, i: (i, j)),
        scratch_shapes=[pltpu.VMEM((k, tn), BF16)],
        compiler_params=_cparams(("parallel", "arbitrary")),
        name="matmul_residual",
    )(a, w, x)


def _mlp_kernel(x_ref, g_ref, w1_ref, w2_ref, o_ref, h_scr):
    @pl.when(pl.program_id(1) == 0)
    def _():
        x = x_ref[...]
        h_scr[...] = _rms(x, g_ref[...]).astype(BF16)
        o_ref[...] = x

    u = jnp.dot(h_scr[...], w1_ref[0].astype(BF16), preferred_element_type=F32)
    u = jnp.square(jnp.maximum(u, 0.0)).astype(BF16)
    o_ref[...] += jnp.dot(u, w2_ref[0].astype(BF16), preferred_element_type=F32)


def mlp_block(x, gain, w1, w2, layer, *, tm=1024, tf=512):
    t, d = x.shape
    dff = w1.shape[2]
    return pl.pallas_call(
        _mlp_kernel,
        out_shape=jax.ShapeDtypeStruct((t, d), F32),
        grid=(t // tm, dff // tf),
        in_specs=[pl.BlockSpec((tm, d), lambda i, f: (i, 0)),
                  pl.BlockSpec((1, d), lambda i, f: (0, 0)),
                  pl.BlockSpec((1, d, tf), lambda i, f: (layer, 0, f)),
                  pl.BlockSpec((1, tf, d), lambda i, f: (layer, f, 0))],
        out_specs=pl.BlockSpec((tm, d), lambda i, f: (i, 0)),
        scratch_shapes=[pltpu.VMEM((tm, d), BF16)],
        compiler_params=_cparams(("parallel", "arbitrary")),
        name="mlp_block",
    )(x, gain.reshape(1, d), w1, w2)


def _final_norm_kernel(x_ref, g_ref, o_ref):
    o_ref[...] = _rms(x_ref[...], g_ref[...])


def final_norm(x, gain, *, tm=512):
    t, d = x.shape
    return pl.pallas_call(
        _final_norm_kernel,
        out_shape=jax.ShapeDtypeStruct((t, d), F32),
        grid=(t // tm,),
        in_specs=[pl.BlockSpec((tm, d), lambda i: (i, 0)),
                  pl.BlockSpec((1, d), lambda i: (0, 0))],
        out_specs=pl.BlockSpec((tm, d), lambda i: (i, 0)),
        compiler_params=_cparams(("parallel",)),
        name="final_norm",
    )(x, gain.reshape(1, d))


def _t5_bucket(n):
    log_ratio = jnp.log(jnp.maximum(n, 1).astype(F32) / BUCKET_EXACT) / math.log(BUCKET_MAX_DIST / BUCKET_EXACT)
    large = jnp.minimum(BUCKET_EXACT + (log_ratio * (N_BUCKETS - BUCKET_EXACT)).astype(jnp.int32), N_BUCKETS - 1)
    return jnp.where(n < BUCKET_EXACT, n, large)


def _lookup_bias(tbl_ref, head, n):
    bucket = _t5_bucket(n)
    out = jnp.zeros(n.shape, F32)
    for b in range(N_BUCKETS):
        out = jnp.where(bucket == b, tbl_ref[b * TOTAL_HEADS + head], out)
    return out


def _band_bias_kernel(tbl_ref, o_ref):
    l = lax.broadcasted_iota(jnp.int32, (TK, TQ), 0)
    q = lax.broadcasted_iota(jnp.int32, (TK, TQ), 1)
    head = pl.program_id(0)
    prev = _lookup_bias(tbl_ref, head, jnp.maximum(q + TK - l, 0)) * LOG2E
    diag = _lookup_bias(tbl_ref, head, jnp.maximum(q - l, 0)) * LOG2E
    sliding = (head >= MIXER["swa"] * N_HEADS) & (head < (MIXER["swa"] + 1) * N_HEADS)
    o_ref[0] = jnp.where(sliding & (q + TK - l >= SWA_WINDOW), NEG_INF, prev)
    o_ref[1] = jnp.where(l > q, NEG_INF, diag)


def band_bias(tbl_flat):
    return pl.pallas_call(
        _band_bias_kernel,
        out_shape=jax.ShapeDtypeStruct((2, TK, TOTAL_HEADS * TQ), F32),
        grid=(TOTAL_HEADS,),
        in_specs=[pl.BlockSpec(memory_space=pltpu.SMEM)],
        out_specs=pl.BlockSpec((2, TK, TQ), lambda h: (0, 0, h)),
        compiler_params=_cparams(("parallel",)),
        name="band_bias",
    )(tbl_flat)


def _cmp_bias_kernel(tbl_ref, o_ref):
    c = lax.broadcasted_iota(jnp.int32, (LANES, TQ), 0)
    t = pl.program_id(0) * TQ + lax.broadcasted_iota(jnp.int32, (LANES, TQ), 1)
    o_ref[0] = _lookup_bias(tbl_ref, pl.program_id(1), jnp.maximum(t - (c * CMP_STRIDE + CMP_LEN - 1), 0))


def cmp_bias(tbl_flat, s):
    return pl.pallas_call(
        _cmp_bias_kernel,
        out_shape=jax.ShapeDtypeStruct((s // TQ, LANES, N_HEADS * TQ), F32),
        grid=(s // TQ, N_HEADS),
        in_specs=[pl.BlockSpec(memory_space=pltpu.SMEM)],
        out_specs=pl.BlockSpec((1, LANES, TQ), lambda i, h: (i, 0, h)),
        compiler_params=_cparams(("parallel", "parallel")),
        name="cmp_bias",
    )(tbl_flat)


def _compress_kernel(x_ref, pos_ref, w1_ref, w2_ref, o_ref):
    n_half = x_ref.shape[1] // CMP_STRIDE
    half = jnp.concatenate([x_ref[0, pl.ds(l, n_half, stride=CMP_STRIDE), :] for l in range(CMP_STRIDE)], axis=1)
    half_w = CMP_STRIDE * HEAD_DIM
    xa = (half + pos_ref[0, 0:1, :]).astype(BF16)
    xb = (half + pos_ref[0, 1:2, :]).astype(BF16)
    ha = jnp.dot(xa, w1_ref[0, 0, 0:half_w, :].astype(BF16), preferred_element_type=F32)
    hb = jnp.dot(xb, w1_ref[0, 0, half_w:2 * half_w, :].astype(BF16), preferred_element_type=F32)
    hidden = ha + pltpu.roll(hb, n_half - 1, 0)
    act = jax.nn.gelu(hidden, approximate=True).astype(BF16)
    o_ref[0, 0, 0] = jnp.dot(act, w2_ref[0, 0].astype(BF16), preferred_element_type=F32)


def nsa_compress(proj, pos, w1, w2, layer):
    b, s, _ = proj.shape
    n_kv, n_half, width = N_KV, s // CMP_STRIDE, CMP_STRIDE * HEAD_DIM
    col_blk = COL_A_KC // HEAD_DIM
    return pl.pallas_call(
        _compress_kernel,
        out_shape=jax.ShapeDtypeStruct((b, 2, n_kv, n_half, HEAD_DIM), F32),
        grid=(2, b, n_kv),
        in_specs=[pl.BlockSpec((1, s, HEAD_DIM), lambda kv, bi, h: (bi, 0, col_blk + kv * n_kv + h)),
                  pl.BlockSpec((1, 2, width), lambda kv, bi, h: (kv, 0, 0)),
                  pl.BlockSpec((1, 1, 2 * width, CMP_HIDDEN), lambda kv, bi, h: (layer, kv, 0, 0)),
                  pl.BlockSpec((1, 1, CMP_HIDDEN, HEAD_DIM), lambda kv, bi, h: (layer, kv, 0, 0))],
        out_specs=pl.BlockSpec((1, 1, 1, n_half, HEAD_DIM), lambda kv, bi, h: (bi, kv, h, 0, 0)),
        compiler_params=_cparams(("parallel", "parallel", "parallel")),
        name="nsa_compress",
    )(proj, pos, w1, w2)


def _stack_heads(q):
    return jnp.concatenate([q[:, g * HEAD_DIM:(g + 1) * HEAD_DIM] for g in range(GROUP)], axis=0)


def _untranspose_heads(o_t):
    return jnp.concatenate([o_t[:, g * TQ:(g + 1) * TQ].T for g in range(GROUP)], axis=1)


def _head_row(ref, first):
    return jnp.concatenate([jnp.full((1, TQ), ref[first + g] * LOG2E, F32) for g in range(GROUP)], axis=1)


def _split3(x):
    x1 = x.astype(BF16)
    r1 = x - x1.astype(F32)
    x2 = r1.astype(BF16)
    x3 = (r1 - x2.astype(F32)).astype(BF16)
    return x1, x2, x3


def _dot_nt(a, b):
    return lax.dot_general(a, b, (((1,), (1,)), ((), ())), preferred_element_type=F32)


def _rank_select(score, n_cand, n_top):
    idx = lax.broadcasted_iota(jnp.int32, score.shape, 0)
    rank = jnp.zeros(score.shape, jnp.int32)
    for j in range(n_cand):
        other = score[j:j + 1, :]
        ahead = (other > score) | ((other == score) & (j < idx))
        rank += jnp.where(ahead, 1, 0)
    return jnp.where((rank < n_top) & (idx < n_cand), 1.0, 0.0)


def _nsa_cmp_kernel(bias_ref, q_ref, kc_ref, vc_ref, share_ref, o_ref, sel_ref):
    qi = pl.program_id(1)
    share = share_ref[...]
    n_sel = share.shape[0]
    c = lax.broadcasted_iota(jnp.int32, (LANES, COLS), 0)
    t = qi * TQ + (lax.broadcasted_iota(jnp.int32, (LANES, COLS), 1) & (TQ - 1))
    valid = t - (c * CMP_STRIDE + CMP_LEN - 1) >= 0
    blk = lax.broadcasted_iota(jnp.int32, (n_sel, TQ), 0)
    cur = (qi * TQ + lax.broadcasted_iota(jnp.int32, (n_sel, TQ), 1)) // SEL_BLOCK
    forced = (blk == 0) | (blk == cur) | (blk == cur - 1)
    for h in range(N_KV):
        cols = slice(h * COLS, (h + 1) * COLS)
        q4 = _stack_heads(q_ref[0, :, cols]).astype(BF16)
        kc = kc_ref[0, 0, h].astype(BF16)
        vc_t = vc_ref[0, 0, h].T.astype(BF16)
        logits = _dot_nt(kc, q4) * SCALE + bias_ref[0, :, cols]
        logits = jnp.where(valid, logits, NEG_INF)
        m = jnp.max(logits, axis=0, keepdims=True)
        p = jnp.where(valid, jnp.exp(logits - m), 0.0)
        p = p / jnp.maximum(jnp.sum(p, axis=0, keepdims=True), TINY)
        o_ref[0, :, cols] = _untranspose_heads(jnp.dot(vc_t, p.astype(BF16), preferred_element_type=F32))

        p_sum = p[:, 0:TQ]
        for g in range(1, GROUP):
            p_sum = p_sum + p[:, g * TQ:(g + 1) * TQ]
        importance = sum(jnp.dot(share, part, preferred_element_type=F32) for part in _split3(p_sum))
        score = jnp.where(forced, FORCE_SCORE, jnp.where(blk <= cur, importance, NEG_INF))
        sel_ref[0, h] = _rank_select(score, n_sel, min(SEL_TOP_N, n_sel))


def nsa_compressed(bias_c, proj, cmp_tokens, share):
    b, s, _ = proj.shape
    n_cmp = cmp_tokens.shape[3]
    n_sel = share.shape[0]
    return pl.pallas_call(
        _nsa_cmp_kernel,
        out_shape=(jax.ShapeDtypeStruct((b, s, MIX_WIDTH), F32),
                   jax.ShapeDtypeStruct((b, N_KV, n_sel, s), F32)),
        grid=(b, s // TQ),
        in_specs=[pl.BlockSpec((1, n_cmp, N_KV * COLS), lambda bi, i: (i, 0, 0)),
                  pl.BlockSpec((1, TQ, MIX_WIDTH), lambda bi, i: (bi, i, COL_A_Q // MIX_WIDTH)),
                  pl.BlockSpec((1, 1, N_KV, n_cmp, HEAD_DIM), lambda bi, i: (bi, 0, 0, 0, 0)),
                  pl.BlockSpec((1, 1, N_KV, n_cmp, HEAD_DIM), lambda bi, i: (bi, 1, 0, 0, 0)),
                  pl.BlockSpec((n_sel, n_cmp), lambda bi, i: (0, 0))],
        out_specs=(pl.BlockSpec((1, TQ, MIX_WIDTH), lambda bi, i: (bi, i, 0)),
                   pl.BlockSpec((1, N_KV, n_sel, TQ), lambda bi, i: (bi, 0, 0, i))),
        compiler_params=_cparams(("parallel", "parallel")),
        name="nsa_compressed",
    )(bias_c, proj, cmp_tokens, cmp_tokens, share)


def _moba_gate_kernel(q_ref, k_ref, sel_ref, km_scr):
    qi = pl.program_id(1)
    n_blk = k_ref.shape[1] // MOBA_BLOCK

    @pl.when(qi == 0)
    def _():
        km_scr[...] = jnp.zeros_like(km_scr)
        for h in range(N_KV):
            for n in range(n_blk):
                blk = k_ref[0, n * MOBA_BLOCK:(n + 1) * MOBA_BLOCK, h * HEAD_DIM:(h + 1) * HEAD_DIM]
                km_scr[h, n:n + 1, :] = jnp.sum(blk, axis=0, keepdims=True) / MOBA_BLOCK

    n = lax.broadcasted_iota(jnp.int32, (BF16_ROWS, COLS), 0)
    q_blk = (qi * TQ) // MOBA_BLOCK
    for h in range(N_KV):
        q4 = _stack_heads(q_ref[0, :, h * COLS:(h + 1) * COLS])
        q_hi, q_lo, _ = _split3(q4)
        k_hi, k_lo, _ = _split3(km_scr[h])
        gate = _dot_nt(k_hi, q_hi) + _dot_nt(k_lo, q_hi) + _dot_nt(k_hi, q_lo)
        gate = jnp.where(n < q_blk, gate, NEG_INF)
        gate = jnp.where(n < n_blk, gate, BELOW_ALL)
        picked = _rank_select(gate, n_blk, min(MOBA_TOP_K, n_blk - 1))
        sel_ref[0, h, 0] = jnp.where(n < q_blk, picked, 0.0)


def moba_gate(proj):
    b, s, _ = proj.shape
    return pl.pallas_call(
        _moba_gate_kernel,
        out_shape=jax.ShapeDtypeStruct((b, N_KV, s // TQ, BF16_ROWS, COLS), F32),
        grid=(b, s // TQ),
        in_specs=[pl.BlockSpec((1, TQ, MIX_WIDTH), lambda bi, i: (bi, i, COL_C_Q // MIX_WIDTH)),
                  pl.BlockSpec((1, s, KV_W), lambda bi, i: (bi, 0, COL_C_K // KV_W))],
        out_specs=pl.BlockSpec((1, N_KV, 1, BF16_ROWS, COLS), lambda bi, i: (bi, 0, i, 0, 0)),
        scratch_shapes=[pltpu.VMEM((N_KV, BF16_ROWS, HEAD_DIM), F32)],
        compiler_params=_cparams(("parallel", "arbitrary")),
        name="moba_gate",
    )(proj, proj)


def _flash_kernel(mode, *refs):
    n_in = 5 if mode == "win" else 6
    ins, o_ref, scr = refs[:n_in], refs[n_in], refs[n_in + 1:]
    if mode == "swa":
        c31_ref, sink_ref, band_ref, q_ref, k_ref, v_ref = ins
    elif mode == "win":
        c31_ref, band_ref, q_ref, k_ref, v_ref = ins
    else:
        c31_ref, band_ref, q_ref, k_ref, v_ref, msk_ref = ins
    kb_scr, vt_scr, m_scr, l_scr, acc_scr, alpha_scr, pb_scr = (scr[i * N_KV:(i + 1) * N_KV] for i in range(7))
    pend_ref = scr[7 * N_KV]
    s_scr = scr[7 * N_KV + 1:8 * N_KV + 1]
    mb_scr = scr[8 * N_KV + 1:]
    qi = pl.program_id(1)
    n_kb = k_ref.shape[1] // TK
    head0 = MIXER[mode] * N_HEADS

    @pl.when(qi == 0)
    def _():
        for h in range(N_KV):
            cols = slice(h * HEAD_DIM, (h + 1) * HEAD_DIM)
            kb_scr[h][...] = k_ref[0, :, cols].astype(BF16)
            for j in range(n_kb):
                vt_scr[h][j] = v_ref[0, j * TK:(j + 1) * TK, cols].T.astype(BF16)

    q4 = [(_stack_heads(q_ref[0, :, h * COLS:(h + 1) * COLS]) * SCALE2).astype(BF16) for h in range(N_KV)]
    far_bias = [_head_row(c31_ref, head0 + h * GROUP) for h in range(N_KV)]

    def penalty(keep):
        return (keep - 1.0) * (-NEG_INF)

    def expand(rows, seg):
        return jnp.concatenate([jnp.broadcast_to(r, (seg, COLS)) for r in rows], axis=0)

    def sel_rows(h, first_blk, n_blocks):
        per = TK // SEL_BLOCK
        rows = []
        for part in range(n_blocks * per):
            r = msk_ref[0, h, pl.ds(first_blk * per + part, 1), :]
            rows.append(jnp.concatenate([penalty(r)] * GROUP, axis=1))
        return rows

    def moba_row(h, key_blk):
        return msk_ref[0, h, 0, pl.ds((key_blk * TK) // MOBA_BLOCK, 1), :]

    n_far = jnp.maximum(qi - 1, 0)

    def far_bias_tile(h, k0):
        tail = jnp.where(k0 + 1 >= n_far, NEG_INF, 0.0)
        if mode == "sel":
            rows = [far_bias[h] + r for r in sel_rows(h, k0, 2)]
            per = TK // SEL_BLOCK
            return expand(rows[:per] + [r + tail for r in rows[per:]], SEL_BLOCK)
        row = far_bias[h] + penalty(moba_row(h, k0))
        return expand([row, row + tail], TK)

    def scores(h, k0, n_blocks, bias):
        start = pl.multiple_of(k0 * TK, TK)
        return _dot_nt(kb_scr[h][pl.ds(start, n_blocks * TK), :], q4[h]) + bias

    def flush():
        k0 = pend_ref[0]
        for h in range(N_KV):
            v_t = jnp.concatenate([vt_scr[h][k0], vt_scr[h][k0 + 1]], axis=1)
            pv = jnp.dot(v_t, pb_scr[h][...], preferred_element_type=F32)
            acc_scr[h][...] = alpha_scr[h][...] * acc_scr[h][...] + pv

    def softmax(h, s, m_blk, first):
        n_keys = s.shape[0]
        if first:
            m_new = m_blk
        else:
            m_prev = m_scr[h][...]
            m_new = jnp.maximum(m_prev, m_blk)
            alpha = jnp.exp2(m_prev - m_new)
        p = jnp.exp2(s - m_new)
        p_sum = jnp.sum(p, axis=0, keepdims=True)
        pb_scr[h][0:n_keys, :] = p.astype(BF16)
        if n_keys == TK:
            pb_scr[h][TK:2 * TK, :] = jnp.zeros((TK, COLS), BF16)
        if first:
            l_scr[h][...] = p_sum
            alpha_scr[h][...] = jnp.zeros((1, COLS), F32)
            acc_scr[h][...] = jnp.zeros((HEAD_DIM, COLS), F32)
        else:
            l_scr[h][...] = alpha * l_scr[h][...] + p_sum
            alpha_scr[h][...] = alpha
        m_scr[h][...] = m_new

    def step(k0, n_blocks, bias, first=False):
        if not first:
            flush()
        for h in range(N_KV):
            s = scores(h, k0, n_blocks, bias[h])
            softmax(h, s, jnp.max(s, axis=0, keepdims=True), first)
        pend_ref[0] = k0

    def produce(pair):
        for h in range(N_KV):
            s = scores(h, 2 * pair, 2, far_bias_tile(h, 2 * pair))
            s_scr[h][...] = s
            mb_scr[h][...] = jnp.max(s, axis=0, keepdims=True)

    @pl.when(qi == 0)
    def _():
        bias = [band_ref[1, :, h * COLS:(h + 1) * COLS] for h in range(N_KV)]
        if mode == "sel":
            bias = [bias[h] + expand(sel_rows(h, 0, 1), SEL_BLOCK) for h in range(N_KV)]
        step(0, 1, bias, first=True)

    @pl.when(qi >= 1)
    def _():
        bias = [band_ref[:, :, h * COLS:(h + 1) * COLS].reshape(2 * TK, COLS) for h in range(N_KV)]
        if mode == "sel":
            bias = [bias[h] + expand(sel_rows(h, qi - 1, 2), SEL_BLOCK) for h in range(N_KV)]
        elif mode == "moba":
            own = (qi * TQ) // MOBA_BLOCK == ((qi - 1) * TK) // MOBA_BLOCK
            own_f = jnp.where(own, 1.0, 0.0)
            bias = [bias[h] + expand([penalty(jnp.minimum(moba_row(h, qi - 1) + own_f, 1.0)),
                                      jnp.zeros((1, COLS), F32)], TK) for h in range(N_KV)]
        step(qi - 1, 2, bias, first=True)
        if mode in ("sel", "moba"):
            produce(0)

    masked_rows = jnp.full((TK, COLS), NEG_INF, F32)
    if mode in ("sel", "moba"):
        n_pairs = (n_far + 1) >> 1

        def pair_body(pair, carry):
            flush()
            for h in range(N_KV):
                softmax(h, s_scr[h][...], mb_scr[h][...], False)
            pend_ref[0] = 2 * pair
            produce(jnp.minimum(pair + 1, n_pairs - 1))
            return carry
        lax.fori_loop(0, n_pairs, pair_body, 0)
    elif mode == "win":
        n_full = NSA_WINDOW // TK

        @pl.when(qi >= 3)
        def _():
            step(qi - 3, 2, far_bias)

        @pl.when(qi == 2)
        def _():
            step(0, 2, [jnp.concatenate([jnp.broadcast_to(far_bias[h], (TK, COLS)), masked_rows], axis=0)
                        for h in range(N_KV)])

        @pl.when(qi >= n_full)
        def _():
            key = lax.broadcasted_iota(jnp.int32, (TK, COLS), 0)
            qry = lax.broadcasted_iota(jnp.int32, (TK, COLS), 1) & (TQ - 1)
            step(qi - n_full, 2, [jnp.concatenate([jnp.where(key > qry, far_bias[h], NEG_INF), masked_rows], axis=0)
                                  for h in range(N_KV)])

    flush()
    for h in range(N_KV):
        m = m_scr[h][...]
        l = l_scr[h][...]
        acc = acc_scr[h][...]
        if mode == "swa":
            sink = _head_row(sink_ref, h * GROUP)
            m_fin = jnp.maximum(m, sink)
            shrink = jnp.exp2(m - m_fin)
            l = l * shrink + jnp.exp2(sink - m_fin)
            acc = acc * shrink
        out = acc / jnp.maximum(l, TINY)
        o_ref[0, :, h * COLS:(h + 1) * COLS] = _untranspose_heads(out).astype(o_ref.dtype)


def flash_attention(mode, proj, band, c31, *, q_col, k_col, v_col, out_dtype, mask=None, sinks=None):
    b, s, _ = proj.shape
    smem = pl.BlockSpec(memory_space=pltpu.SMEM)
    in_specs = [smem]
    args = [c31]
    if mode == "swa":
        in_specs.append(smem)
        args.append(sinks)
    in_specs += [pl.BlockSpec((2, TK, N_KV * COLS), lambda bi, i: (0, 0, MIXER[mode])),
                 pl.BlockSpec((1, TQ, MIX_WIDTH), lambda bi, i: (bi, i, q_col // MIX_WIDTH)),
                 pl.BlockSpec((1, s, KV_W), lambda bi, i: (bi, 0, k_col // KV_W)),
                 pl.BlockSpec((1, s, KV_W), lambda bi, i: (bi, 0, v_col // KV_W))]
    args += [band, proj, proj, proj]
    if mode == "sel":
        in_specs.append(pl.BlockSpec((1, N_KV, mask.shape[2], TQ), lambda bi, i: (bi, 0, 0, i)))
        args.append(mask)
    elif mode == "moba":
        in_specs.append(pl.BlockSpec((1, N_KV, 1, mask.shape[3], COLS), lambda bi, i: (bi, 0, i, 0, 0)))
        args.append(mask)
    per_head = [pltpu.VMEM((s, HEAD_DIM), BF16), pltpu.VMEM((s // TK, HEAD_DIM, TK), BF16),
                pltpu.VMEM((1, COLS), F32), pltpu.VMEM((1, COLS), F32), pltpu.VMEM((HEAD_DIM, COLS), F32),
                pltpu.VMEM((1, COLS), F32), pltpu.VMEM((2 * TK, COLS), BF16)]
    scratch = [shape for shape in per_head for _ in range(N_KV)] + [pltpu.SMEM((1,), jnp.int32)]
    if mode in ("sel", "moba"):
        scratch += [pltpu.VMEM((2 * TK, COLS), F32)] * N_KV + [pltpu.VMEM((1, COLS), F32)] * N_KV
    return pl.pallas_call(
        functools.partial(_flash_kernel, mode),
        out_shape=jax.ShapeDtypeStruct((b, s, MIX_WIDTH), out_dtype),
        grid=(b, s // TQ),
        in_specs=in_specs,
        out_specs=pl.BlockSpec((1, TQ, MIX_WIDTH), lambda bi, i: (bi, i, 0)),
        scratch_shapes=scratch,
        compiler_params=_cparams(("parallel", "arbitrary")),
        name="flash_" + mode,
    )(*args)


def _nsa_combine_kernel(cmp_ref, sel_ref, win_ref, gate_ref, o_ref):
    gate = jax.nn.sigmoid(gate_ref[...])
    for h in range(N_HEADS):
        cols = slice(h * HEAD_DIM, (h + 1) * HEAD_DIM)
        out = (gate[:, 3 * h:3 * h + 1] * cmp_ref[:, cols]
               + gate[:, 3 * h + 1:3 * h + 2] * sel_ref[:, cols]
               + gate[:, 3 * h + 2:3 * h + 3] * win_ref[:, cols])
        o_ref[:, cols] = out.astype(BF16)


def nsa_combine(o_cmp, o_sel, o_win, proj, *, tm=512):
    t = o_cmp.shape[0]
    o_spec = pl.BlockSpec((tm, MIX_WIDTH), lambda i: (i, 0))
    return pl.pallas_call(
        _nsa_combine_kernel,
        out_shape=jax.ShapeDtypeStruct((t, MIX_WIDTH), BF16),
        grid=(t // tm,),
        in_specs=[o_spec, o_spec, o_spec,
                  pl.BlockSpec((tm, LANES), lambda i: (i, COL_A_GATE // LANES))],
        out_specs=o_spec,
        compiler_params=_cparams(("parallel",)),
        name="nsa_combine",
    )(o_cmp, o_sel, o_win, proj)


def _selection_share(n_sel, n_half):
    blk = np.arange(n_sel)[:, None] * SEL_BLOCK
    starts = np.arange(n_half)[None, :] * CMP_STRIDE
    shared = np.clip(np.minimum(starts + CMP_LEN, blk + SEL_BLOCK) - np.maximum(starts, blk), 0, None)
    shared = shared / CMP_STRIDE
    shared[:, n_half - 1] = 0.0
    return shared.astype(np.float32)


REGROUP_TILE = 512


def _regroup_tables():
    q_w, kv_w = MIX_WIDTH, KV_W
    a_kv = q_w
    a_gate = a_kv + 6 * kv_w
    b_q = a_gate + ORIG_GATE_W
    b_kv = b_q + q_w
    c_q = b_kv + 2 * kv_w
    c_kv = c_q + q_w
    merge = c_kv + 2 * kv_w
    segments = [(COL_A_Q, q_w, 0), (COL_B_Q, q_w, b_q), (COL_C_Q, q_w, c_q), (COL_A_KC, 6 * kv_w, a_kv),
                (COL_B_K, 2 * kv_w, b_kv), (COL_C_K, 2 * kv_w, c_kv), (COL_A_GATE, COL_MERGE - COL_A_GATE, a_gate),
                (COL_MERGE, 3 * D_MODEL, merge)]
    start, is_gate = [], []
    for new0, width, orig0 in segments:
        for off in range(0, width, REGROUP_TILE):
            start.append(orig0 + off)
            is_gate.append(int(new0 == COL_A_GATE))
    return np.asarray(start, np.int32), np.asarray(is_gate, np.int32)


def _regroup_kernel(start_tbl, gate_tbl, wt_ref, o_ref):
    rows = wt_ref[...]
    row = lax.broadcasted_iota(jnp.int32, rows.shape, 0)
    keep = jnp.where(gate_tbl[pl.program_id(1)] == 1, ORIG_GATE_W, REGROUP_TILE)
    o_ref[0] = jnp.where(row < keep, rows, 0.0).T.astype(BF16)


def regroup_w_in(w):
    depth, d, n_in = w.shape
    start, is_gate = (jnp.asarray(t) for t in _regroup_tables())
    return pl.pallas_call(
        _regroup_kernel,
        out_shape=jax.ShapeDtypeStruct((depth, d, N_PROJ), BF16),
        grid_spec=pltpu.PrefetchScalarGridSpec(
            num_scalar_prefetch=2,
            grid=(depth, N_PROJ // REGROUP_TILE),
            in_specs=[pl.BlockSpec((None, pl.Element(REGROUP_TILE), pl.Element(d)),
                                   lambda l, j, st, gt: (l, pl.multiple_of(st[j], ORIG_GATE_W), 0))],
            out_specs=pl.BlockSpec((1, d, REGROUP_TILE), lambda l, j, st, gt: (l, 0, j))),
        compiler_params=_cparams(("parallel", "parallel")),
        name="regroup_w_in",
    )(start, is_gate, jnp.swapaxes(w, 1, 2))


def kernel(x, w_in, cmp_pos, cmp_w1, cmp_w2, swa_sinks, w_branch, w_out, w_mlp_in, w_mlp_out,
           norm_mix, norm_mlp, norm_final, rel_bias):
    b, s, d = x.shape
    depth = w_in.shape[0]
    t = b * s
    n_half = s // CMP_STRIDE
    n_sel = s // SEL_BLOCK

    tbl_flat = rel_bias.reshape(-1)
    c31 = rel_bias[N_BUCKETS - 1]
    band = band_bias(tbl_flat)
    bias_c = cmp_bias(tbl_flat, s)
    share = jnp.asarray(_selection_share(n_sel, n_half), BF16)

    w_in_cols = regroup_w_in(w_in)
    xt = x.reshape(t, d)
    for layer in range(depth):
        proj2d = norm_matmul(xt, norm_mix[layer], w_in_cols, layer)
        proj = proj2d.reshape(b, s, N_PROJ)

        cmp_tokens = nsa_compress(proj, cmp_pos[layer].reshape(2, 2, CMP_STRIDE * HEAD_DIM), cmp_w1, cmp_w2, layer)
        o_cmp, sel_mask = nsa_compressed(bias_c, proj, cmp_tokens, share)
        o_sel = flash_attention("sel", proj, band, c31, q_col=COL_A_Q, k_col=COL_A_KS, v_col=COL_A_VS,
                                out_dtype=F32, mask=sel_mask)
        o_win = flash_attention("win", proj, band, c31, q_col=COL_A_Q, k_col=COL_A_KW, v_col=COL_A_VW,
                                out_dtype=F32)
        o_a = nsa_combine(o_cmp.reshape(t, MIX_WIDTH), o_sel.reshape(t, MIX_WIDTH),
                          o_win.reshape(t, MIX_WIDTH), proj2d)

        o_b = flash_attention("swa", proj, band, c31, q_col=COL_B_Q, k_col=COL_B_K, v_col=COL_B_V,
                              out_dtype=BF16, sinks=swa_sinks[layer])
        moba_mask = moba_gate(proj)
        o_c = flash_attention("moba", proj, band, c31, q_col=COL_C_Q, k_col=COL_C_K, v_col=COL_C_V,
                              out_dtype=BF16, mask=moba_mask)

        z = merge_branches(o_a, o_b.reshape(t, MIX_WIDTH), o_c.reshape(t, MIX_WIDTH), w_branch, layer, proj2d)
        xt = matmul_residual(z, w_out, layer, xt)
        xt = mlp_block(xt, norm_mlp[layer], w_mlp_in, w_mlp_out, layer)

    return final_norm(xt, norm_final).reshape(b, s, d)
```

```python
import functools
import math

import numpy as np
import jax
import jax.numpy as jnp
from jax import lax
from jax.experimental import pallas as pl
from jax.experimental.pallas import tpu as pltpu

F32 = jnp.float32
BF16 = jnp.bfloat16

D_MODEL = 2048
HEAD_DIM = 128
N_HEADS = 8
N_KV = 2
GROUP = N_HEADS // N_KV
MIX_WIDTH = N_HEADS * HEAD_DIM
CMP_LEN = 32
CMP_STRIDE = 16
CMP_HIDDEN = 2 * HEAD_DIM
SEL_BLOCK = 64
SEL_TOP_N = 16
NSA_WINDOW = 512
SWA_WINDOW = 128
MOBA_BLOCK = 256
MOBA_TOP_K = 3
N_BUCKETS = 32
BUCKET_EXACT = N_BUCKETS // 2
BUCKET_MAX_DIST = 128
TOTAL_HEADS = 3 * N_HEADS
D_FF = 4 * D_MODEL
RMS_EPS = 1e-6
NEG_INF = -1e30
FORCE_SCORE = 1e30
TINY = 1e-30
BELOW_ALL = -3e38
SCALE = HEAD_DIM ** -0.5

LANES = 128
BF16_ROWS = 16
VMEM_LIMIT = 56 * 1024 * 1024
TQ = 128
TK = 128
COLS = GROUP * TQ
KV_W = N_KV * HEAD_DIM
LOG2E = math.log2(math.e)
SCALE2 = SCALE * LOG2E

COL_A_Q = 0
COL_B_Q = 1024
COL_C_Q = 2048
COL_A_KC = 3072
COL_A_KS = 3584
COL_A_VS = 3840
COL_A_KW = 4096
COL_A_VW = 4352
COL_B_K = 4608
COL_B_V = 4864
COL_C_K = 5120
COL_C_V = 5376
COL_A_GATE = 5632
COL_MERGE = 6144
N_PROJ = COL_MERGE + 3 * D_MODEL
ORIG_GATE_W = 3 * N_HEADS

MIXER = {"sel": 0, "win": 0, "swa": 1, "moba": 2}


def _cparams(semantics):
    return pltpu.CompilerParams(dimension_semantics=semantics, vmem_limit_bytes=VMEM_LIMIT)


def _rms(x, gain):
    y = x * lax.rsqrt(jnp.mean(x * x, axis=-1, keepdims=True) + RMS_EPS)
    return y * gain


def _norm_matmul_kernel(x_ref, g_ref, w_ref, o_ref, h_scr):
    @pl.when(pl.program_id(1) == 0)
    def _():
        h_scr[...] = _rms(x_ref[...], g_ref[...]).astype(BF16)

    o_ref[...] = jnp.dot(h_scr[...], w_ref[0], preferred_element_type=F32)


def norm_matmul(x, gain, w, layer, *, tm=1024, tn=1024):
    t, k = x.shape
    n = w.shape[2]
    return pl.pallas_call(
        _norm_matmul_kernel,
        out_shape=jax.ShapeDtypeStruct((t, n), F32),
        grid=(t // tm, n // tn),
        in_specs=[pl.BlockSpec((tm, k), lambda i, j: (i, 0)),
                  pl.BlockSpec((1, k), lambda i, j: (0, 0)),
                  pl.BlockSpec((1, k, tn), lambda i, j: (layer, 0, j))],
        out_specs=pl.BlockSpec((tm, tn), lambda i, j: (i, j)),
        scratch_shapes=[pltpu.VMEM((tm, k), BF16)],
        compiler_params=_cparams(("parallel", "arbitrary")),
        name="norm_matmul",
    )(x, gain.reshape(1, k), w)


def _merge_kernel(oa_ref, ob_ref, oc_ref, wb_ref, g0_ref, g1_ref, g2_ref, z_ref, w_scr):
    @pl.when(pl.program_id(1) == 0)
    def _():
        w_scr[...] = wb_ref[0].astype(BF16)

    acc = jax.nn.sigmoid(g0_ref[...]) * jnp.dot(oa_ref[...], w_scr[0], preferred_element_type=F32)
    acc += jax.nn.sigmoid(g1_ref[...]) * jnp.dot(ob_ref[...], w_scr[1], preferred_element_type=F32)
    acc += jax.nn.sigmoid(g2_ref[...]) * jnp.dot(oc_ref[...], w_scr[2], preferred_element_type=F32)
    z_ref[...] = acc.astype(BF16)


def merge_branches(o_a, o_b, o_c, w_branch, layer, proj, *, tm=512, tn=1024):
    t = o_a.shape[0]
    gate_blk = COL_MERGE // tn
    per_branch = D_MODEL // tn
    o_spec = pl.BlockSpec((tm, MIX_WIDTH), lambda j, i: (i, 0))

    def gate_spec(m):
        return pl.BlockSpec((tm, tn), lambda j, i: (i, gate_blk + m * per_branch + j))

    return pl.pallas_call(
        _merge_kernel,
        out_shape=jax.ShapeDtypeStruct((t, D_MODEL), BF16),
        grid=(D_MODEL // tn, t // tm),
        in_specs=[o_spec, o_spec, o_spec,
                  pl.BlockSpec((1, 3, MIX_WIDTH, tn), lambda j, i: (layer, 0, 0, j)),
                  gate_spec(0), gate_spec(1), gate_spec(2)],
        out_specs=pl.BlockSpec((tm, tn), lambda j, i: (i, j)),
        scratch_shapes=[pltpu.VMEM((3, MIX_WIDTH, tn), BF16)],
        compiler_params=_cparams(("parallel", "arbitrary")),
        name="merge_branches",
    )(o_a, o_b, o_c, w_branch, proj, proj, proj)


def _matmul_res_kernel(a_ref, w_ref, x_ref, o_ref, w_scr):
    @pl.when(pl.program_id(1) == 0)
    def _():
        w_scr[...] = w_ref[0].astype(BF16)

    o_ref[...] = x_ref[...] + jnp.dot(a_ref[...], w_scr[...], preferred_element_type=F32)


def matmul_residual(a, w, layer, x, *, tm=1024, tn=1024):
    t, k = a.shape
    n = w.shape[2]
    return pl.pallas_call(
        _matmul_res_kernel,
        out_shape=jax.ShapeDtypeStruct((t, n), F32),
        grid=(n // tn, t // tm),
        in_specs=[pl.BlockSpec((tm, k), lambda j, i: (i, 0)),
                  pl.BlockSpec((1, k, tn), lambda j, i: (layer, 0, j)),
                  pl.BlockSpec((tm, tn), lambda j, i: (i, j))],
        out_specs=pl.BlockSpec((tm, tn), lambda j, i: (i, j)),
        scratch_shapes=[pltpu.VMEM((k, tn), BF16)],
        compiler_params=_cparams(("parallel", "arbitrary")),
        name="matmul_residual",
    )(a, w, x)


def _mlp_kernel(x_ref, g_ref, w1_ref, w2_ref, o_ref, h_scr):
    @pl.when(pl.program_id(1) == 0)
    def _():
        x = x_ref[...]
        h_scr[...] = _rms(x, g_ref[...]).astype(BF16)
        o_ref[...] = x

    u = jnp.dot(h_scr[...], w1_ref[0].astype(BF16), preferred_element_type=F32)
    u = jnp.square(jnp.maximum(u, 0.0)).astype(BF16)
    o_ref[...] += jnp.dot(u, w2_ref[0].astype(BF16), preferred_element_type=F32)


def mlp_block(x, gain, w1, w2, layer, *, tm=1024, tf=512):
    t, d = x.shape
    dff = w1.shape[2]
    return pl.pallas_call(
        _mlp_kernel,
        out_shape=jax.ShapeDtypeStruct((t, d), F32),
        grid=(t // tm, dff // tf),
        in_specs=[pl.BlockSpec((tm, d), lambda i, f: (i, 0)),
                  pl.BlockSpec((1, d), lambda i, f: (0, 0)),
                  pl.BlockSpec((1, d, tf), lambda i, f: (layer, 0, f)),
                  pl.BlockSpec((1, tf, d), lambda i, f: (layer, f, 0))],
        out_specs=pl.BlockSpec((tm, d), lambda i, f: (i, 0)),
        scratch_shapes=[pltpu.VMEM((tm, d), BF16)],
        compiler_params=_cparams(("parallel", "arbitrary")),
        name="mlp_block",
    )(x, gain.reshape(1, d), w1, w2)


def _final_norm_kernel(x_ref, g_ref, o_ref):
    o_ref[...] = _rms(x_ref[...], g_ref[...])


def final_norm(x, gain, *, tm=512):
    t, d = x.shape
    return pl.pallas_call(
        _final_norm_kernel,
        out_shape=jax.ShapeDtypeStruct((t, d), F32),
        grid=(t // tm,),
        in_specs=[pl.BlockSpec((tm, d), lambda i: (i, 0)),
                  pl.BlockSpec((1, d), lambda i: (0, 0))],
        out_specs=pl.BlockSpec((tm, d), lambda i: (i, 0)),
        compiler_params=_cparams(("parallel",)),
        name="final_norm",
    )(x, gain.reshape(1, d))


def _t5_bucket(n):
    log_ratio = jnp.log(jnp.maximum(n, 1).astype(F32) / BUCKET_EXACT) / math.log(BUCKET_MAX_DIST / BUCKET_EXACT)
    large = jnp.minimum(BUCKET_EXACT + (log_ratio * (N_BUCKETS - BUCKET_EXACT)).astype(jnp.int32), N_BUCKETS - 1)
    return jnp.where(n < BUCKET_EXACT, n, large)


def _lookup_bias(tbl_ref, head, n):
    bucket = _t5_bucket(n)
    out = jnp.zeros(n.shape, F32)
    for b in range(N_BUCKETS):
        out = jnp.where(bucket == b, tbl_ref[b * TOTAL_HEADS + head], out)
    return out


def _band_bias_kernel(tbl_ref, o_ref):
    l = lax.broadcasted_iota(jnp.int32, (TK, TQ), 0)
    q = lax.broadcasted_iota(jnp.int32, (TK, TQ), 1)
    head = pl.program_id(0)
    prev = _lookup_bias(tbl_ref, head, jnp.maximum(q + TK - l, 0)) * LOG2E
    diag = _lookup_bias(tbl_ref, head, jnp.maximum(q - l, 0)) * LOG2E
    sliding = (head >= MIXER["swa"] * N_HEADS) & (head < (MIXER["swa"] + 1) * N_HEADS)
    o_ref[0] = jnp.where(sliding & (q + TK - l >= SWA_WINDOW), NEG_INF, prev)
    o_ref[1] = jnp.where(l > q, NEG_INF, diag)


def band_bias(tbl_flat):
    return pl.pallas_call(
        _band_bias_kernel,
        out_shape=jax.ShapeDtypeStruct((2, TK, TOTAL_HEADS * TQ), F32),
        grid=(TOTAL_HEADS,),
        in_specs=[pl.BlockSpec(memory_space=pltpu.SMEM)],
        out_specs=pl.BlockSpec((2, TK, TQ), lambda h: (0, 0, h)),
        compiler_params=_cparams(("parallel",)),
        name="band_bias",
    )(tbl_flat)


def _cmp_bias_kernel(tbl_ref, o_ref):
    c = lax.broadcasted_iota(jnp.int32, (LANES, TQ), 0)
    t = pl.program_id(0) * TQ + lax.broadcasted_iota(jnp.int32, (LANES, TQ), 1)
    o_ref[0] = _lookup_bias(tbl_ref, pl.program_id(1), jnp.maximum(t - (c * CMP_STRIDE + CMP_LEN - 1), 0))


def cmp_bias(tbl_flat, s):
    return pl.pallas_call(
        _cmp_bias_kernel,
        out_shape=jax.ShapeDtypeStruct((s // TQ, LANES, N_HEADS * TQ), F32),
        grid=(s // TQ, N_HEADS),
        in_specs=[pl.BlockSpec(memory_space=pltpu.SMEM)],
        out_specs=pl.BlockSpec((1, LANES, TQ), lambda i, h: (i, 0, h)),
        compiler_params=_cparams(("parallel", "parallel")),
        name="cmp_bias",
    )(tbl_flat)


def _compress_kernel(x_ref, pos_ref, w1_ref, w2_ref, o_ref):
    n_half = x_ref.shape[1] // CMP_STRIDE
    half = jnp.concatenate([x_ref[0, pl.ds(l, n_half, stride=CMP_STRIDE), :] for l in range(CMP_STRIDE)], axis=1)
    half_w = CMP_STRIDE * HEAD_DIM
    xa = (half + pos_ref[0, 0:1, :]).astype(BF16)
    xb = (half + pos_ref[0, 1:2, :]).astype(BF16)
    ha = jnp.dot(xa, w1_ref[0, 0, 0:half_w, :].astype(BF16), preferred_element_type=F32)
    hb = jnp.dot(xb, w1_ref[0, 0, half_w:2 * half_w, :].astype(BF16), preferred_element_type=F32)
    hidden = ha + pltpu.roll(hb, n_half - 1, 0)
    act = jax.nn.gelu(hidden, approximate=True).astype(BF16)
    o_ref[0, 0, 0] = jnp.dot(act, w2_ref[0, 0].astype(BF16), preferred_element_type=F32)


def nsa_compress(proj, pos, w1, w2, layer):
    b, s, _ = proj.shape
    n_kv, n_half, width = N_KV, s // CMP_STRIDE, CMP_STRIDE * HEAD_DIM
    col_blk = COL_A_KC // HEAD_DIM
    return pl.pallas_call(
        _compress_kernel,
        out_shape=jax.ShapeDtypeStruct((b, 2, n_kv, n_half, HEAD_DIM), F32),
        grid=(2, b, n_kv),
        in_specs=[pl.BlockSpec((1, s, HEAD_DIM), lambda kv, bi, h: (bi, 0, col_blk + kv * n_kv + h)),
                  pl.BlockSpec((1, 2, width), lambda kv, bi, h: (kv, 0, 0)),
                  pl.BlockSpec((1, 1, 2 * width, CMP_HIDDEN), lambda kv, bi, h: (layer, kv, 0, 0)),
                  pl.BlockSpec((1, 1, CMP_HIDDEN, HEAD_DIM), lambda kv, bi, h: (layer, kv, 0, 0))],
        out_specs=pl.BlockSpec((1, 1, 1, n_half, HEAD_DIM), lambda kv, bi, h: (bi, kv, h, 0, 0)),
        compiler_params=_cparams(("parallel", "parallel", "parallel")),
        name="nsa_compress",
    )(proj, pos, w1, w2)


def _stack_heads(q):
    return jnp.concatenate([q[:, g * HEAD_DIM:(g + 1) * HEAD_DIM] for g in range(GROUP)], axis=0)


def _untranspose_heads(o_t):
    return jnp.concatenate([o_t[:, g * TQ:(g + 1) * TQ].T for g in range(GROUP)], axis=1)


def _head_row(ref, first):
    return jnp.concatenate([jnp.full((1, TQ), ref[first + g] * LOG2E, F32) for g in range(GROUP)], axis=1)


def _split3(x):
    x1 = x.astype(BF16)
    r1 = x - x1.astype(F32)
    x2 = r1.astype(BF16)
    x3 = (r1 - x2.astype(F32)).astype(BF16)
    return x1, x2, x3


def _dot_nt(a, b):
    return lax.dot_general(a, b, (((1,), (1,)), ((), ())), preferred_element_type=F32)


def _rank_select(score, n_cand, n_top):
    idx = lax.broadcasted_iota(jnp.int32, score.shape, 0)
    rank = jnp.zeros(score.shape, jnp.int32)
    for j in range(n_cand):
        other = score[j:j + 1, :]
        ahead = (other > score) | ((other == score) & (j < idx))
        rank += jnp.where(ahead, 1, 0)
    return jnp.where((rank < n_top) & (idx < n_cand), 1.0, 0.0)


def _nsa_cmp_kernel(bias_ref, q_ref, kc_ref, vc_ref, share_ref, o_ref, sel_ref):
    qi = pl.program_id(1)
    share = share_ref[...]
    n_sel = share.shape[0]
    c = lax.broadcasted_iota(jnp.int32, (LANES, COLS), 0)
    t = qi * TQ + (lax.broadcasted_iota(jnp.int32, (LANES, COLS), 1) & (TQ - 1))
    valid = t - (c * CMP_STRIDE + CMP_LEN - 1) >= 0
    blk = lax.broadcasted_iota(jnp.int32, (n_sel, TQ), 0)
    cur = (qi * TQ + lax.broadcasted_iota(jnp.int32, (n_sel, TQ), 1)) // SEL_BLOCK
    forced = (blk == 0) | (blk == cur) | (blk == cur - 1)
    for h in range(N_KV):
        cols = slice(h * COLS, (h + 1) * COLS)
        q4 = _stack_heads(q_ref[0, :, cols]).astype(BF16)
        kc = kc_ref[0, 0, h].astype(BF16)
        vc_t = vc_ref[0, 0, h].T.astype(BF16)
        logits = _dot_nt(kc, q4) * SCALE + bias_ref[0, :, cols]
        logits = jnp.where(valid, logits, NEG_INF)
        m = jnp.max(logits, axis=0, keepdims=True)
        p = jnp.where(valid, jnp.exp(logits - m), 0.0)
        p = p / jnp.maximum(jnp.sum(p, axis=0, keepdims=True), TINY)
        o_ref[0, :, cols] = _untranspose_heads(jnp.dot(vc_t, p.astype(BF16), preferred_element_type=F32))

        p_sum = p[:, 0:TQ]
        for g in range(1, GROUP):
            p_sum = p_sum + p[:, g * TQ:(g + 1) * TQ]
        importance = sum(jnp.dot(share, part, preferred_element_type=F32) for part in _split3(p_sum))
        score = jnp.where(forced, FORCE_SCORE, jnp.where(blk <= cur, importance, NEG_INF))
        sel_ref[0, h] = _rank_select(score, n_sel, min(SEL_TOP_N, n_sel))


def nsa_compressed(bias_c, proj, cmp_tokens, share):
    b, s, _ = proj.shape
    n_cmp = cmp_tokens.shape[3]
    n_sel = share.shape[0]
    return pl.pallas_call(
        _nsa_cmp_kernel,
        out_shape=(jax.ShapeDtypeStruct((b, s, MIX_WIDTH), F32),
                   jax.ShapeDtypeStruct((b, N_KV, n_sel, s), F32)),
        grid=(b, s // TQ),
        in_specs=[pl.BlockSpec((1, n_cmp, N_KV * COLS), lambda bi, i: (i, 0, 0)),
                  pl.BlockSpec((1, TQ, MIX_WIDTH), lambda bi, i: (bi, i, COL_A_Q // MIX_WIDTH)),
                  pl.BlockSpec((1, 1, N_KV, n_cmp, HEAD_DIM), lambda bi, i: (bi, 0, 0, 0, 0)),
                  pl.BlockSpec((1, 1, N_KV, n_cmp, HEAD_DIM), lambda bi, i: (bi, 1, 0, 0, 0)),
                  pl.BlockSpec((n_sel, n_cmp), lambda bi, i: (0, 0))],
        out_specs=(pl.BlockSpec((1, TQ, MIX_WIDTH), lambda bi, i: (bi, i, 0)),
                   pl.BlockSpec((1, N_KV, n_sel, TQ), lambda bi, i: (bi, 0, 0, i))),
        compiler_params=_cparams(("parallel", "parallel")),
        name="nsa_compressed",
    )(bias_c, proj, cmp_tokens, cmp_tokens, share)


def _moba_gate_kernel(q_ref, k_ref, sel_ref):
    s = q_ref.shape[1]
    n_blk = s // MOBA_BLOCK
    n = lax.broadcasted_iota(jnp.int32, (BF16_ROWS, GROUP * s), 0)
    q_blk = (lax.broadcasted_iota(jnp.int32, (BF16_ROWS, GROUP * s), 1) & (s - 1)) // MOBA_BLOCK
    for h in range(N_KV):
        k = k_ref[0, :, h * HEAD_DIM:(h + 1) * HEAD_DIM]
        k_mean = [jnp.sum(k[j * MOBA_BLOCK:(j + 1) * MOBA_BLOCK], axis=0, keepdims=True) / MOBA_BLOCK
                  for j in range(n_blk)]
        k_mean = jnp.concatenate(k_mean + [jnp.zeros((BF16_ROWS - n_blk, HEAD_DIM), F32)], axis=0)
        q4 = jnp.concatenate([q_ref[0, :, (h * GROUP + g) * HEAD_DIM:(h * GROUP + g + 1) * HEAD_DIM]
                              for g in range(GROUP)], axis=0)
        q_hi, q_lo, _ = _split3(q4)
        k_hi, k_lo, _ = _split3(k_mean)
        gate = _dot_nt(k_hi, q_hi) + _dot_nt(k_lo, q_hi) + _dot_nt(k_hi, q_lo)
        gate = jnp.where(n < q_blk, gate, NEG_INF)
        gate = jnp.where(n < n_blk, gate, BELOW_ALL)
        picked = _rank_select(gate, n_blk, min(MOBA_TOP_K, n_blk - 1))
        picked = jnp.where(n < q_blk, picked, 0.0)
        for qi in range(s // TQ):
            for g in range(GROUP):
                sel_ref[0, h, qi, :, g * TQ:(g + 1) * TQ] = picked[:, g * s + qi * TQ:g * s + (qi + 1) * TQ]


def moba_gate(proj):
    b, s, _ = proj.shape
    assert s & (s - 1) == 0
    return pl.pallas_call(
        _moba_gate_kernel,
        out_shape=jax.ShapeDtypeStruct((b, N_KV, s // TQ, BF16_ROWS, COLS), F32),
        grid=(b,),
        in_specs=[pl.BlockSpec((1, s, MIX_WIDTH), lambda bi: (bi, 0, COL_C_Q // MIX_WIDTH)),
                  pl.BlockSpec((1, s, KV_W), lambda bi: (bi, 0, COL_C_K // KV_W))],
        out_specs=pl.BlockSpec((1, N_KV, s // TQ, BF16_ROWS, COLS), lambda bi: (bi, 0, 0, 0, 0)),
        compiler_params=_cparams(("parallel",)),
        name="moba_gate",
    )(proj, proj)


def _flash_kernel(mode, *refs):
    n_in = 5 if mode == "win" else 6
    ins, o_ref, scr = refs[:n_in], refs[n_in], refs[n_in + 1:]
    if mode == "swa":
        c31_ref, sink_ref, band_ref, q_ref, k_ref, v_ref = ins
    elif mode == "win":
        c31_ref, band_ref, q_ref, k_ref, v_ref = ins
    else:
        c31_ref, band_ref, q_ref, k_ref, v_ref, msk_ref = ins
    kb_scr, vt_scr, m_scr, l_scr, acc_scr, alpha_scr, pb_scr = (scr[i * N_KV:(i + 1) * N_KV] for i in range(7))
    pend_ref = scr[7 * N_KV]
    s_scr = scr[7 * N_KV + 1:8 * N_KV + 1]
    mb_scr = scr[8 * N_KV + 1:]
    qi = pl.program_id(1)
    n_kb = k_ref.shape[1] // TK
    head0 = MIXER[mode] * N_HEADS

    @pl.when(qi == 0)
    def _():
        for h in range(N_KV):
            cols = slice(h * HEAD_DIM, (h + 1) * HEAD_DIM)
            kb_scr[h][...] = k_ref[0, :, cols].astype(BF16)
            for j in range(n_kb):
                vt_scr[h][j] = v_ref[0, j * TK:(j + 1) * TK, cols].T.astype(BF16)

    q4 = [(_stack_heads(q_ref[0, :, h * COLS:(h + 1) * COLS]) * SCALE2).astype(BF16) for h in range(N_KV)]
    far_bias = [_head_row(c31_ref, head0 + h * GROUP) for h in range(N_KV)]

    def penalty(keep):
        return (keep - 1.0) * (-NEG_INF)

    def expand(rows, seg):
        return jnp.concatenate([jnp.broadcast_to(r, (seg, COLS)) for r in rows], axis=0)

    def sel_rows(h, first_blk, n_blocks):
        per = TK // SEL_BLOCK
        rows = []
        for part in range(n_blocks * per):
            r = msk_ref[0, h, pl.ds(first_blk * per + part, 1), :]
            rows.append(jnp.concatenate([penalty(r)] * GROUP, axis=1))
        return rows

    def moba_row(h, key_blk):
        return msk_ref[0, h, 0, pl.ds((key_blk * TK) // MOBA_BLOCK, 1), :]

    far_lo = jnp.maximum(qi - NSA_WINDOW // TK, 0) if mode == "win" else 0
    n_far = jnp.maximum(qi - 1 - far_lo, 0)

    def far_bias_tile(h, k0):
        tail = jnp.where(k0 + 1 >= qi - 1, NEG_INF, 0.0)
        if mode == "sel":
            rows = [far_bias[h] + r for r in sel_rows(h, k0, 2)]
            per = TK // SEL_BLOCK
            return expand(rows[:per] + [r + tail for r in rows[per:]], SEL_BLOCK)
        if mode == "moba":
            row = far_bias[h] + penalty(moba_row(h, k0))
            return expand([row, row + tail], TK)
        key = lax.broadcasted_iota(jnp.int32, (TK, COLS), 0)
        qry = lax.broadcasted_iota(jnp.int32, (TK, COLS), 1) & (TQ - 1)
        edge = jnp.where(k0 == qi - NSA_WINDOW // TK, NEG_INF, 0.0)
        first = far_bias[h] + jnp.where(key <= qry, edge, 0.0)
        return jnp.concatenate([first, jnp.broadcast_to(far_bias[h] + tail, (TK, COLS))], axis=0)

    def scores(h, k0, n_blocks, bias):
        start = pl.multiple_of(k0 * TK, TK)
        return _dot_nt(kb_scr[h][pl.ds(start, n_blocks * TK), :], q4[h]) + bias

    def flush():
        k0 = pend_ref[0]
        for h in range(N_KV):
            v_t = jnp.concatenate([vt_scr[h][k0], vt_scr[h][k0 + 1]], axis=1)
            pv = jnp.dot(v_t, pb_scr[h][...], preferred_element_type=F32)
            acc_scr[h][...] = alpha_scr[h][...] * acc_scr[h][...] + pv

    def softmax(h, s, m_blk, first):
        n_keys = s.shape[0]
        if first:
            m_new = m_blk
        else:
            m_prev = m_scr[h][...]
            m_new = jnp.maximum(m_prev, m_blk)
            alpha = jnp.exp2(m_prev - m_new)
        p = jnp.exp2(s - m_new)
        p_sum = jnp.sum(p, axis=0, keepdims=True)
        pb_scr[h][0:n_keys, :] = p.astype(BF16)
        if n_keys == TK:
            pb_scr[h][TK:2 * TK, :] = jnp.zeros((TK, COLS), BF16)
        if first:
            l_scr[h][...] = p_sum
            alpha_scr[h][...] = jnp.zeros((1, COLS), F32)
            acc_scr[h][...] = jnp.zeros((HEAD_DIM, COLS), F32)
        else:
            l_scr[h][...] = alpha * l_scr[h][...] + p_sum
            alpha_scr[h][...] = alpha
        m_scr[h][...] = m_new

    def first_step(k0, n_blocks, bias):
        for h in range(N_KV):
            s = scores(h, k0, n_blocks, bias[h])
            softmax(h, s, jnp.max(s, axis=0, keepdims=True), True)
        pend_ref[0] = k0

    def produce(pair):
        k0 = far_lo + 2 * pair
        for h in range(N_KV):
            s = scores(h, k0, 2, far_bias_tile(h, k0))
            s_scr[h][...] = s
            mb_scr[h][...] = jnp.max(s, axis=0, keepdims=True)

    @pl.when(qi == 0)
    def _():
        bias = [band_ref[1, :, h * COLS:(h + 1) * COLS] for h in range(N_KV)]
        if mode == "sel":
            bias = [bias[h] + expand(sel_rows(h, 0, 1), SEL_BLOCK) for h in range(N_KV)]
        first_step(0, 1, bias)

    @pl.when(qi >= 1)
    def _():
        bias = [band_ref[:, :, h * COLS:(h + 1) * COLS].reshape(2 * TK, COLS) for h in range(N_KV)]
        if mode == "sel":
            bias = [bias[h] + expand(sel_rows(h, qi - 1, 2), SEL_BLOCK) for h in range(N_KV)]
        elif mode == "moba":
            own = (qi * TQ) // MOBA_BLOCK == ((qi - 1) * TK) // MOBA_BLOCK
            own_f = jnp.where(own, 1.0, 0.0)
            bias = [bias[h] + expand([penalty(jnp.minimum(moba_row(h, qi - 1) + own_f, 1.0)),
                                      jnp.zeros((1, COLS), F32)], TK) for h in range(N_KV)]
        first_step(qi - 1, 2, bias)
        if mode != "swa":
            produce(0)

    if mode != "swa":
        n_pairs = (n_far + 1) >> 1

        def pair_body(pair, carry):
            flush()
            for h in range(N_KV):
                softmax(h, s_scr[h][...], mb_scr[h][...], False)
            pend_ref[0] = far_lo + 2 * pair
            produce(jnp.minimum(pair + 1, n_pairs - 1))
            return carry
        lax.fori_loop(0, n_pairs, pair_body, 0)

    flush()
    for h in range(N_KV):
        m = m_scr[h][...]
        l = l_scr[h][...]
        acc = acc_scr[h][...]
        if mode == "swa":
            sink = _head_row(sink_ref, h * GROUP)
            m_fin = jnp.maximum(m, sink)
            shrink = jnp.exp2(m - m_fin)
            l = l * shrink + jnp.exp2(sink - m_fin)
            acc = acc * shrink
        out = acc / jnp.maximum(l, TINY)
        o_ref[0, :, h * COLS:(h + 1) * COLS] = _untranspose_heads(out).astype(o_ref.dtype)


def flash_attention(mode, proj, band, c31, *, q_col, k_col, v_col, out_dtype, mask=None, sinks=None):
    b, s, _ = proj.shape
    smem = pl.BlockSpec(memory_space=pltpu.SMEM)
    in_specs = [smem]
    args = [c31]
    if mode == "swa":
        in_specs.append(smem)
        args.append(sinks)
    in_specs += [pl.BlockSpec((2, TK, N_KV * COLS), lambda bi, i: (0, 0, MIXER[mode])),
                 pl.BlockSpec((1, TQ, MIX_WIDTH), lambda bi, i: (bi, i, q_col // MIX_WIDTH)),
                 pl.BlockSpec((1, s, KV_W), lambda bi, i: (bi, 0, k_col // KV_W)),
                 pl.BlockSpec((1, s, KV_W), lambda bi, i: (bi, 0, v_col // KV_W))]
    args += [band, proj, proj, proj]
    if mode == "sel":
        in_specs.append(pl.BlockSpec((1, N_KV, mask.shape[2], TQ), lambda bi, i: (bi, 0, 0, i)))
        args.append(mask)
    elif mode == "moba":
        in_specs.append(pl.BlockSpec((1, N_KV, 1, mask.shape[3], COLS), lambda bi, i: (bi, 0, i, 0, 0)))
        args.append(mask)
    per_head = [pltpu.VMEM((s, HEAD_DIM), BF16), pltpu.VMEM((s // TK, HEAD_DIM, TK), BF16),
                pltpu.VMEM((1, COLS), F32), pltpu.VMEM((1, COLS), F32), pltpu.VMEM((HEAD_DIM, COLS), F32),
                pltpu.VMEM((1, COLS), F32), pltpu.VMEM((2 * TK, COLS), BF16)]
    scratch = [shape for shape in per_head for _ in range(N_KV)] + [pltpu.SMEM((1,), jnp.int32)]
    if mode != "swa":
        scratch += [pltpu.VMEM((2 * TK, COLS), F32)] * N_KV + [pltpu.VMEM((1, COLS), F32)] * N_KV
    return pl.pallas_call(
        functools.partial(_flash_kernel, mode),
        out_shape=jax.ShapeDtypeStruct((b, s, MIX_WIDTH), out_dtype),
        grid=(b, s // TQ),
        in_specs=in_specs,
        out_specs=pl.BlockSpec((1, TQ, MIX_WIDTH), lambda bi, i: (bi, i, 0)),
        scratch_shapes=scratch,
        compiler_params=_cparams(("parallel", "arbitrary")),
        name="flash_" + mode,
    )(*args)


def _nsa_combine_kernel(cmp_ref, sel_ref, win_ref, gate_ref, o_ref):
    gate = jax.nn.sigmoid(gate_ref[...])
    for h in range(N_HEADS):
        cols = slice(h * HEAD_DIM, (h + 1) * HEAD_DIM)
        out = (gate[:, 3 * h:3 * h + 1] * cmp_ref[:, cols]
               + gate[:, 3 * h + 1:3 * h + 2] * sel_ref[:, cols]
               + gate[:, 3 * h + 2:3 * h + 3] * win_ref[:, cols])
        o_ref[:, cols] = out.astype(BF16)


def nsa_combine(o_cmp, o_sel, o_win, proj, *, tm=512):
    t = o_cmp.shape[0]
    o_spec = pl.BlockSpec((tm, MIX_WIDTH), lambda i: (i, 0))
    return pl.pallas_call(
        _nsa_combine_kernel,
        out_shape=jax.ShapeDtypeStruct((t, MIX_WIDTH), BF16),
        grid=(t // tm,),
        in_specs=[o_spec, o_spec, o_spec,
                  pl.BlockSpec((tm, LANES), lambda i: (i, COL_A_GATE // LANES))],
        out_specs=o_spec,
        compiler_params=_cparams(("parallel",)),
        name="nsa_combine",
    )(o_cmp, o_sel, o_win, proj)


def _selection_share(n_sel, n_half):
    blk = np.arange(n_sel)[:, None] * SEL_BLOCK
    starts = np.arange(n_half)[None, :] * CMP_STRIDE
    shared = np.clip(np.minimum(starts + CMP_LEN, blk + SEL_BLOCK) - np.maximum(starts, blk), 0, None)
    shared = shared / CMP_STRIDE
    shared[:, n_half - 1] = 0.0
    return shared.astype(np.float32)


REGROUP_TILE = 512


def _regroup_tables():
    q_w, kv_w = MIX_WIDTH, KV_W
    a_kv = q_w
    a_gate = a_kv + 6 * kv_w
    b_q = a_gate + ORIG_GATE_W
    b_kv = b_q + q_w
    c_q = b_kv + 2 * kv_w
    c_kv = c_q + q_w
    merge = c_kv + 2 * kv_w
    segments = [(COL_A_Q, q_w, 0), (COL_B_Q, q_w, b_q), (COL_C_Q, q_w, c_q), (COL_A_KC, 6 * kv_w, a_kv),
                (COL_B_K, 2 * kv_w, b_kv), (COL_C_K, 2 * kv_w, c_kv), (COL_A_GATE, COL_MERGE - COL_A_GATE, a_gate),
                (COL_MERGE, 3 * D_MODEL, merge)]
    start, is_gate = [], []
    for new0, width, orig0 in segments:
        for off in range(0, width, REGROUP_TILE):
            start.append(orig0 + off)
            is_gate.append(int(new0 == COL_A_GATE))
    return np.asarray(start, np.int32), np.asarray(is_gate, np.int32)


def _regroup_kernel(start_tbl, gate_tbl, wt_ref, o_ref):
    rows = wt_ref[...]
    row = lax.broadcasted_iota(jnp.int32, rows.shape, 0)
    keep = jnp.where(gate_tbl[pl.program_id(1)] == 1, ORIG_GATE_W, REGROUP_TILE)
    o_ref[0] = jnp.where(row < keep, rows, 0.0).T.astype(BF16)


def regroup_w_in(w):
    depth, d, n_in = w.shape
    start, is_gate = (jnp.asarray(t) for t in _regroup_tables())
    return pl.pallas_call(
        _regroup_kernel,
        out_shape=jax.ShapeDtypeStruct((depth, d, N_PROJ), BF16),
        grid_spec=pltpu.PrefetchScalarGridSpec(
            num_scalar_prefetch=2,
            grid=(depth, N_PROJ // REGROUP_TILE),
            in_specs=[pl.BlockSpec((None, pl.Element(REGROUP_TILE), pl.Element(d)),
                                   lambda l, j, st, gt: (l, pl.multiple_of(st[j], ORIG_GATE_W), 0))],
            out_specs=pl.BlockSpec((1, d, REGROUP_TILE), lambda l, j, st, gt: (l, 0, j))),
        compiler_params=_cparams(("parallel", "parallel")),
        name="regroup_w_in",
    )(start, is_gate, jnp.swapaxes(w, 1, 2))


def kernel(x, w_in, cmp_pos, cmp_w1, cmp_w2, swa_sinks, w_branch, w_out, w_mlp_in, w_mlp_out,
           norm_mix, norm_mlp, norm_final, rel_bias):
    b, s, d = x.shape
    depth = w_in.shape[0]
    t = b * s
    n_half = s // CMP_STRIDE
    n_sel = s // SEL_BLOCK

    tbl_flat = rel_bias.reshape(-1)
    c31 = rel_bias[N_BUCKETS - 1]
    band = band_bias(tbl_flat)
    bias_c = cmp_bias(tbl_flat, s)
    share = jnp.asarray(_selection_share(n_sel, n_half), BF16)

    w_in_cols = regroup_w_in(w_in)
    xt = x.reshape(t, d)
    for layer in range(depth):
        proj2d = norm_matmul(xt, norm_mix[layer], w_in_cols, layer)
        proj = proj2d.reshape(b, s, N_PROJ)

        cmp_tokens = nsa_compress(proj, cmp_pos[layer].reshape(2, 2, CMP_STRIDE * HEAD_DIM), cmp_w1, cmp_w2, layer)
        o_cmp, sel_mask = nsa_compressed(bias_c, proj, cmp_tokens, share)
        o_sel = flash_attention("sel", proj, band, c31, q_col=COL_A_Q, k_col=COL_A_KS, v_col=COL_A_VS,
                                out_dtype=F32, mask=sel_mask)
        o_win = flash_attention("win", proj, band, c31, q_col=COL_A_Q, k_col=COL_A_KW, v_col=COL_A_VW,
                                out_dtype=F32)
        o_a = nsa_combine(o_cmp.reshape(t, MIX_WIDTH), o_sel.reshape(t, MIX_WIDTH),
                          o_win.reshape(t, MIX_WIDTH), proj2d)

        o_b = flash_attention("swa", proj, band, c31, q_col=COL_B_Q, k_col=COL_B_K, v_col=COL_B_V,
                              out_dtype=BF16, sinks=swa_sinks[layer])
        moba_mask = moba_gate(proj)
        o_c = flash_attention("moba", proj, band, c31, q_col=COL_C_Q, k_col=COL_C_K, v_col=COL_C_V,
                              out_dtype=BF16, mask=moba_mask)

        z = merge_branches(o_a, o_b.reshape(t, MIX_WIDTH), o_c.reshape(t, MIX_WIDTH), w_branch, layer, proj2d)
        xt = matmul_residual(z, w_out, layer, xt)
        xt = mlp_block(xt, norm_mlp[layer], w_mlp_in, w_mlp_out, layer)

    return final_norm(xt, norm_final).reshape(b, s, d)
```

```python
import functools
import math

import numpy as np
import jax
import jax.numpy as jnp
from jax import lax
from jax.experimental import pallas as pl
from jax.experimental.pallas import tpu as pltpu

F32 = jnp.float32
BF16 = jnp.bfloat16

D_MODEL = 2048
HEAD_DIM = 128
N_HEADS = 8
N_KV = 2
GROUP = N_HEADS // N_KV
MIX_WIDTH = N_HEADS * HEAD_DIM
CMP_LEN = 32
CMP_STRIDE = 16
CMP_HIDDEN = 2 * HEAD_DIM
SEL_BLOCK = 64
SEL_TOP_N = 16
NSA_WINDOW = 512
SWA_WINDOW = 128
MOBA_BLOCK = 256
MOBA_TOP_K = 3
N_BUCKETS = 32
BUCKET_EXACT = N_BUCKETS // 2
BUCKET_MAX_DIST = 128
TOTAL_HEADS = 3 * N_HEADS
D_FF = 4 * D_MODEL
RMS_EPS = 1e-6
NEG_INF = -1e30
FORCE_SCORE = 1e30
TINY = 1e-30
BELOW_ALL = -3e38
SCALE = HEAD_DIM ** -0.5

LANES = 128
BF16_ROWS = 16
VMEM_LIMIT = 56 * 1024 * 1024
TQ = 128
TK = 128
COLS = GROUP * TQ
Q_BLOCKS = 4
KV_W = N_KV * HEAD_DIM
LOG2E = math.log2(math.e)
SCALE2 = SCALE * LOG2E

COL_A_Q = 0
COL_B_Q = 1024
COL_C_Q = 2048
COL_A_KC = 3072
COL_A_KS = 3584
COL_A_VS = 3840
COL_A_KW = 4096
COL_A_VW = 4352
COL_B_K = 4608
COL_B_V = 4864
COL_C_K = 5120
COL_C_V = 5376
COL_A_GATE = 5632
COL_MERGE = 6144
N_PROJ = COL_MERGE + 3 * D_MODEL
ORIG_GATE_W = 3 * N_HEADS

MIXER = {"sel": 0, "win": 0, "swa": 1, "moba": 2}


def _cparams(semantics):
    return pltpu.CompilerParams(dimension_semantics=semantics, vmem_limit_bytes=VMEM_LIMIT)


def _rms(x, gain):
    y = x * lax.rsqrt(jnp.mean(x * x, axis=-1, keepdims=True) + RMS_EPS)
    return y * gain


def _norm_matmul_kernel(x_ref, g_ref, w_ref, o_ref, h_scr):
    @pl.when(pl.program_id(1) == 0)
    def _():
        h_scr[...] = _rms(x_ref[...], g_ref[...]).astype(BF16)

    o_ref[...] = jnp.dot(h_scr[...], w_ref[0], preferred_element_type=F32)


def norm_matmul(x, gain, w, layer, *, tm=1024, tn=1024):
    t, k = x.shape
    n = w.shape[2]
    return pl.pallas_call(
        _norm_matmul_kernel,
        out_shape=jax.ShapeDtypeStruct((t, n), F32),
        grid=(t // tm, n // tn),
        in_specs=[pl.BlockSpec((tm, k), lambda i, j: (i, 0)),
                  pl.BlockSpec((1, k), lambda i, j: (0, 0)),
                  pl.BlockSpec((1, k, tn), lambda i, j: (layer, 0, j))],
        out_specs=pl.BlockSpec((tm, tn), lambda i, j: (i, j)),
        scratch_shapes=[pltpu.VMEM((tm, k), BF16)],
        compiler_params=_cparams(("parallel", "arbitrary")),
        name="norm_matmul",
    )(x, gain.reshape(1, k), w)


def _merge_kernel(oa_ref, ob_ref, oc_ref, wb_ref, g0_ref, g1_ref, g2_ref, z_ref, w_scr):
    @pl.when(pl.program_id(1) == 0)
    def _():
        w_scr[...] = wb_ref[0].astype(BF16)

    acc = jax.nn.sigmoid(g0_ref[...]) * jnp.dot(oa_ref[...], w_scr[0], preferred_element_type=F32)
    acc += jax.nn.sigmoid(g1_ref[...]) * jnp.dot(ob_ref[...], w_scr[1], preferred_element_type=F32)
    acc += jax.nn.sigmoid(g2_ref[...]) * jnp.dot(oc_ref[...], w_scr[2], preferred_element_type=F32)
    z_ref[...] = acc.astype(BF16)


def merge_branches(o_a, o_b, o_c, w_branch, layer, proj, *, tm=512, tn=1024):
    t = o_a.shape[0]
    gate_blk = COL_MERGE // tn
    per_branch = D_MODEL // tn
    o_spec = pl.BlockSpec((tm, MIX_WIDTH), lambda j, i: (i, 0))

    def gate_spec(m):
        return pl.BlockSpec((tm, tn), lambda j, i: (i, gate_blk + m * per_branch + j))

    return pl.pallas_call(
        _merge_kernel,
        out_shape=jax.ShapeDtypeStruct((t, D_MODEL), BF16),
        grid=(D_MODEL // tn, t // tm),
        in_specs=[o_spec, o_spec, o_spec,
                  pl.BlockSpec((1, 3, MIX_WIDTH, tn), lambda j, i: (layer, 0, 0, j)),
                  gate_spec(0), gate_spec(1), gate_spec(2)],
        out_specs=pl.BlockSpec((tm, tn), lambda j, i: (i, j)),
        scratch_shapes=[pltpu.VMEM((3, MIX_WIDTH, tn), BF16)],
        compiler_params=_cparams(("parallel", "arbitrary")),
        name="merge_branches",
    )(o_a, o_b, o_c, w_branch, proj, proj, proj)


def _matmul_res_kernel(a_ref, w_ref, x_ref, o_ref, w_scr):
    @pl.when(pl.program_id(1) == 0)
    def _():
        w_scr[...] = w_ref[0].astype(BF16)

    o_ref[...] = x_ref[...] + jnp.dot(a_ref[...], w_scr[...], preferred_element_type=F32)


def matmul_residual(a, w, layer, x, *, tm=1024, tn=1024):
    t, k = a.shape
    n = w.shape[2]
    return pl.pallas_call(
        _matmul_res_kernel,
        out_shape=jax.ShapeDtypeStruct((t, n), F32),
        grid=(n // tn, t // tm),
        in_specs=[pl.BlockSpec((tm, k), lambda j, i: (i, 0)),
                  pl.BlockSpec((1, k, tn), lambda j, i: (layer, 0, j)),
                  pl.BlockSpec((tm, tn), lambda j, i: (i, j))],
        out_specs=pl.BlockSpec((tm, tn), lambda j, i: (i, j)),
        scratch_shapes=[pltpu.VMEM((k, tn), BF16)],
        compiler_params=_cparams(("parallel", "arbitrary")),
        name="matmul_residual",
    )(a, w, x)


def _mlp_kernel(x_ref, g_ref, w1_ref, w2_ref, o_ref, h_scr):
    @pl.when(pl.program_id(1) == 0)
    def _():
        x = x_ref[...]
        h_scr[...] = _rms(x, g_ref[...]).astype(BF16)
        o_ref[...] = x

    u = jnp.dot(h_scr[...], w1_ref[0].astype(BF16), preferred_element_type=F32)
    u = jnp.square(jnp.maximum(u, 0.0)).astype(BF16)
    o_ref[...] += jnp.dot(u, w2_ref[0].astype(BF16), preferred_element_type=F32)


def mlp_block(x, gain, w1, w2, layer, *, tm=1024, tf=512):
    t, d = x.shape
    dff = w1.shape[2]
    return pl.pallas_call(
        _mlp_kernel,
        out_shape=jax.ShapeDtypeStruct((t, d), F32),
        grid=(t // tm, dff // tf),
        in_specs=[pl.BlockSpec((tm, d), lambda i, f: (i, 0)),
                  pl.BlockSpec((1, d), lambda i, f: (0, 0)),
                  pl.BlockSpec((1, d, tf), lambda i, f: (layer, 0, f)),
                  pl.BlockSpec((1, tf, d), lambda i, f: (layer, f, 0))],
        out_specs=pl.BlockSpec((tm, d), lambda i, f: (i, 0)),
        scratch_shapes=[pltpu.VMEM((tm, d), BF16)],
        compiler_params=_cparams(("parallel", "arbitrary")),
        name="mlp_block",
    )(x, gain.reshape(1, d), w1, w2)


def _final_norm_kernel(x_ref, g_ref, o_ref):
    o_ref[...] = _rms(x_ref[...], g_ref[...])


def final_norm(x, gain, *, tm=512):
    t, d = x.shape
    return pl.pallas_call(
        _final_norm_kernel,
        out_shape=jax.ShapeDtypeStruct((t, d), F32),
        grid=(t // tm,),
        in_specs=[pl.BlockSpec((tm, d), lambda i: (i, 0)),
                  pl.BlockSpec((1, d), lambda i: (0, 0))],
        out_specs=pl.BlockSpec((tm, d), lambda i: (i, 0)),
        compiler_params=_cparams(("parallel",)),
        name="final_norm",
    )(x, gain.reshape(1, d))


def _t5_bucket(n):
    log_ratio = jnp.log(jnp.maximum(n, 1).astype(F32) / BUCKET_EXACT) / math.log(BUCKET_MAX_DIST / BUCKET_EXACT)
    large = jnp.minimum(BUCKET_EXACT + (log_ratio * (N_BUCKETS - BUCKET_EXACT)).astype(jnp.int32), N_BUCKETS - 1)
    return jnp.where(n < BUCKET_EXACT, n, large)


def _lookup_bias(tbl_ref, head, n):
    bucket = _t5_bucket(n)
    out = jnp.zeros(n.shape, F32)
    for b in range(N_BUCKETS):
        out = jnp.where(bucket == b, tbl_ref[b * TOTAL_HEADS + head], out)
    return out


def _band_bias_kernel(tbl_ref, o_ref):
    l = lax.broadcasted_iota(jnp.int32, (TK, TQ), 0)
    q = lax.broadcasted_iota(jnp.int32, (TK, TQ), 1)
    head = pl.program_id(0)
    prev = _lookup_bias(tbl_ref, head, jnp.maximum(q + TK - l, 0)) * LOG2E
    diag = _lookup_bias(tbl_ref, head, jnp.maximum(q - l, 0)) * LOG2E
    sliding = (head >= MIXER["swa"] * N_HEADS) & (head < (MIXER["swa"] + 1) * N_HEADS)
    o_ref[0] = jnp.where(sliding & (q + TK - l >= SWA_WINDOW), NEG_INF, prev)
    o_ref[1] = jnp.where(l > q, NEG_INF, diag)


def band_bias(tbl_flat):
    return pl.pallas_call(
        _band_bias_kernel,
        out_shape=jax.ShapeDtypeStruct((2, TK, TOTAL_HEADS * TQ), F32),
        grid=(TOTAL_HEADS,),
        in_specs=[pl.BlockSpec(memory_space=pltpu.SMEM)],
        out_specs=pl.BlockSpec((2, TK, TQ), lambda h: (0, 0, h)),
        compiler_params=_cparams(("parallel",)),
        name="band_bias",
    )(tbl_flat)


def _cmp_bias_kernel(tbl_ref, o_ref):
    c = lax.broadcasted_iota(jnp.int32, (LANES, TQ), 0)
    t = pl.program_id(0) * TQ + lax.broadcasted_iota(jnp.int32, (LANES, TQ), 1)
    o_ref[0] = _lookup_bias(tbl_ref, pl.program_id(1), jnp.maximum(t - (c * CMP_STRIDE + CMP_LEN - 1), 0))


def cmp_bias(tbl_flat, s):
    return pl.pallas_call(
        _cmp_bias_kernel,
        out_shape=jax.ShapeDtypeStruct((s // TQ, LANES, N_HEADS * TQ), F32),
        grid=(s // TQ, N_HEADS),
        in_specs=[pl.BlockSpec(memory_space=pltpu.SMEM)],
        out_specs=pl.BlockSpec((1, LANES, TQ), lambda i, h: (i, 0, h)),
        compiler_params=_cparams(("parallel", "parallel")),
        name="cmp_bias",
    )(tbl_flat)


def _compress_kernel(x_ref, pos_ref, w1_ref, w2_ref, o_ref):
    n_half = x_ref.shape[1] // CMP_STRIDE
    half = jnp.concatenate([x_ref[0, pl.ds(l, n_half, stride=CMP_STRIDE), :] for l in range(CMP_STRIDE)], axis=1)
    half_w = CMP_STRIDE * HEAD_DIM
    xa = (half + pos_ref[0, 0:1, :]).astype(BF16)
    xb = (half + pos_ref[0, 1:2, :]).astype(BF16)
    ha = jnp.dot(xa, w1_ref[0, 0, 0:half_w, :].astype(BF16), preferred_element_type=F32)
    hb = jnp.dot(xb, w1_ref[0, 0, half_w:2 * half_w, :].astype(BF16), preferred_element_type=F32)
    hidden = ha + pltpu.roll(hb, n_half - 1, 0)
    act = jax.nn.gelu(hidden, approximate=True).astype(BF16)
    o_ref[0, 0, 0] = jnp.dot(act, w2_ref[0, 0].astype(BF16), preferred_element_type=F32)


def nsa_compress(proj, pos, w1, w2, layer):
    b, s, _ = proj.shape
    n_kv, n_half, width = N_KV, s // CMP_STRIDE, CMP_STRIDE * HEAD_DIM
    col_blk = COL_A_KC // HEAD_DIM
    return pl.pallas_call(
        _compress_kernel,
        out_shape=jax.ShapeDtypeStruct((b, 2, n_kv, n_half, HEAD_DIM), F32),
        grid=(2, b, n_kv),
        in_specs=[pl.BlockSpec((1, s, HEAD_DIM), lambda kv, bi, h: (bi, 0, col_blk + kv * n_kv + h)),
                  pl.BlockSpec((1, 2, width), lambda kv, bi, h: (kv, 0, 0)),
                  pl.BlockSpec((1, 1, 2 * width, CMP_HIDDEN), lambda kv, bi, h: (layer, kv, 0, 0)),
                  pl.BlockSpec((1, 1, CMP_HIDDEN, HEAD_DIM), lambda kv, bi, h: (layer, kv, 0, 0))],
        out_specs=pl.BlockSpec((1, 1, 1, n_half, HEAD_DIM), lambda kv, bi, h: (bi, kv, h, 0, 0)),
        compiler_params=_cparams(("parallel", "parallel", "parallel")),
        name="nsa_compress",
    )(proj, pos, w1, w2)


def _stack_heads(q):
    return jnp.concatenate([q[:, g * HEAD_DIM:(g + 1) * HEAD_DIM] for g in range(GROUP)], axis=0)


def _untranspose_heads(o_t):
    return jnp.concatenate([o_t[:, g * TQ:(g + 1) * TQ].T for g in range(GROUP)], axis=1)


def _head_row(ref, first):
    return jnp.concatenate([jnp.full((1, TQ), ref[first + g] * LOG2E, F32) for g in range(GROUP)], axis=1)


def _split3(x):
    x1 = x.astype(BF16)
    r1 = x - x1.astype(F32)
    x2 = r1.astype(BF16)
    x3 = (r1 - x2.astype(F32)).astype(BF16)
    return x1, x2, x3


def _dot_nt(a, b):
    return lax.dot_general(a, b, (((1,), (1,)), ((), ())), preferred_element_type=F32)


def _rank_select(score, n_cand, n_top):
    idx = lax.broadcasted_iota(jnp.int32, score.shape, 0)
    rank = jnp.zeros(score.shape, jnp.int32)
    for j in range(n_cand):
        other = score[j:j + 1, :]
        ahead = (other > score) | ((other == score) & (j < idx))
        rank += jnp.where(ahead, 1, 0)
    return jnp.where((rank < n_top) & (idx < n_cand), 1.0, 0.0)


def _nsa_cmp_kernel(*refs):
    def q_block(sub, carry):
        _nsa_cmp_q_block(pl.program_id(1) * Q_BLOCKS + sub, sub, *refs)
        return carry
    lax.fori_loop(0, Q_BLOCKS, q_block, 0)


def _nsa_cmp_q_block(qi, sub, bias_ref, q_ref, kc_ref, vc_ref, share_ref, o_ref, sel_ref):
    q_rows = pl.ds(pl.multiple_of(sub * TQ, TQ), TQ)
    share = share_ref[...]
    n_sel = share.shape[0]
    c = lax.broadcasted_iota(jnp.int32, (LANES, COLS), 0)
    t = qi * TQ + (lax.broadcasted_iota(jnp.int32, (LANES, COLS), 1) & (TQ - 1))
    valid = t - (c * CMP_STRIDE + CMP_LEN - 1) >= 0
    blk = lax.broadcasted_iota(jnp.int32, (n_sel, TQ), 0)
    cur = (qi * TQ + lax.broadcasted_iota(jnp.int32, (n_sel, TQ), 1)) // SEL_BLOCK
    forced = (blk == 0) | (blk == cur) | (blk == cur - 1)
    for h in range(N_KV):
        cols = slice(h * COLS, (h + 1) * COLS)
        q4 = _stack_heads(q_ref[0, q_rows, cols]).astype(BF16)
        kc = kc_ref[0, 0, h].astype(BF16)
        vc_t = vc_ref[0, 0, h].T.astype(BF16)
        logits = _dot_nt(kc, q4) * SCALE + bias_ref[sub, :, cols]
        logits = jnp.where(valid, logits, NEG_INF)
        m = jnp.max(logits, axis=0, keepdims=True)
        p = jnp.where(valid, jnp.exp(logits - m), 0.0)
        p = p / jnp.maximum(jnp.sum(p, axis=0, keepdims=True), TINY)
        o_ref[0, q_rows, cols] = _untranspose_heads(jnp.dot(vc_t, p.astype(BF16), preferred_element_type=F32))

        p_sum = p[:, 0:TQ]
        for g in range(1, GROUP):
            p_sum = p_sum + p[:, g * TQ:(g + 1) * TQ]
        importance = sum(jnp.dot(share, part, preferred_element_type=F32) for part in _split3(p_sum))
        score = jnp.where(forced, FORCE_SCORE, jnp.where(blk <= cur, importance, NEG_INF))
        sel_ref[0, h, sub] = _rank_select(score, n_sel, min(SEL_TOP_N, n_sel))


def nsa_compressed(bias_c, proj, cmp_tokens, share):
    b, s, _ = proj.shape
    n_cmp = cmp_tokens.shape[3]
    n_sel = share.shape[0]
    return pl.pallas_call(
        _nsa_cmp_kernel,
        out_shape=(jax.ShapeDtypeStruct((b, s, MIX_WIDTH), F32),
                   jax.ShapeDtypeStruct((b, N_KV, s // TQ, n_sel, TQ), F32)),
        grid=(b, s // (Q_BLOCKS * TQ)),
        in_specs=[pl.BlockSpec((Q_BLOCKS, n_cmp, N_KV * COLS), lambda bi, i: (i, 0, 0)),
                  pl.BlockSpec((1, Q_BLOCKS * TQ, MIX_WIDTH), lambda bi, i: (bi, i, COL_A_Q // MIX_WIDTH)),
                  pl.BlockSpec((1, 1, N_KV, n_cmp, HEAD_DIM), lambda bi, i: (bi, 0, 0, 0, 0)),
                  pl.BlockSpec((1, 1, N_KV, n_cmp, HEAD_DIM), lambda bi, i: (bi, 1, 0, 0, 0)),
                  pl.BlockSpec((n_sel, n_cmp), lambda bi, i: (0, 0))],
        out_specs=(pl.BlockSpec((1, Q_BLOCKS * TQ, MIX_WIDTH), lambda bi, i: (bi, i, 0)),
                   pl.BlockSpec((1, N_KV, Q_BLOCKS, n_sel, TQ), lambda bi, i: (bi, 0, i, 0, 0))),
        compiler_params=_cparams(("parallel", "parallel")),
        name="nsa_compressed",
    )(bias_c, proj, cmp_tokens, cmp_tokens, share)


def _moba_gate_kernel(q_ref, k_ref, sel_ref):
    s = q_ref.shape[1]
    n_blk = s // MOBA_BLOCK
    n = lax.broadcasted_iota(jnp.int32, (BF16_ROWS, GROUP * s), 0)
    q_blk = (lax.broadcasted_iota(jnp.int32, (BF16_ROWS, GROUP * s), 1) & (s - 1)) // MOBA_BLOCK
    for h in range(N_KV):
        k = k_ref[0, :, h * HEAD_DIM:(h + 1) * HEAD_DIM]
        k_mean = [jnp.sum(k[j * MOBA_BLOCK:(j + 1) * MOBA_BLOCK], axis=0, keepdims=True) / MOBA_BLOCK
                  for j in range(n_blk)]
        k_mean = jnp.concatenate(k_mean + [jnp.zeros((BF16_ROWS - n_blk, HEAD_DIM), F32)], axis=0)
        q4 = jnp.concatenate([q_ref[0, :, (h * GROUP + g) * HEAD_DIM:(h * GROUP + g + 1) * HEAD_DIM]
                              for g in range(GROUP)], axis=0)
        q_hi, q_lo, _ = _split3(q4)
        k_hi, k_lo, _ = _split3(k_mean)
        gate = _dot_nt(k_hi, q_hi) + _dot_nt(k_lo, q_hi) + _dot_nt(k_hi, q_lo)
        gate = jnp.where(n < q_blk, gate, NEG_INF)
        gate = jnp.where(n < n_blk, gate, BELOW_ALL)
        picked = _rank_select(gate, n_blk, min(MOBA_TOP_K, n_blk - 1))
        picked = jnp.where(n < q_blk, picked, 0.0)
        for qi in range(s // TQ):
            for g in range(GROUP):
                sel_ref[0, h, qi, :, g * TQ:(g + 1) * TQ] = picked[:, g * s + qi * TQ:g * s + (qi + 1) * TQ]


def moba_gate(proj):
    b, s, _ = proj.shape
    assert s & (s - 1) == 0
    return pl.pallas_call(
        _moba_gate_kernel,
        out_shape=jax.ShapeDtypeStruct((b, N_KV, s // TQ, BF16_ROWS, COLS), F32),
        grid=(b,),
        in_specs=[pl.BlockSpec((1, s, MIX_WIDTH), lambda bi: (bi, 0, COL_C_Q // MIX_WIDTH)),
                  pl.BlockSpec((1, s, KV_W), lambda bi: (bi, 0, COL_C_K // KV_W))],
        out_specs=pl.BlockSpec((1, N_KV, s // TQ, BF16_ROWS, COLS), lambda bi: (bi, 0, 0, 0, 0)),
        compiler_params=_cparams(("parallel",)),
        name="moba_gate",
    )(proj, proj)


def _flash_q_block(mode, qi, sub, *refs):
    n_in = 5 if mode == "win" else 6
    ins, o_ref, scr = refs[:n_in], refs[n_in], refs[n_in + 1:]
    if mode == "swa":
        c31_ref, sink_ref, band_ref, q_ref, k_ref, v_ref = ins
    elif mode == "win":
        c31_ref, band_ref, q_ref, k_ref, v_ref = ins
    else:
        c31_ref, band_ref, q_ref, k_ref, v_ref, msk_ref = ins
    kb_scr, vt_scr, m_scr, l_scr, acc_scr, alpha_scr, pb_scr = (scr[i * N_KV:(i + 1) * N_KV] for i in range(7))
    pend_ref = scr[7 * N_KV]
    s_scr = scr[7 * N_KV + 1:8 * N_KV + 1]
    mb_scr = scr[8 * N_KV + 1:]
    n_kb = k_ref.shape[1] // TK
    head0 = MIXER[mode] * N_HEADS
    q_rows = pl.ds(pl.multiple_of(sub * TQ, TQ), TQ)

    @pl.when(qi == 0)
    def _():
        for h in range(N_KV):
            cols = slice(h * HEAD_DIM, (h + 1) * HEAD_DIM)
            kb_scr[h][...] = k_ref[0, :, cols].astype(BF16)
            for j in range(n_kb):
                vt_scr[h][j] = v_ref[0, j * TK:(j + 1) * TK, cols].T.astype(BF16)

    q4 = [(_stack_heads(q_ref[0, q_rows, h * COLS:(h + 1) * COLS]) * SCALE2).astype(BF16) for h in range(N_KV)]
    far_bias = [_head_row(c31_ref, head0 + h * GROUP) for h in range(N_KV)]

    def penalty(keep):
        return (keep - 1.0) * (-NEG_INF)

    def expand(rows, seg):
        return jnp.concatenate([jnp.broadcast_to(r, (seg, COLS)) for r in rows], axis=0)

    def sel_rows(h, first_blk, n_blocks):
        per = TK // SEL_BLOCK
        rows = []
        for part in range(n_blocks * per):
            r = msk_ref[0, h, sub, pl.ds(first_blk * per + part, 1), :]
            rows.append(jnp.concatenate([penalty(r)] * GROUP, axis=1))
        return rows

    def moba_row(h, key_blk):
        return msk_ref[0, h, sub, pl.ds((key_blk * TK) // MOBA_BLOCK, 1), :]

    far_lo = jnp.maximum(qi - NSA_WINDOW // TK, 0) if mode == "win" else 0
    n_far = jnp.maximum(qi - 1 - far_lo, 0)

    def far_bias_tile(h, k0):
        tail = jnp.where(k0 + 1 >= qi - 1, NEG_INF, 0.0)
        if mode == "sel":
            rows = [far_bias[h] + r for r in sel_rows(h, k0, 2)]
            per = TK // SEL_BLOCK
            return expand(rows[:per] + [r + tail for r in rows[per:]], SEL_BLOCK)
        if mode == "moba":
            row = far_bias[h] + penalty(moba_row(h, k0))
            return expand([row, row + tail], TK)
        key = lax.broadcasted_iota(jnp.int32, (TK, COLS), 0)
        qry = lax.broadcasted_iota(jnp.int32, (TK, COLS), 1) & (TQ - 1)
        edge = jnp.where(k0 == qi - NSA_WINDOW // TK, NEG_INF, 0.0)
        first = far_bias[h] + jnp.where(key <= qry, edge, 0.0)
        return jnp.concatenate([first, jnp.broadcast_to(far_bias[h] + tail, (TK, COLS))], axis=0)

    def scores(h, k0, n_blocks, bias):
        start = pl.multiple_of(k0 * TK, TK)
        return _dot_nt(kb_scr[h][pl.ds(start, n_blocks * TK), :], q4[h]) + bias

    def flush():
        k0 = pend_ref[0]
        for h in range(N_KV):
            v_t = jnp.concatenate([vt_scr[h][k0], vt_scr[h][k0 + 1]], axis=1)
            pv = jnp.dot(v_t, pb_scr[h][...], preferred_element_type=F32)
            acc_scr[h][...] = alpha_scr[h][...] * acc_scr[h][...] + pv

    def softmax(h, s, m_blk, first):
        n_keys = s.shape[0]
        if first:
            m_new = m_blk
        else:
            m_prev = m_scr[h][...]
            m_new = jnp.maximum(m_prev, m_blk)
            alpha = jnp.exp2(m_prev - m_new)
        p = jnp.exp2(s - m_new)
        p_sum = jnp.sum(p, axis=0, keepdims=True)
        pb_scr[h][0:n_keys, :] = p.astype(BF16)
        if n_keys == TK:
            pb_scr[h][TK:2 * TK, :] = jnp.zeros((TK, COLS), BF16)
        if first:
            l_scr[h][...] = p_sum
            alpha_scr[h][...] = jnp.zeros((1, COLS), F32)
            acc_scr[h][...] = jnp.zeros((HEAD_DIM, COLS), F32)
        else:
            l_scr[h][...] = alpha * l_scr[h][...] + p_sum
            alpha_scr[h][...] = alpha
        m_scr[h][...] = m_new

    def first_step(k0, n_blocks, bias):
        for h in range(N_KV):
            s = scores(h, k0, n_blocks, bias[h])
            softmax(h, s, jnp.max(s, axis=0, keepdims=True), True)
        pend_ref[0] = k0

    def produce(pair):
        k0 = far_lo + 2 * pair
        for h in range(N_KV):
            s = scores(h, k0, 2, far_bias_tile(h, k0))
            s_scr[h][...] = s
            mb_scr[h][...] = jnp.max(s, axis=0, keepdims=True)

    @pl.when(qi == 0)
    def _():
        bias = [band_ref[1, :, h * COLS:(h + 1) * COLS] for h in range(N_KV)]
        if mode == "sel":
            bias = [bias[h] + expand(sel_rows(h, 0, 1), SEL_BLOCK) for h in range(N_KV)]
        first_step(0, 1, bias)

    @pl.when(qi >= 1)
    def _():
        bias = [band_ref[:, :, h * COLS:(h + 1) * COLS].reshape(2 * TK, COLS) for h in range(N_KV)]
        if mode == "sel":
            bias = [bias[h] + expand(sel_rows(h, qi - 1, 2), SEL_BLOCK) for h in range(N_KV)]
        elif mode == "moba":
            own = (qi * TQ) // MOBA_BLOCK == ((qi - 1) * TK) // MOBA_BLOCK
            own_f = jnp.where(own, 1.0, 0.0)
            bias = [bias[h] + expand([penalty(jnp.minimum(moba_row(h, qi - 1) + own_f, 1.0)),
                                      jnp.zeros((1, COLS), F32)], TK) for h in range(N_KV)]
        first_step(qi - 1, 2, bias)
        if mode != "swa":
            produce(0)

    if mode != "swa":
        n_pairs = (n_far + 1) >> 1

        def pair_body(pair, carry):
            flush()
            for h in range(N_KV):
                softmax(h, s_scr[h][...], mb_scr[h][...], False)
            pend_ref[0] = far_lo + 2 * pair
            produce(jnp.minimum(pair + 1, n_pairs - 1))
            return carry
        lax.fori_loop(0, n_pairs, pair_body, 0)

    flush()
    for h in range(N_KV):
        m = m_scr[h][...]
        l = l_scr[h][...]
        acc = acc_scr[h][...]
        if mode == "swa":
            sink = _head_row(sink_ref, h * GROUP)
            m_fin = jnp.maximum(m, sink)
            shrink = jnp.exp2(m - m_fin)
            l = l * shrink + jnp.exp2(sink - m_fin)
            acc = acc * shrink
        out = acc / jnp.maximum(l, TINY)
        o_ref[0, q_rows, h * COLS:(h + 1) * COLS] = _untranspose_heads(out).astype(o_ref.dtype)


def _flash_kernel(mode, *refs):
    def q_block(sub, carry):
        _flash_q_block(mode, pl.program_id(1) * Q_BLOCKS + sub, sub, *refs)
        return carry
    lax.fori_loop(0, Q_BLOCKS, q_block, 0)


def flash_attention(mode, proj, band, c31, *, q_col, k_col, v_col, out_dtype, mask=None, sinks=None):
    b, s, _ = proj.shape
    smem = pl.BlockSpec(memory_space=pltpu.SMEM)
    in_specs = [smem]
    args = [c31]
    if mode == "swa":
        in_specs.append(smem)
        args.append(sinks)
    in_specs += [pl.BlockSpec((2, TK, N_KV * COLS), lambda bi, i: (0, 0, MIXER[mode])),
                 pl.BlockSpec((1, Q_BLOCKS * TQ, MIX_WIDTH), lambda bi, i: (bi, i, q_col // MIX_WIDTH)),
                 pl.BlockSpec((1, s, KV_W), lambda bi, i: (bi, 0, k_col // KV_W)),
                 pl.BlockSpec((1, s, KV_W), lambda bi, i: (bi, 0, v_col // KV_W))]
    args += [band, proj, proj, proj]
    if mode == "sel":
        in_specs.append(pl.BlockSpec((1, N_KV, Q_BLOCKS, mask.shape[3], TQ), lambda bi, i: (bi, 0, i, 0, 0)))
        args.append(mask)
    elif mode == "moba":
        in_specs.append(pl.BlockSpec((1, N_KV, Q_BLOCKS, mask.shape[3], COLS), lambda bi, i: (bi, 0, i, 0, 0)))
        args.append(mask)
    per_head = [pltpu.VMEM((s, HEAD_DIM), BF16), pltpu.VMEM((s // TK, HEAD_DIM, TK), BF16),
                pltpu.VMEM((1, COLS), F32), pltpu.VMEM((1, COLS), F32), pltpu.VMEM((HEAD_DIM, COLS), F32),
                pltpu.VMEM((1, COLS), F32), pltpu.VMEM((2 * TK, COLS), BF16)]
    scratch = [shape for shape in per_head for _ in range(N_KV)] + [pltpu.SMEM((1,), jnp.int32)]
    if mode != "swa":
        scratch += [pltpu.VMEM((2 * TK, COLS), F32)] * N_KV + [pltpu.VMEM((1, COLS), F32)] * N_KV
    return pl.pallas_call(
        functools.partial(_flash_kernel, mode),
        out_shape=jax.ShapeDtypeStruct((b, s, MIX_WIDTH), out_dtype),
        grid=(b, s // (Q_BLOCKS * TQ)),
        in_specs=in_specs,
        out_specs=pl.BlockSpec((1, Q_BLOCKS * TQ, MIX_WIDTH), lambda bi, i: (bi, i, 0)),
        scratch_shapes=scratch,
        compiler_params=_cparams(("parallel", "arbitrary")),
        name="flash_" + mode,
    )(*args)


def _nsa_combine_kernel(cmp_ref, sel_ref, win_ref, gate_ref, o_ref):
    gate = jax.nn.sigmoid(gate_ref[...])
    for h in range(N_HEADS):
        cols = slice(h * HEAD_DIM, (h + 1) * HEAD_DIM)
        out = (gate[:, 3 * h:3 * h + 1] * cmp_ref[:, cols]
               + gate[:, 3 * h + 1:3 * h + 2] * sel_ref[:, cols]
               + gate[:, 3 * h + 2:3 * h + 3] * win_ref[:, cols])
        o_ref[:, cols] = out.astype(BF16)


def nsa_combine(o_cmp, o_sel, o_win, proj, *, tm=512):
    t = o_cmp.shape[0]
    o_spec = pl.BlockSpec((tm, MIX_WIDTH), lambda i: (i, 0))
    return pl.pallas_call(
        _nsa_combine_kernel,
        out_shape=jax.ShapeDtypeStruct((t, MIX_WIDTH), BF16),
        grid=(t // tm,),
        in_specs=[o_spec, o_spec, o_spec,
                  pl.BlockSpec((tm, LANES), lambda i: (i, COL_A_GATE // LANES))],
        out_specs=o_spec,
        compiler_params=_cparams(("parallel",)),
        name="nsa_combine",
    )(o_cmp, o_sel, o_win, proj)


def _selection_share(n_sel, n_half):
    blk = np.arange(n_sel)[:, None] * SEL_BLOCK
    starts = np.arange(n_half)[None, :] * CMP_STRIDE
    shared = np.clip(np.minimum(starts + CMP_LEN, blk + SEL_BLOCK) - np.maximum(starts, blk), 0, None)
    shared = shared / CMP_STRIDE
    shared[:, n_half - 1] = 0.0
    return shared.astype(np.float32)


REGROUP_TILE = 512


def _regroup_tables():
    q_w, kv_w = MIX_WIDTH, KV_W
    a_kv = q_w
    a_gate = a_kv + 6 * kv_w
    b_q = a_gate + ORIG_GATE_W
    b_kv = b_q + q_w
    c_q = b_kv + 2 * kv_w
    c_kv = c_q + q_w
    merge = c_kv + 2 * kv_w
    segments = [(COL_A_Q, q_w, 0), (COL_B_Q, q_w, b_q), (COL_C_Q, q_w, c_q), (COL_A_KC, 6 * kv_w, a_kv),
                (COL_B_K, 2 * kv_w, b_kv), (COL_C_K, 2 * kv_w, c_kv), (COL_A_GATE, COL_MERGE - COL_A_GATE, a_gate),
                (COL_MERGE, 3 * D_MODEL, merge)]
    start, is_gate = [], []
    for new0, width, orig0 in segments:
        for off in range(0, width, REGROUP_TILE):
            start.append(orig0 + off)
            is_gate.append(int(new0 == COL_A_GATE))
    return np.asarray(start, np.int32), np.asarray(is_gate, np.int32)


def _regroup_kernel(start_tbl, gate_tbl, wt_ref, o_ref):
    rows = wt_ref[...]
    row = lax.broadcasted_iota(jnp.int32, rows.shape, 0)
    keep = jnp.where(gate_tbl[pl.program_id(1)] == 1, ORIG_GATE_W, REGROUP_TILE)
    o_ref[0] = jnp.where(row < keep, rows, 0.0).T.astype(BF16)


def regroup_w_in(w):
    depth, d, n_in = w.shape
    start, is_gate = (jnp.asarray(t) for t in _regroup_tables())
    return pl.pallas_call(
        _regroup_kernel,
        out_shape=jax.ShapeDtypeStruct((depth, d, N_PROJ), BF16),
        grid_spec=pltpu.PrefetchScalarGridSpec(
            num_scalar_prefetch=2,
            grid=(depth, N_PROJ // REGROUP_TILE),
            in_specs=[pl.BlockSpec((None, pl.Element(REGROUP_TILE), pl.Element(d)),
                                   lambda l, j, st, gt: (l, pl.multiple_of(st[j], ORIG_GATE_W), 0))],
            out_specs=pl.BlockSpec((1, d, REGROUP_TILE), lambda l, j, st, gt: (l, 0, j))),
        compiler_params=_cparams(("parallel", "parallel")),
        name="regroup_w_in",
    )(start, is_gate, jnp.swapaxes(w, 1, 2))


def kernel(x, w_in, cmp_pos, cmp_w1, cmp_w2, swa_sinks, w_branch, w_out, w_mlp_in, w_mlp_out,
           norm_mix, norm_mlp, norm_final, rel_bias):
    b, s, d = x.shape
    depth = w_in.shape[0]
    t = b * s
    n_half = s // CMP_STRIDE
    n_sel = s // SEL_BLOCK

    tbl_flat = rel_bias.reshape(-1)
    c31 = rel_bias[N_BUCKETS - 1]
    band = band_bias(tbl_flat)
    bias_c = cmp_bias(tbl_flat, s)
    share = jnp.asarray(_selection_share(n_sel, n_half), BF16)

    w_in_cols = regroup_w_in(w_in)
    xt = x.reshape(t, d)
    for layer in range(depth):
        proj2d = norm_matmul(xt, norm_mix[layer], w_in_cols, layer)
        proj = proj2d.reshape(b, s, N_PROJ)

        cmp_tokens = nsa_compress(proj, cmp_pos[layer].reshape(2, 2, CMP_STRIDE * HEAD_DIM), cmp_w1, cmp_w2, layer)
        o_cmp, sel_mask = nsa_compressed(bias_c, proj, cmp_tokens, share)
        o_sel = flash_attention("sel", proj, band, c31, q_col=COL_A_Q, k_col=COL_A_KS, v_col=COL_A_VS,
                                out_dtype=F32, mask=sel_mask)
        o_win = flash_attention("win", proj, band, c31, q_col=COL_A_Q, k_col=COL_A_KW, v_col=COL_A_VW,
                                out_dtype=F32)
        o_a = nsa_combine(o_cmp.reshape(t, MIX_WIDTH), o_sel.reshape(t, MIX_WIDTH),
                          o_win.reshape(t, MIX_WIDTH), proj2d)

        o_b = flash_attention("swa", proj, band, c31, q_col=COL_B_Q, k_col=COL_B_K, v_col=COL_B_V,
                              out_dtype=BF16, sinks=swa_sinks[layer])
        moba_mask = moba_gate(proj)
        o_c = flash_attention("moba", proj, band, c31, q_col=COL_C_Q, k_col=COL_C_K, v_col=COL_C_V,
                              out_dtype=BF16, mask=moba_mask)

        z = merge_branches(o_a, o_b.reshape(t, MIX_WIDTH), o_c.reshape(t, MIX_WIDTH), w_branch, layer, proj2d)
        xt = matmul_residual(z, w_out, layer, xt)
        xt = mlp_block(xt, norm_mlp[layer], w_mlp_in, w_mlp_out, layer)

    return final_norm(xt, norm_final).reshape(b, s, d)
```

```python
import functools
import math

import numpy as np
import jax
import jax.numpy as jnp
from jax import lax
from jax.experimental import pallas as pl
from jax.experimental.pallas import tpu as pltpu

F32 = jnp.float32
BF16 = jnp.bfloat16

D_MODEL = 2048
HEAD_DIM = 128
N_HEADS = 8
N_KV = 2
GROUP = N_HEADS // N_KV
MIX_WIDTH = N_HEADS * HEAD_DIM
CMP_LEN = 32
CMP_STRIDE = 16
CMP_HIDDEN = 2 * HEAD_DIM
SEL_BLOCK = 64
SEL_TOP_N = 16
NSA_WINDOW = 512
SWA_WINDOW = 128
MOBA_BLOCK = 256
MOBA_TOP_K = 3
N_BUCKETS = 32
BUCKET_EXACT = N_BUCKETS // 2
BUCKET_MAX_DIST = 128
TOTAL_HEADS = 3 * N_HEADS
D_FF = 4 * D_MODEL
RMS_EPS = 1e-6
NEG_INF = -1e30
FORCE_SCORE = 1e30
TINY = 1e-30
BELOW_ALL = -3e38
SCALE = HEAD_DIM ** -0.5

LANES = 128
BF16_ROWS = 16
VMEM_LIMIT = 56 * 1024 * 1024
TQ = 128
TK = 128
COLS = GROUP * TQ
Q_BLOCKS = 4
KV_W = N_KV * HEAD_DIM
LOG2E = math.log2(math.e)
SCALE2 = SCALE * LOG2E

COL_A_Q = 0
COL_B_Q = 1024
COL_C_Q = 2048
COL_A_KC = 3072
COL_A_KS = 3584
COL_A_VS = 3840
COL_A_KW = 4096
COL_A_VW = 4352
COL_B_K = 4608
COL_B_V = 4864
COL_C_K = 5120
COL_C_V = 5376
COL_A_GATE = 5632
COL_MERGE = 6144
N_PROJ = COL_MERGE + 3 * D_MODEL
ORIG_GATE_W = 3 * N_HEADS

MIXER = {"sel": 0, "win": 0, "swa": 1, "moba": 2}


def _cparams(semantics):
    return pltpu.CompilerParams(dimension_semantics=semantics, vmem_limit_bytes=VMEM_LIMIT)


def _rms(x, gain):
    y = x * lax.rsqrt(jnp.mean(x * x, axis=-1, keepdims=True) + RMS_EPS)
    return y * gain


def _norm_matmul_kernel(x_ref, g_ref, w_ref, o_ref, h_scr):
    @pl.when(pl.program_id(1) == 0)
    def _():
        h_scr[...] = _rms(x_ref[...], g_ref[...]).astype(BF16)

    o_ref[...] = jnp.dot(h_scr[...], w_ref[0], preferred_element_type=F32)


def norm_matmul(x, gain, w, layer, *, tm=1024, tn=1024):
    t, k = x.shape
    n = w.shape[2]
    return pl.pallas_call(
        _norm_matmul_kernel,
        out_shape=jax.ShapeDtypeStruct((t, n), F32),
        grid=(t // tm, n // tn),
        in_specs=[pl.BlockSpec((tm, k), lambda i, j: (i, 0)),
                  pl.BlockSpec((1, k), lambda i, j: (0, 0)),
                  pl.BlockSpec((1, k, tn), lambda i, j: (layer, 0, j))],
        out_specs=pl.BlockSpec((tm, tn), lambda i, j: (i, j)),
        scratch_shapes=[pltpu.VMEM((tm, k), BF16)],
        compiler_params=_cparams(("parallel", "arbitrary")),
        name="norm_matmul",
    )(x, gain.reshape(1, k), w)


def _merge_kernel(oa_ref, ob_ref, oc_ref, wb_ref, g0_ref, g1_ref, g2_ref, z_ref, w_scr):
    @pl.when(pl.program_id(1) == 0)
    def _():
        w_scr[...] = wb_ref[0].astype(BF16)

    acc = jax.nn.sigmoid(g0_ref[...]) * jnp.dot(oa_ref[...], w_scr[0], preferred_element_type=F32)
    acc += jax.nn.sigmoid(g1_ref[...]) * jnp.dot(ob_ref[...], w_scr[1], preferred_element_type=F32)
    acc += jax.nn.sigmoid(g2_ref[...]) * jnp.dot(oc_ref[...], w_scr[2], preferred_element_type=F32)
    z_ref[...] = acc.astype(BF16)


def merge_branches(o_a, o_b, o_c, w_branch, layer, proj, *, tm=512, tn=1024):
    t = o_a.shape[0]
    gate_blk = COL_MERGE // tn
    per_branch = D_MODEL // tn
    o_spec = pl.BlockSpec((tm, MIX_WIDTH), lambda j, i: (i, 0))

    def gate_spec(m):
        return pl.BlockSpec((tm, tn), lambda j, i: (i, gate_blk + m * per_branch + j))

    return pl.pallas_call(
        _merge_kernel,
        out_shape=jax.ShapeDtypeStruct((t, D_MODEL), BF16),
        grid=(D_MODEL // tn, t // tm),
        in_specs=[o_spec, o_spec, o_spec,
                  pl.BlockSpec((1, 3, MIX_WIDTH, tn), lambda j, i: (layer, 0, 0, j)),
                  gate_spec(0), gate_spec(1), gate_spec(2)],
        out_specs=pl.BlockSpec((tm, tn), lambda j, i: (i, j)),
        scratch_shapes=[pltpu.VMEM((3, MIX_WIDTH, tn), BF16)],
        compiler_params=_cparams(("parallel", "arbitrary")),
        name="merge_branches",
    )(o_a, o_b, o_c, w_branch, proj, proj, proj)


def _matmul_res_kernel(a_ref, w_ref, x_ref, o_ref, w_scr):
    @pl.when(pl.program_id(1) == 0)
    def _():
        w_scr[...] = w_ref[0].astype(BF16)

    o_ref[...] = x_ref[...] + jnp.dot(a_ref[...], w_scr[...], preferred_element_type=F32)


def matmul_residual(a, w, layer, x, *, tm=1024, tn=1024):
    t, k = a.shape
    n = w.shape[2]
    return pl.pallas_call(
        _matmul_res_kernel,
        out_shape=jax.ShapeDtypeStruct((t, n), F32),
        grid=(n // tn, t // tm),
        in_specs=[pl.BlockSpec((tm, k), lambda j, i: (i, 0)),
                  pl.BlockSpec((1, k, tn), lambda j, i: (layer, 0, j)),
                  pl.BlockSpec((tm, tn), lambda j, i: (i, j))],
        out_specs=pl.BlockSpec((tm, tn), lambda j, i: (i, j)),
        scratch_shapes=[pltpu.VMEM((k, tn), BF16)],
        compiler_params=_cparams(("parallel", "arbitrary")),
        name="matmul_residual",
    )(a, w, x)


def _mlp_kernel(x_ref, g_ref, w1_ref, w2_ref, o_ref, h_scr):
    @pl.when(pl.program_id(1) == 0)
    def _():
        x = x_ref[...]
        h_scr[...] = _rms(x, g_ref[...]).astype(BF16)
        o_ref[...] = x

    u = jnp.dot(h_scr[...], w1_ref[0].astype(BF16), preferred_element_type=F32)
    u = jnp.square(jnp.maximum(u, 0.0)).astype(BF16)
    o_ref[...] += jnp.dot(u, w2_ref[0].astype(BF16), preferred_element_type=F32)


def mlp_block(x, gain, w1, w2, layer, *, tm=1024, tf=512):
    t, d = x.shape
    dff = w1.shape[2]
    return pl.pallas_call(
        _mlp_kernel,
        out_shape=jax.ShapeDtypeStruct((t, d), F32),
        grid=(t // tm, dff // tf),
        in_specs=[pl.BlockSpec((tm, d), lambda i, f: (i, 0)),
                  pl.BlockSpec((1, d), lambda i, f: (0, 0)),
                  pl.BlockSpec((1, d, tf), lambda i, f: (layer, 0, f)),
                  pl.BlockSpec((1, tf, d), lambda i, f: (layer, f, 0))],
        out_specs=pl.BlockSpec((tm, d), lambda i, f: (i, 0)),
        scratch_shapes=[pltpu.VMEM((tm, d), BF16)],
        compiler_params=_cparams(("parallel", "arbitrary")),
        name="mlp_block",
    )(x, gain.reshape(1, d), w1, w2)


def _final_norm_kernel(x_ref, g_ref, o_ref):
    o_ref[...] = _rms(x_ref[...], g_ref[...])


def final_norm(x, gain, *, tm=512):
    t, d = x.shape
    return pl.pallas_call(
        _final_norm_kernel,
        out_shape=jax.ShapeDtypeStruct((t, d), F32),
        grid=(t // tm,),
        in_specs=[pl.BlockSpec((tm, d), lambda i: (i, 0)),
                  pl.BlockSpec((1, d), lambda i: (0, 0))],
        out_specs=pl.BlockSpec((tm, d), lambda i: (i, 0)),
        compiler_params=_cparams(("parallel",)),
        name="final_norm",
    )(x, gain.reshape(1, d))


def _t5_bucket(n):
    log_ratio = jnp.log(jnp.maximum(n, 1).astype(F32) / BUCKET_EXACT) / math.log(BUCKET_MAX_DIST / BUCKET_EXACT)
    large = jnp.minimum(BUCKET_EXACT + (log_ratio * (N_BUCKETS - BUCKET_EXACT)).astype(jnp.int32), N_BUCKETS - 1)
    return jnp.where(n < BUCKET_EXACT, n, large)


def _lookup_bias(tbl_ref, head, n):
    bucket = _t5_bucket(n)
    out = jnp.zeros(n.shape, F32)
    for b in range(N_BUCKETS):
        out = jnp.where(bucket == b, tbl_ref[b * TOTAL_HEADS + head], out)
    return out


def _band_bias_kernel(tbl_ref, o_ref):
    l = lax.broadcasted_iota(jnp.int32, (TK, TQ), 0)
    q = lax.broadcasted_iota(jnp.int32, (TK, TQ), 1)
    head = pl.program_id(0)
    prev = _lookup_bias(tbl_ref, head, jnp.maximum(q + TK - l, 0)) * LOG2E
    diag = _lookup_bias(tbl_ref, head, jnp.maximum(q - l, 0)) * LOG2E
    sliding = (head >= MIXER["swa"] * N_HEADS) & (head < (MIXER["swa"] + 1) * N_HEADS)
    o_ref[0] = jnp.where(sliding & (q + TK - l >= SWA_WINDOW), NEG_INF, prev)
    o_ref[1] = jnp.where(l > q, NEG_INF, diag)


def band_bias(tbl_flat):
    return pl.pallas_call(
        _band_bias_kernel,
        out_shape=jax.ShapeDtypeStruct((2, TK, TOTAL_HEADS * TQ), F32),
        grid=(TOTAL_HEADS,),
        in_specs=[pl.BlockSpec(memory_space=pltpu.SMEM)],
        out_specs=pl.BlockSpec((2, TK, TQ), lambda h: (0, 0, h)),
        compiler_params=_cparams(("parallel",)),
        name="band_bias",
    )(tbl_flat)


def _cmp_bias_kernel(tbl_ref, o_ref):
    c = lax.broadcasted_iota(jnp.int32, (LANES, TQ), 0)
    t = pl.program_id(0) * TQ + lax.broadcasted_iota(jnp.int32, (LANES, TQ), 1)
    o_ref[0] = _lookup_bias(tbl_ref, pl.program_id(1), jnp.maximum(t - (c * CMP_STRIDE + CMP_LEN - 1), 0))


def cmp_bias(tbl_flat, s):
    return pl.pallas_call(
        _cmp_bias_kernel,
        out_shape=jax.ShapeDtypeStruct((s // TQ, LANES, N_HEADS * TQ), F32),
        grid=(s // TQ, N_HEADS),
        in_specs=[pl.BlockSpec(memory_space=pltpu.SMEM)],
        out_specs=pl.BlockSpec((1, LANES, TQ), lambda i, h: (i, 0, h)),
        compiler_params=_cparams(("parallel", "parallel")),
        name="cmp_bias",
    )(tbl_flat)


def _compress_kernel(x_ref, pos_ref, w1_ref, w2_ref, o_ref):
    n_half = x_ref.shape[1] // CMP_STRIDE
    half = jnp.concatenate([x_ref[0, pl.ds(l, n_half, stride=CMP_STRIDE), :] for l in range(CMP_STRIDE)], axis=1)
    half_w = CMP_STRIDE * HEAD_DIM
    xa = (half + pos_ref[0, 0:1, :]).astype(BF16)
    xb = (half + pos_ref[0, 1:2, :]).astype(BF16)
    ha = jnp.dot(xa, w1_ref[0, 0, 0:half_w, :].astype(BF16), preferred_element_type=F32)
    hb = jnp.dot(xb, w1_ref[0, 0, half_w:2 * half_w, :].astype(BF16), preferred_element_type=F32)
    hidden = ha + pltpu.roll(hb, n_half - 1, 0)
    act = jax.nn.gelu(hidden, approximate=True).astype(BF16)
    o_ref[0, 0, 0] = jnp.dot(act, w2_ref[0, 0].astype(BF16), preferred_element_type=F32)


def nsa_compress(proj, pos, w1, w2, layer):
    b, s, _ = proj.shape
    n_kv, n_half, width = N_KV, s // CMP_STRIDE, CMP_STRIDE * HEAD_DIM
    col_blk = COL_A_KC // HEAD_DIM
    return pl.pallas_call(
        _compress_kernel,
        out_shape=jax.ShapeDtypeStruct((b, 2, n_kv, n_half, HEAD_DIM), F32),
        grid=(2, b, n_kv),
        in_specs=[pl.BlockSpec((1, s, HEAD_DIM), lambda kv, bi, h: (bi, 0, col_blk + kv * n_kv + h)),
                  pl.BlockSpec((1, 2, width), lambda kv, bi, h: (kv, 0, 0)),
                  pl.BlockSpec((1, 1, 2 * width, CMP_HIDDEN), lambda kv, bi, h: (layer, kv, 0, 0)),
                  pl.BlockSpec((1, 1, CMP_HIDDEN, HEAD_DIM), lambda kv, bi, h: (layer, kv, 0, 0))],
        out_specs=pl.BlockSpec((1, 1, 1, n_half, HEAD_DIM), lambda kv, bi, h: (bi, kv, h, 0, 0)),
        compiler_params=_cparams(("parallel", "parallel", "parallel")),
        name="nsa_compress",
    )(proj, pos, w1, w2)


def _stack_heads(q):
    return jnp.concatenate([q[:, g * HEAD_DIM:(g + 1) * HEAD_DIM] for g in range(GROUP)], axis=0)


def _untranspose_heads(o_t):
    return jnp.concatenate([o_t[:, g * TQ:(g + 1) * TQ].T for g in range(GROUP)], axis=1)


def _head_row(ref, first):
    return jnp.concatenate([jnp.full((1, TQ), ref[first + g] * LOG2E, F32) for g in range(GROUP)], axis=1)


def _split3(x):
    x1 = x.astype(BF16)
    r1 = x - x1.astype(F32)
    x2 = r1.astype(BF16)
    x3 = (r1 - x2.astype(F32)).astype(BF16)
    return x1, x2, x3


def _dot_nt(a, b):
    return lax.dot_general(a, b, (((1,), (1,)), ((), ())), preferred_element_type=F32)


def _rank_select(score, n_cand, n_top):
    idx = lax.broadcasted_iota(jnp.int32, score.shape, 0)
    rank = jnp.zeros(score.shape, jnp.int32)
    for j in range(n_cand):
        other = score[j:j + 1, :]
        ahead = (other > score) | ((other == score) & (j < idx))
        rank += jnp.where(ahead, 1, 0)
    return jnp.where((rank < n_top) & (idx < n_cand), 1.0, 0.0)


def _nsa_cmp_kernel(*refs):
    def q_block(sub, carry):
        _nsa_cmp_q_block(pl.program_id(1) * Q_BLOCKS + sub, sub, *refs)
        return carry
    lax.fori_loop(0, Q_BLOCKS, q_block, 0)


def _nsa_cmp_q_block(qi, sub, bias_ref, q_ref, kc_ref, vc_ref, share_ref, o_ref, sel_ref):
    q_rows = pl.ds(pl.multiple_of(sub * TQ, TQ), TQ)
    share = share_ref[...]
    n_sel = share.shape[0]
    c = lax.broadcasted_iota(jnp.int32, (LANES, COLS), 0)
    t = qi * TQ + (lax.broadcasted_iota(jnp.int32, (LANES, COLS), 1) & (TQ - 1))
    valid = t - (c * CMP_STRIDE + CMP_LEN - 1) >= 0
    blk = lax.broadcasted_iota(jnp.int32, (n_sel, TQ), 0)
    cur = (qi * TQ + lax.broadcasted_iota(jnp.int32, (n_sel, TQ), 1)) // SEL_BLOCK
    forced = (blk == 0) | (blk == cur) | (blk == cur - 1)
    for h in range(N_KV):
        cols = slice(h * COLS, (h + 1) * COLS)
        q4 = _stack_heads(q_ref[0, q_rows, cols]).astype(BF16)
        kc = kc_ref[0, 0, h].astype(BF16)
        vc_t = vc_ref[0, 0, h].T.astype(BF16)
        logits = _dot_nt(kc, q4) * SCALE + bias_ref[sub, :, cols]
        logits = jnp.where(valid, logits, NEG_INF)
        m = jnp.max(logits, axis=0, keepdims=True)
        p = jnp.where(valid, jnp.exp(logits - m), 0.0)
        p = p / jnp.maximum(jnp.sum(p, axis=0, keepdims=True), TINY)
        o_ref[0, q_rows, cols] = _untranspose_heads(jnp.dot(vc_t, p.astype(BF16), preferred_element_type=F32))

        p_sum = p[:, 0:TQ]
        for g in range(1, GROUP):
            p_sum = p_sum + p[:, g * TQ:(g + 1) * TQ]
        importance = sum(jnp.dot(share, part, preferred_element_type=F32) for part in _split3(p_sum))
        score = jnp.where(forced, FORCE_SCORE, jnp.where(blk <= cur, importance, NEG_INF))
        sel_ref[0, h, sub] = _rank_select(score, n_sel, min(SEL_TOP_N, n_sel))


def nsa_compressed(bias_c, proj, cmp_tokens, share):
    b, s, _ = proj.shape
    n_cmp = cmp_tokens.shape[3]
    n_sel = share.shape[0]
    return pl.pallas_call(
        _nsa_cmp_kernel,
        out_shape=(jax.ShapeDtypeStruct((b, s, MIX_WIDTH), F32),
                   jax.ShapeDtypeStruct((b, N_KV, s // TQ, n_sel, TQ), F32)),
        grid=(b, s // (Q_BLOCKS * TQ)),
        in_specs=[pl.BlockSpec((Q_BLOCKS, n_cmp, N_KV * COLS), lambda bi, i: (i, 0, 0)),
                  pl.BlockSpec((1, Q_BLOCKS * TQ, MIX_WIDTH), lambda bi, i: (bi, i, COL_A_Q // MIX_WIDTH)),
                  pl.BlockSpec((1, 1, N_KV, n_cmp, HEAD_DIM), lambda bi, i: (bi, 0, 0, 0, 0)),
                  pl.BlockSpec((1, 1, N_KV, n_cmp, HEAD_DIM), lambda bi, i: (bi, 1, 0, 0, 0)),
                  pl.BlockSpec((n_sel, n_cmp), lambda bi, i: (0, 0))],
        out_specs=(pl.BlockSpec((1, Q_BLOCKS * TQ, MIX_WIDTH), lambda bi, i: (bi, i, 0)),
                   pl.BlockSpec((1, N_KV, Q_BLOCKS, n_sel, TQ), lambda bi, i: (bi, 0, i, 0, 0))),
        compiler_params=_cparams(("parallel", "parallel")),
        name="nsa_compressed",
    )(bias_c, proj, cmp_tokens, cmp_tokens, share)


def _moba_gate_kernel(q_ref, k_ref, sel_ref):
    s = q_ref.shape[1]
    n_blk = s // MOBA_BLOCK
    n = lax.broadcasted_iota(jnp.int32, (BF16_ROWS, GROUP * s), 0)
    q_blk = (lax.broadcasted_iota(jnp.int32, (BF16_ROWS, GROUP * s), 1) & (s - 1)) // MOBA_BLOCK
    for h in range(N_KV):
        k = k_ref[0, :, h * HEAD_DIM:(h + 1) * HEAD_DIM]
        k_mean = [jnp.sum(k[j * MOBA_BLOCK:(j + 1) * MOBA_BLOCK], axis=0, keepdims=True) / MOBA_BLOCK
                  for j in range(n_blk)]
        k_mean = jnp.concatenate(k_mean + [jnp.zeros((BF16_ROWS - n_blk, HEAD_DIM), F32)], axis=0)
        q4 = jnp.concatenate([q_ref[0, :, (h * GROUP + g) * HEAD_DIM:(h * GROUP + g + 1) * HEAD_DIM]
                              for g in range(GROUP)], axis=0)
        q_hi, q_lo, _ = _split3(q4)
        k_hi, k_lo, _ = _split3(k_mean)
        gate = _dot_nt(k_hi, q_hi) + _dot_nt(k_lo, q_hi) + _dot_nt(k_hi, q_lo)
        gate = jnp.where(n < q_blk, gate, NEG_INF)
        gate = jnp.where(n < n_blk, gate, BELOW_ALL)
        picked = _rank_select(gate, n_blk, min(MOBA_TOP_K, n_blk - 1))
        picked = jnp.where(n < q_blk, picked, 0.0)
        for qi in range(s // TQ):
            for g in range(GROUP):
                sel_ref[0, h, qi, :, g * TQ:(g + 1) * TQ] = picked[:, g * s + qi * TQ:g * s + (qi + 1) * TQ]


def moba_gate(proj):
    b, s, _ = proj.shape
    assert s & (s - 1) == 0
    return pl.pallas_call(
        _moba_gate_kernel,
        out_shape=jax.ShapeDtypeStruct((b, N_KV, s // TQ, BF16_ROWS, COLS), F32),
        grid=(b,),
        in_specs=[pl.BlockSpec((1, s, MIX_WIDTH), lambda bi: (bi, 0, COL_C_Q // MIX_WIDTH)),
                  pl.BlockSpec((1, s, KV_W), lambda bi: (bi, 0, COL_C_K // KV_W))],
        out_specs=pl.BlockSpec((1, N_KV, s // TQ, BF16_ROWS, COLS), lambda bi: (bi, 0, 0, 0, 0)),
        compiler_params=_cparams(("parallel",)),
        name="moba_gate",
    )(proj, proj)


def _flash_q_block(mode, qi, sub, *refs):
    n_in = 8 if mode == "win" else 6
    ins, o_ref, scr = refs[:n_in], refs[n_in], refs[n_in + 1:]
    if mode == "swa":
        c31_ref, sink_ref, band_ref, q_ref, k_ref, v_ref = ins
    elif mode == "win":
        c31_ref, band_ref, q_ref, k_ref, v_ref, cmp_ref, sel_ref, gate_ref = ins
    else:
        c31_ref, band_ref, q_ref, k_ref, v_ref, msk_ref = ins
    kb_scr, vt_scr, m_scr, l_scr, acc_scr, alpha_scr, pb_scr = (scr[i * N_KV:(i + 1) * N_KV] for i in range(7))
    pend_ref = scr[7 * N_KV]
    s_scr = scr[7 * N_KV + 1:8 * N_KV + 1]
    mb_scr = scr[8 * N_KV + 1:]
    n_kb = k_ref.shape[1] // TK
    head0 = MIXER[mode] * N_HEADS
    q_rows = pl.ds(pl.multiple_of(sub * TQ, TQ), TQ)

    @pl.when(qi == 0)
    def _():
        for h in range(N_KV):
            cols = slice(h * HEAD_DIM, (h + 1) * HEAD_DIM)
            kb_scr[h][...] = k_ref[0, :, cols].astype(BF16)
            for j in range(n_kb):
                vt_scr[h][j] = v_ref[0, j * TK:(j + 1) * TK, cols].T.astype(BF16)

    q4 = [(_stack_heads(q_ref[0, q_rows, h * COLS:(h + 1) * COLS]) * SCALE2).astype(BF16) for h in range(N_KV)]
    far_bias = [_head_row(c31_ref, head0 + h * GROUP) for h in range(N_KV)]

    def penalty(keep):
        return (keep - 1.0) * (-NEG_INF)

    def expand(rows, seg):
        return jnp.concatenate([jnp.broadcast_to(r, (seg, COLS)) for r in rows], axis=0)

    def sel_rows(h, first_blk, n_blocks):
        per = TK // SEL_BLOCK
        rows = []
        for part in range(n_blocks * per):
            r = msk_ref[0, h, sub, pl.ds(first_blk * per + part, 1), :]
            rows.append(jnp.concatenate([penalty(r)] * GROUP, axis=1))
        return rows

    def moba_row(h, key_blk):
        return msk_ref[0, h, sub, pl.ds((key_blk * TK) // MOBA_BLOCK, 1), :]

    far_lo = jnp.maximum(qi - NSA_WINDOW // TK, 0) if mode == "win" else 0
    n_far = jnp.maximum(qi - 1 - far_lo, 0)

    def far_bias_tile(h, k0):
        tail = jnp.where(k0 + 1 >= qi - 1, NEG_INF, 0.0)
        if mode == "sel":
            rows = [far_bias[h] + r for r in sel_rows(h, k0, 2)]
            per = TK // SEL_BLOCK
            return expand(rows[:per] + [r + tail for r in rows[per:]], SEL_BLOCK)
        if mode == "moba":
            row = far_bias[h] + penalty(moba_row(h, k0))
            return expand([row, row + tail], TK)
        key = lax.broadcasted_iota(jnp.int32, (TK, COLS), 0)
        qry = lax.broadcasted_iota(jnp.int32, (TK, COLS), 1) & (TQ - 1)
        edge = jnp.where(k0 == qi - NSA_WINDOW // TK, NEG_INF, 0.0)
        first = far_bias[h] + jnp.where(key <= qry, edge, 0.0)
        return jnp.concatenate([first, jnp.broadcast_to(far_bias[h] + tail, (TK, COLS))], axis=0)

    def scores(h, k0, n_blocks, bias):
        start = pl.multiple_of(k0 * TK, TK)
        return _dot_nt(kb_scr[h][pl.ds(start, n_blocks * TK), :], q4[h]) + bias

    def flush():
        k0 = pend_ref[0]
        for h in range(N_KV):
            v_t = jnp.concatenate([vt_scr[h][k0], vt_scr[h][k0 + 1]], axis=1)
            pv = jnp.dot(v_t, pb_scr[h][...], preferred_element_type=F32)
            acc_scr[h][...] = alpha_scr[h][...] * acc_scr[h][...] + pv

    def softmax(h, s, m_blk, first):
        n_keys = s.shape[0]
        if first:
            m_new = m_blk
        else:
            m_prev = m_scr[h][...]
            m_new = jnp.maximum(m_prev, m_blk)
            alpha = jnp.exp2(m_prev - m_new)
        p = jnp.exp2(s - m_new)
        p_sum = jnp.sum(p, axis=0, keepdims=True)
        pb_scr[h][0:n_keys, :] = p.astype(BF16)
        if n_keys == TK:
            pb_scr[h][TK:2 * TK, :] = jnp.zeros((TK, COLS), BF16)
        if first:
            l_scr[h][...] = p_sum
            alpha_scr[h][...] = jnp.zeros((1, COLS), F32)
            acc_scr[h][...] = jnp.zeros((HEAD_DIM, COLS), F32)
        else:
            l_scr[h][...] = alpha * l_scr[h][...] + p_sum
            alpha_scr[h][...] = alpha
        m_scr[h][...] = m_new

    def first_step(k0, n_blocks, bias):
        for h in range(N_KV):
            s = scores(h, k0, n_blocks, bias[h])
            softmax(h, s, jnp.max(s, axis=0, keepdims=True), True)
        pend_ref[0] = k0

    def produce(pair):
        k0 = far_lo + 2 * pair
        for h in range(N_KV):
            s = scores(h, k0, 2, far_bias_tile(h, k0))
            s_scr[h][...] = s
            mb_scr[h][...] = jnp.max(s, axis=0, keepdims=True)

    @pl.when(qi == 0)
    def _():
        bias = [band_ref[1, :, h * COLS:(h + 1) * COLS] for h in range(N_KV)]
        if mode == "sel":
            bias = [bias[h] + expand(sel_rows(h, 0, 1), SEL_BLOCK) for h in range(N_KV)]
        first_step(0, 1, bias)

    @pl.when(qi >= 1)
    def _():
        bias = [band_ref[:, :, h * COLS:(h + 1) * COLS].reshape(2 * TK, COLS) for h in range(N_KV)]
        if mode == "sel":
            bias = [bias[h] + expand(sel_rows(h, qi - 1, 2), SEL_BLOCK) for h in range(N_KV)]
        elif mode == "moba":
            own = (qi * TQ) // MOBA_BLOCK == ((qi - 1) * TK) // MOBA_BLOCK
            own_f = jnp.where(own, 1.0, 0.0)
            bias = [bias[h] + expand([penalty(jnp.minimum(moba_row(h, qi - 1) + own_f, 1.0)),
                                      jnp.zeros((1, COLS), F32)], TK) for h in range(N_KV)]
        first_step(qi - 1, 2, bias)
        if mode != "swa":
            produce(0)

    if mode != "swa":
        n_pairs = (n_far + 1) >> 1

        def pair_body(pair, carry):
            flush()
            for h in range(N_KV):
                softmax(h, s_scr[h][...], mb_scr[h][...], False)
            pend_ref[0] = far_lo + 2 * pair
            produce(jnp.minimum(pair + 1, n_pairs - 1))
            return carry
        lax.fori_loop(0, n_pairs, pair_body, 0)

    flush()
    for h in range(N_KV):
        m = m_scr[h][...]
        l = l_scr[h][...]
        acc = acc_scr[h][...]
        if mode == "swa":
            sink = _head_row(sink_ref, h * GROUP)
            m_fin = jnp.maximum(m, sink)
            shrink = jnp.exp2(m - m_fin)
            l = l * shrink + jnp.exp2(sink - m_fin)
            acc = acc * shrink
        out = _untranspose_heads(acc / jnp.maximum(l, TINY))
        if mode == "win":
            gate = jax.nn.sigmoid(gate_ref[0, q_rows, :])
            for g in range(GROUP):
                head = h * GROUP + g
                cols = slice(head * HEAD_DIM, (head + 1) * HEAD_DIM)
                mixed = (gate[:, 3 * head:3 * head + 1] * cmp_ref[0, q_rows, cols]
                         + gate[:, 3 * head + 1:3 * head + 2] * sel_ref[0, q_rows, cols]
                         + gate[:, 3 * head + 2:3 * head + 3] * out[:, g * HEAD_DIM:(g + 1) * HEAD_DIM])
                o_ref[0, q_rows, cols] = mixed.astype(o_ref.dtype)
        else:
            o_ref[0, q_rows, h * COLS:(h + 1) * COLS] = out.astype(o_ref.dtype)


def _flash_kernel(mode, *refs):
    def q_block(sub, carry):
        _flash_q_block(mode, pl.program_id(1) * Q_BLOCKS + sub, sub, *refs)
        return carry
    lax.fori_loop(0, Q_BLOCKS, q_block, 0)


def flash_attention(mode, proj, band, c31, *, q_col, k_col, v_col, out_dtype, mask=None, sinks=None, branches=None):
    b, s, _ = proj.shape
    smem = pl.BlockSpec(memory_space=pltpu.SMEM)
    in_specs = [smem]
    args = [c31]
    if mode == "swa":
        in_specs.append(smem)
        args.append(sinks)
    in_specs += [pl.BlockSpec((2, TK, N_KV * COLS), lambda bi, i: (0, 0, MIXER[mode])),
                 pl.BlockSpec((1, Q_BLOCKS * TQ, MIX_WIDTH), lambda bi, i: (bi, i, q_col // MIX_WIDTH)),
                 pl.BlockSpec((1, s, KV_W), lambda bi, i: (bi, 0, k_col // KV_W)),
                 pl.BlockSpec((1, s, KV_W), lambda bi, i: (bi, 0, v_col // KV_W))]
    args += [band, proj, proj, proj]
    if mode == "sel":
        in_specs.append(pl.BlockSpec((1, N_KV, Q_BLOCKS, mask.shape[3], TQ), lambda bi, i: (bi, 0, i, 0, 0)))
        args.append(mask)
    elif mode == "moba":
        in_specs.append(pl.BlockSpec((1, N_KV, Q_BLOCKS, mask.shape[3], COLS), lambda bi, i: (bi, 0, i, 0, 0)))
        args.append(mask)
    elif mode == "win":
        rows = pl.BlockSpec((1, Q_BLOCKS * TQ, MIX_WIDTH), lambda bi, i: (bi, i, 0))
        in_specs += [rows, rows, pl.BlockSpec((1, Q_BLOCKS * TQ, LANES), lambda bi, i: (bi, i, COL_A_GATE // LANES))]
        args += [branches[0], branches[1], proj]
    per_head = [pltpu.VMEM((s, HEAD_DIM), BF16), pltpu.VMEM((s // TK, HEAD_DIM, TK), BF16),
                pltpu.VMEM((1, COLS), F32), pltpu.VMEM((1, COLS), F32), pltpu.VMEM((HEAD_DIM, COLS), F32),
                pltpu.VMEM((1, COLS), F32), pltpu.VMEM((2 * TK, COLS), BF16)]
    scratch = [shape for shape in per_head for _ in range(N_KV)] + [pltpu.SMEM((1,), jnp.int32)]
    if mode != "swa":
        scratch += [pltpu.VMEM((2 * TK, COLS), F32)] * N_KV + [pltpu.VMEM((1, COLS), F32)] * N_KV
    return pl.pallas_call(
        functools.partial(_flash_kernel, mode),
        out_shape=jax.ShapeDtypeStruct((b, s, MIX_WIDTH), out_dtype),
        grid=(b, s // (Q_BLOCKS * TQ)),
        in_specs=in_specs,
        out_specs=pl.BlockSpec((1, Q_BLOCKS * TQ, MIX_WIDTH), lambda bi, i: (bi, i, 0)),
        scratch_shapes=scratch,
        compiler_params=_cparams(("parallel", "arbitrary")),
        name="flash_" + mode,
    )(*args)


def _selection_share(n_sel, n_half):
    blk = np.arange(n_sel)[:, None] * SEL_BLOCK
    starts = np.arange(n_half)[None, :] * CMP_STRIDE
    shared = np.clip(np.minimum(starts + CMP_LEN, blk + SEL_BLOCK) - np.maximum(starts, blk), 0, None)
    shared = shared / CMP_STRIDE
    shared[:, n_half - 1] = 0.0
    return shared.astype(np.float32)


REGROUP_TILE = 512


def _regroup_tables():
    q_w, kv_w = MIX_WIDTH, KV_W
    a_kv = q_w
    a_gate = a_kv + 6 * kv_w
    b_q = a_gate + ORIG_GATE_W
    b_kv = b_q + q_w
    c_q = b_kv + 2 * kv_w
    c_kv = c_q + q_w
    merge = c_kv + 2 * kv_w
    segments = [(COL_A_Q, q_w, 0), (COL_B_Q, q_w, b_q), (COL_C_Q, q_w, c_q), (COL_A_KC, 6 * kv_w, a_kv),
                (COL_B_K, 2 * kv_w, b_kv), (COL_C_K, 2 * kv_w, c_kv), (COL_A_GATE, COL_MERGE - COL_A_GATE, a_gate),
                (COL_MERGE, 3 * D_MODEL, merge)]
    start, is_gate = [], []
    for new0, width, orig0 in segments:
        for off in range(0, width, REGROUP_TILE):
            start.append(orig0 + off)
            is_gate.append(int(new0 == COL_A_GATE))
    return np.asarray(start, np.int32), np.asarray(is_gate, np.int32)


def _regroup_kernel(start_tbl, gate_tbl, wt_ref, o_ref):
    rows = wt_ref[...]
    row = lax.broadcasted_iota(jnp.int32, rows.shape, 0)
    keep = jnp.where(gate_tbl[pl.program_id(1)] == 1, ORIG_GATE_W, REGROUP_TILE)
    o_ref[0] = jnp.where(row < keep, rows, 0.0).T.astype(BF16)


def regroup_w_in(w):
    depth, d, n_in = w.shape
    start, is_gate = (jnp.asarray(t) for t in _regroup_tables())
    return pl.pallas_call(
        _regroup_kernel,
        out_shape=jax.ShapeDtypeStruct((depth, d, N_PROJ), BF16),
        grid_spec=pltpu.PrefetchScalarGridSpec(
            num_scalar_prefetch=2,
            grid=(depth, N_PROJ // REGROUP_TILE),
            in_specs=[pl.BlockSpec((None, pl.Element(REGROUP_TILE), pl.Element(d)),
                                   lambda l, j, st, gt: (l, pl.multiple_of(st[j], ORIG_GATE_W), 0))],
            out_specs=pl.BlockSpec((1, d, REGROUP_TILE), lambda l, j, st, gt: (l, 0, j))),
        compiler_params=_cparams(("parallel", "parallel")),
        name="regroup_w_in",
    )(start, is_gate, jnp.swapaxes(w, 1, 2))


def kernel(x, w_in, cmp_pos, cmp_w1, cmp_w2, swa_sinks, w_branch, w_out, w_mlp_in, w_mlp_out,
           norm_mix, norm_mlp, norm_final, rel_bias):
    b, s, d = x.shape
    depth = w_in.shape[0]
    t = b * s
    n_half = s // CMP_STRIDE
    n_sel = s // SEL_BLOCK

    tbl_flat = rel_bias.reshape(-1)
    c31 = rel_bias[N_BUCKETS - 1]
    band = band_bias(tbl_flat)
    bias_c = cmp_bias(tbl_flat, s)
    share = jnp.asarray(_selection_share(n_sel, n_half), BF16)

    w_in_cols = regroup_w_in(w_in)
    xt = x.reshape(t, d)
    for layer in range(depth):
        proj2d = norm_matmul(xt, norm_mix[layer], w_in_cols, layer)
        proj = proj2d.reshape(b, s, N_PROJ)

        cmp_tokens = nsa_compress(proj, cmp_pos[layer].reshape(2, 2, CMP_STRIDE * HEAD_DIM), cmp_w1, cmp_w2, layer)
        o_cmp, sel_mask = nsa_compressed(bias_c, proj, cmp_tokens, share)
        o_sel = flash_attention("sel", proj, band, c31, q_col=COL_A_Q, k_col=COL_A_KS, v_col=COL_A_VS,
                                out_dtype=F32, mask=sel_mask)
        o_a = flash_attention("win", proj, band, c31, q_col=COL_A_Q, k_col=COL_A_KW, v_col=COL_A_VW,
                              out_dtype=BF16, branches=(o_cmp, o_sel))

        o_b = flash_attention("swa", proj, band, c31, q_col=COL_B_Q, k_col=COL_B_K, v_col=COL_B_V,
                              out_dtype=BF16, sinks=swa_sinks[layer])
        moba_mask = moba_gate(proj)
        o_c = flash_attention("moba", proj, band, c31, q_col=COL_C_Q, k_col=COL_C_K, v_col=COL_C_V,
                              out_dtype=BF16, mask=moba_mask)

        z = merge_branches(o_a.reshape(t, MIX_WIDTH), o_b.reshape(t, MIX_WIDTH), o_c.reshape(t, MIX_WIDTH),
                           w_branch, layer, proj2d)
        xt = matmul_residual(z, w_out, layer, xt)
        xt = mlp_block(xt, norm_mlp[layer], w_mlp_in, w_mlp_out, layer)

    return final_norm(xt, norm_final).reshape(b, s, d)
```

```python
import functools
import math

import numpy as np
import jax
import jax.numpy as jnp
from jax import lax
from jax.experimental import pallas as pl
from jax.experimental.pallas import tpu as pltpu

F32 = jnp.float32
BF16 = jnp.bfloat16

D_MODEL = 2048
HEAD_DIM = 128
N_HEADS = 8
N_KV = 2
GROUP = N_HEADS // N_KV
MIX_WIDTH = N_HEADS * HEAD_DIM
CMP_LEN = 32
CMP_STRIDE = 16
CMP_HIDDEN = 2 * HEAD_DIM
SEL_BLOCK = 64
SEL_TOP_N = 16
NSA_WINDOW = 512
SWA_WINDOW = 128
MOBA_BLOCK = 256
MOBA_TOP_K = 3
N_BUCKETS = 32
BUCKET_EXACT = N_BUCKETS // 2
BUCKET_MAX_DIST = 128
TOTAL_HEADS = 3 * N_HEADS
D_FF = 4 * D_MODEL
RMS_EPS = 1e-6
NEG_INF = -1e30
FORCE_SCORE = 1e30
TINY = 1e-30
BELOW_ALL = -3e38
SCALE = HEAD_DIM ** -0.5

LANES = 128
BF16_ROWS = 16
VMEM_LIMIT = 56 * 1024 * 1024
TQ = 128
TK = 128
COLS = GROUP * TQ
Q_BLOCKS = 8
KV_W = N_KV * HEAD_DIM
LOG2E = math.log2(math.e)
SCALE2 = SCALE * LOG2E

COL_A_Q = 0
COL_B_Q = 1024
COL_C_Q = 2048
COL_A_KC = 3072
COL_A_KS = 3584
COL_A_VS = 3840
COL_A_KW = 4096
COL_A_VW = 4352
COL_B_K = 4608
COL_B_V = 4864
COL_C_K = 5120
COL_C_V = 5376
COL_A_GATE = 5632
COL_MERGE = 6144
N_PROJ = COL_MERGE + 3 * D_MODEL
ORIG_GATE_W = 3 * N_HEADS

MIXER = {"sel": 0, "win": 0, "swa": 1, "moba": 2}


def _cparams(semantics):
    return pltpu.CompilerParams(dimension_semantics=semantics, vmem_limit_bytes=VMEM_LIMIT)


def _rms(x, gain):
    y = x * lax.rsqrt(jnp.mean(x * x, axis=-1, keepdims=True) + RMS_EPS)
    return y * gain


def _norm_matmul_kernel(x_ref, g_ref, w_ref, o_ref, h_scr):
    @pl.when(pl.program_id(1) == 0)
    def _():
        h_scr[...] = _rms(x_ref[...], g_ref[...]).astype(BF16)

    o_ref[...] = jnp.dot(h_scr[...], w_ref[0], preferred_element_type=F32)


def norm_matmul(x, gain, w, layer, *, tm=1024, tn=1024):
    t, k = x.shape
    n = w.shape[2]
    return pl.pallas_call(
        _norm_matmul_kernel,
        out_shape=jax.ShapeDtypeStruct((t, n), F32),
        grid=(t // tm, n // tn),
        in_specs=[pl.BlockSpec((tm, k), lambda i, j: (i, 0)),
                  pl.BlockSpec((1, k), lambda i, j: (0, 0)),
                  pl.BlockSpec((1, k, tn), lambda i, j: (layer, 0, j))],
        out_specs=pl.BlockSpec((tm, tn), lambda i, j: (i, j)),
        scratch_shapes=[pltpu.VMEM((tm, k), BF16)],
        compiler_params=_cparams(("parallel", "arbitrary")),
        name="norm_matmul",
    )(x, gain.reshape(1, k), w)


def _merge_kernel(oa_ref, ob_ref, oc_ref, wb_ref, g0_ref, g1_ref, g2_ref, z_ref, w_scr):
    @pl.when(pl.program_id(1) == 0)
    def _():
        w_scr[...] = wb_ref[0].astype(BF16)

    acc = jax.nn.sigmoid(g0_ref[...]) * jnp.dot(oa_ref[...], w_scr[0], preferred_element_type=F32)
    acc += jax.nn.sigmoid(g1_ref[...]) * jnp.dot(ob_ref[...], w_scr[1], preferred_element_type=F32)
    acc += jax.nn.sigmoid(g2_ref[...]) * jnp.dot(oc_ref[...], w_scr[2], preferred_element_type=F32)
    z_ref[...] = acc.astype(BF16)


def merge_branches(o_a, o_b, o_c, w_branch, layer, proj, *, tm=512, tn=1024):
    t = o_a.shape[0]
    gate_blk = COL_MERGE // tn
    per_branch = D_MODEL // tn
    o_spec = pl.BlockSpec((tm, MIX_WIDTH), lambda j, i: (i, 0))

    def gate_spec(m):
        return pl.BlockSpec((tm, tn), lambda j, i: (i, gate_blk + m * per_branch + j))

    return pl.pallas_call(
        _merge_kernel,
        out_shape=jax.ShapeDtypeStruct((t, D_MODEL), BF16),
        grid=(D_MODEL // tn, t // tm),
        in_specs=[o_spec, o_spec, o_spec,
                  pl.BlockSpec((1, 3, MIX_WIDTH, tn), lambda j, i: (layer, 0, 0, j)),
                  gate_spec(0), gate_spec(1), gate_spec(2)],
        out_specs=pl.BlockSpec((tm, tn), lambda j, i: (i, j)),
        scratch_shapes=[pltpu.VMEM((3, MIX_WIDTH, tn), BF16)],
        compiler_params=_cparams(("parallel", "arbitrary")),
        name="merge_branches",
    )(o_a, o_b, o_c, w_branch, proj, proj, proj)


def _matmul_res_kernel(a_ref, w_ref, x_ref, o_ref, w_scr):
    @pl.when(pl.program_id(1) == 0)
    def _():
        w_scr[...] = w_ref[0].astype(BF16)

    o_ref[...] = x_ref[...] + jnp.dot(a_ref[...], w_scr[...], preferred_element_type=F32)


def matmul_residual(a, w, layer, x, *, tm=1024, tn=1024):
    t, k = a.shape
    n = w.shape[2]
    return pl.pallas_call(
        _matmul_res_kernel,
        out_shape=jax.ShapeDtypeStruct((t, n), F32),
        grid=(n // tn, t // tm),
        in_specs=[pl.BlockSpec((tm, k), lambda j, i: (i, 0)),
                  pl.BlockSpec((1, k, tn), lambda j, i: (layer, 0, j)),
                  pl.BlockSpec((tm, tn), lambda j, i: (i, j))],
        out_specs=pl.BlockSpec((tm, tn), lambda j, i: (i, j)),
        scratch_shapes=[pltpu.VMEM((k, tn), BF16)],
        compiler_params=_cparams(("parallel", "arbitrary")),
        name="matmul_residual",
    )(a, w, x)


def _mlp_kernel(x_ref, g_ref, w1_ref, w2_ref, o_ref, h_scr):
    @pl.when(pl.program_id(1) == 0)
    def _():
        x = x_ref[...]
        h_scr[...] = _rms(x, g_ref[...]).astype(BF16)
        o_ref[...] = x

    u = jnp.dot(h_scr[...], w1_ref[0].astype(BF16), preferred_element_type=F32)
    u = jnp.square(jnp.maximum(u, 0.0)).astype(BF16)
    o_ref[...] += jnp.dot(u, w2_ref[0].astype(BF16), preferred_element_type=F32)


def mlp_block(x, gain, w1, w2, layer, *, tm=1024, tf=512):
    t, d = x.shape
    dff = w1.shape[2]
    return pl.pallas_call(
        _mlp_kernel,
        out_shape=jax.ShapeDtypeStruct((t, d), F32),
        grid=(t // tm, dff // tf),
        in_specs=[pl.BlockSpec((tm, d), lambda i, f: (i, 0)),
                  pl.BlockSpec((1, d), lambda i, f: (0, 0)),
                  pl.BlockSpec((1, d, tf), lambda i, f: (layer, 0, f)),
                  pl.BlockSpec((1, tf, d), lambda i, f: (layer, f, 0))],
        out_specs=pl.BlockSpec((tm, d), lambda i, f: (i, 0)),
        scratch_shapes=[pltpu.VMEM((tm, d), BF16)],
        compiler_params=_cparams(("parallel", "arbitrary")),
        name="mlp_block",
    )(x, gain.reshape(1, d), w1, w2)


def _final_norm_kernel(x_ref, g_ref, o_ref):
    o_ref[...] = _rms(x_ref[...], g_ref[...])


def final_norm(x, gain, *, tm=512):
    t, d = x.shape
    return pl.pallas_call(
        _final_norm_kernel,
        out_shape=jax.ShapeDtypeStruct((t, d), F32),
        grid=(t // tm,),
        in_specs=[pl.BlockSpec((tm, d), lambda i: (i, 0)),
                  pl.BlockSpec((1, d), lambda i: (0, 0))],
        out_specs=pl.BlockSpec((tm, d), lambda i: (i, 0)),
        compiler_params=_cparams(("parallel",)),
        name="final_norm",
    )(x, gain.reshape(1, d))


def _t5_bucket(n):
    log_ratio = jnp.log(jnp.maximum(n, 1).astype(F32) / BUCKET_EXACT) / math.log(BUCKET_MAX_DIST / BUCKET_EXACT)
    large = jnp.minimum(BUCKET_EXACT + (log_ratio * (N_BUCKETS - BUCKET_EXACT)).astype(jnp.int32), N_BUCKETS - 1)
    return jnp.where(n < BUCKET_EXACT, n, large)


def _lookup_bias(tbl_ref, head, n):
    bucket = _t5_bucket(n)
    out = jnp.zeros(n.shape, F32)
    for b in range(N_BUCKETS):
        out = jnp.where(bucket == b, tbl_ref[b * TOTAL_HEADS + head], out)
    return out


def _band_bias_kernel(tbl_ref, o_ref):
    l = lax.broadcasted_iota(jnp.int32, (TK, TQ), 0)
    q = lax.broadcasted_iota(jnp.int32, (TK, TQ), 1)
    head = pl.program_id(0)
    prev = _lookup_bias(tbl_ref, head, jnp.maximum(q + TK - l, 0)) * LOG2E
    diag = _lookup_bias(tbl_ref, head, jnp.maximum(q - l, 0)) * LOG2E
    sliding = (head >= MIXER["swa"] * N_HEADS) & (head < (MIXER["swa"] + 1) * N_HEADS)
    o_ref[0] = jnp.where(sliding & (q + TK - l >= SWA_WINDOW), NEG_INF, prev)
    o_ref[1] = jnp.where(l > q, NEG_INF, diag)


def band_bias(tbl_flat):
    return pl.pallas_call(
        _band_bias_kernel,
        out_shape=jax.ShapeDtypeStruct((2, TK, TOTAL_HEADS * TQ), F32),
        grid=(TOTAL_HEADS,),
        in_specs=[pl.BlockSpec(memory_space=pltpu.SMEM)],
        out_specs=pl.BlockSpec((2, TK, TQ), lambda h: (0, 0, h)),
        compiler_params=_cparams(("parallel",)),
        name="band_bias",
    )(tbl_flat)


def _cmp_bias_kernel(tbl_ref, o_ref):
    c = lax.broadcasted_iota(jnp.int32, (LANES, TQ), 0)
    t = pl.program_id(0) * TQ + lax.broadcasted_iota(jnp.int32, (LANES, TQ), 1)
    o_ref[0] = _lookup_bias(tbl_ref, pl.program_id(1), jnp.maximum(t - (c * CMP_STRIDE + CMP_LEN - 1), 0))


def cmp_bias(tbl_flat, s):
    return pl.pallas_call(
        _cmp_bias_kernel,
        out_shape=jax.ShapeDtypeStruct((s // TQ, LANES, N_HEADS * TQ), F32),
        grid=(s // TQ, N_HEADS),
        in_specs=[pl.BlockSpec(memory_space=pltpu.SMEM)],
        out_specs=pl.BlockSpec((1, LANES, TQ), lambda i, h: (i, 0, h)),
        compiler_params=_cparams(("parallel", "parallel")),
        name="cmp_bias",
    )(tbl_flat)


def _compress_kernel(x_ref, pos_ref, w1_ref, w2_ref, o_ref):
    n_half = x_ref.shape[1] // CMP_STRIDE
    half = jnp.concatenate([x_ref[0, pl.ds(l, n_half, stride=CMP_STRIDE), :] for l in range(CMP_STRIDE)], axis=1)
    half_w = CMP_STRIDE * HEAD_DIM
    xa = (half + pos_ref[0, 0:1, :]).astype(BF16)
    xb = (half + pos_ref[0, 1:2, :]).astype(BF16)
    ha = jnp.dot(xa, w1_ref[0, 0, 0:half_w, :].astype(BF16), preferred_element_type=F32)
    hb = jnp.dot(xb, w1_ref[0, 0, half_w:2 * half_w, :].astype(BF16), preferred_element_type=F32)
    hidden = ha + pltpu.roll(hb, n_half - 1, 0)
    act = jax.nn.gelu(hidden, approximate=True).astype(BF16)
    o_ref[0, 0, 0] = jnp.dot(act, w2_ref[0, 0].astype(BF16), preferred_element_type=F32)


def nsa_compress(proj, pos, w1, w2, layer):
    b, s, _ = proj.shape
    n_kv, n_half, width = N_KV, s // CMP_STRIDE, CMP_STRIDE * HEAD_DIM
    col_blk = COL_A_KC // HEAD_DIM
    return pl.pallas_call(
        _compress_kernel,
        out_shape=jax.ShapeDtypeStruct((b, 2, n_kv, n_half, HEAD_DIM), F32),
        grid=(2, b, n_kv),
        in_specs=[pl.BlockSpec((1, s, HEAD_DIM), lambda kv, bi, h: (bi, 0, col_blk + kv * n_kv + h)),
                  pl.BlockSpec((1, 2, width), lambda kv, bi, h: (kv, 0, 0)),
                  pl.BlockSpec((1, 1, 2 * width, CMP_HIDDEN), lambda kv, bi, h: (layer, kv, 0, 0)),
                  pl.BlockSpec((1, 1, CMP_HIDDEN, HEAD_DIM), lambda kv, bi, h: (layer, kv, 0, 0))],
        out_specs=pl.BlockSpec((1, 1, 1, n_half, HEAD_DIM), lambda kv, bi, h: (bi, kv, h, 0, 0)),
        compiler_params=_cparams(("parallel", "parallel", "parallel")),
        name="nsa_compress",
    )(proj, pos, w1, w2)


def _stack_heads(q):
    return jnp.concatenate([q[:, g * HEAD_DIM:(g + 1) * HEAD_DIM] for g in range(GROUP)], axis=0)


def _untranspose_heads(o_t):
    return jnp.concatenate([o_t[:, g * TQ:(g + 1) * TQ].T for g in range(GROUP)], axis=1)


def _head_row(ref, first):
    return jnp.concatenate([jnp.full((1, TQ), ref[first + g] * LOG2E, F32) for g in range(GROUP)], axis=1)


def _split3(x):
    x1 = x.astype(BF16)
    r1 = x - x1.astype(F32)
    x2 = r1.astype(BF16)
    x3 = (r1 - x2.astype(F32)).astype(BF16)
    return x1, x2, x3


def _dot_nt(a, b):
    return lax.dot_general(a, b, (((1,), (1,)), ((), ())), preferred_element_type=F32)


def _rank_select(score, n_cand, n_top):
    idx = lax.broadcasted_iota(jnp.int32, score.shape, 0)
    rank = jnp.zeros(score.shape, jnp.int32)
    for j in range(n_cand):
        other = score[j:j + 1, :]
        ahead = (other > score) | ((other == score) & (j < idx))
        rank += jnp.where(ahead, 1, 0)
    return jnp.where((rank < n_top) & (idx < n_cand), 1.0, 0.0)


def _nsa_cmp_kernel(*refs):
    def q_block(sub, carry):
        _nsa_cmp_q_block(pl.program_id(1) * Q_BLOCKS + sub, sub, *refs)
        return carry
    lax.fori_loop(0, Q_BLOCKS, q_block, 0)


def _nsa_cmp_q_block(qi, sub, bias_ref, q_ref, kc_ref, vc_ref, share_ref, o_ref, sel_ref):
    q_rows = pl.ds(pl.multiple_of(sub * TQ, TQ), TQ)
    share = share_ref[...]
    n_sel = share.shape[0]
    c = lax.broadcasted_iota(jnp.int32, (LANES, COLS), 0)
    t = qi * TQ + (lax.broadcasted_iota(jnp.int32, (LANES, COLS), 1) & (TQ - 1))
    valid = t - (c * CMP_STRIDE + CMP_LEN - 1) >= 0
    blk = lax.broadcasted_iota(jnp.int32, (n_sel, TQ), 0)
    cur = (qi * TQ + lax.broadcasted_iota(jnp.int32, (n_sel, TQ), 1)) // SEL_BLOCK
    forced = (blk == 0) | (blk == cur) | (blk == cur - 1)
    for h in range(N_KV):
        cols = slice(h * COLS, (h + 1) * COLS)
        q4 = _stack_heads(q_ref[0, q_rows, cols]).astype(BF16)
        kc = kc_ref[0, 0, h].astype(BF16)
        vc_t = vc_ref[0, 0, h].T.astype(BF16)
        logits = _dot_nt(kc, q4) * SCALE + bias_ref[sub, :, cols]
        logits = jnp.where(valid, logits, NEG_INF)
        m = jnp.max(logits, axis=0, keepdims=True)
        p = jnp.where(valid, jnp.exp(logits - m), 0.0)
        p = p / jnp.maximum(jnp.sum(p, axis=0, keepdims=True), TINY)
        o_ref[0, q_rows, cols] = _untranspose_heads(jnp.dot(vc_t, p.astype(BF16), preferred_element_type=F32))

        p_sum = p[:, 0:TQ]
        for g in range(1, GROUP):
            p_sum = p_sum + p[:, g * TQ:(g + 1) * TQ]
        importance = sum(jnp.dot(share, part, preferred_element_type=F32) for part in _split3(p_sum))
        score = jnp.where(forced, FORCE_SCORE, jnp.where(blk <= cur, importance, NEG_INF))
        sel_ref[0, h, sub] = _rank_select(score, n_sel, min(SEL_TOP_N, n_sel))


def nsa_compressed(bias_c, proj, cmp_tokens, share):
    b, s, _ = proj.shape
    n_cmp = cmp_tokens.shape[3]
    n_sel = share.shape[0]
    return pl.pallas_call(
        _nsa_cmp_kernel,
        out_shape=(jax.ShapeDtypeStruct((b, s, MIX_WIDTH), F32),
                   jax.ShapeDtypeStruct((b, N_KV, s // TQ, n_sel, TQ), F32)),
        grid=(b, s // (Q_BLOCKS * TQ)),
        in_specs=[pl.BlockSpec((Q_BLOCKS, n_cmp, N_KV * COLS), lambda bi, i: (i, 0, 0)),
                  pl.BlockSpec((1, Q_BLOCKS * TQ, MIX_WIDTH), lambda bi, i: (bi, i, COL_A_Q // MIX_WIDTH)),
                  pl.BlockSpec((1, 1, N_KV, n_cmp, HEAD_DIM), lambda bi, i: (bi, 0, 0, 0, 0)),
                  pl.BlockSpec((1, 1, N_KV, n_cmp, HEAD_DIM), lambda bi, i: (bi, 1, 0, 0, 0)),
                  pl.BlockSpec((n_sel, n_cmp), lambda bi, i: (0, 0))],
        out_specs=(pl.BlockSpec((1, Q_BLOCKS * TQ, MIX_WIDTH), lambda bi, i: (bi, i, 0)),
                   pl.BlockSpec((1, N_KV, Q_BLOCKS, n_sel, TQ), lambda bi, i: (bi, 0, i, 0, 0))),
        compiler_params=_cparams(("parallel", "parallel")),
        name="nsa_compressed",
    )(bias_c, proj, cmp_tokens, cmp_tokens, share)


def _moba_gate_kernel(q_ref, k_ref, sel_ref):
    s = q_ref.shape[1]
    n_blk = s // MOBA_BLOCK
    n = lax.broadcasted_iota(jnp.int32, (BF16_ROWS, GROUP * s), 0)
    q_blk = (lax.broadcasted_iota(jnp.int32, (BF16_ROWS, GROUP * s), 1) & (s - 1)) // MOBA_BLOCK
    for h in range(N_KV):
        k = k_ref[0, :, h * HEAD_DIM:(h + 1) * HEAD_DIM]
        k_mean = [jnp.sum(k[j * MOBA_BLOCK:(j + 1) * MOBA_BLOCK], axis=0, keepdims=True) / MOBA_BLOCK
                  for j in range(n_blk)]
        k_mean = jnp.concatenate(k_mean + [jnp.zeros((BF16_ROWS - n_blk, HEAD_DIM), F32)], axis=0)
        q4 = jnp.concatenate([q_ref[0, :, (h * GROUP + g) * HEAD_DIM:(h * GROUP + g + 1) * HEAD_DIM]
                              for g in range(GROUP)], axis=0)
        q_hi, q_lo, _ = _split3(q4)
        k_hi, k_lo, _ = _split3(k_mean)
        gate = _dot_nt(k_hi, q_hi) + _dot_nt(k_lo, q_hi) + _dot_nt(k_hi, q_lo)
        gate = jnp.where(n < q_blk, gate, NEG_INF)
        gate = jnp.where(n < n_blk, gate, BELOW_ALL)
        picked = _rank_select(gate, n_blk, min(MOBA_TOP_K, n_blk - 1))
        picked = jnp.where(n < q_blk, picked, 0.0)
        for qi in range(s // TQ):
            for g in range(GROUP):
                sel_ref[0, h, qi, :, g * TQ:(g + 1) * TQ] = picked[:, g * s + qi * TQ:g * s + (qi + 1) * TQ]


def moba_gate(proj):
    b, s, _ = proj.shape
    assert s & (s - 1) == 0
    return pl.pallas_call(
        _moba_gate_kernel,
        out_shape=jax.ShapeDtypeStruct((b, N_KV, s // TQ, BF16_ROWS, COLS), F32),
        grid=(b,),
        in_specs=[pl.BlockSpec((1, s, MIX_WIDTH), lambda bi: (bi, 0, COL_C_Q // MIX_WIDTH)),
                  pl.BlockSpec((1, s, KV_W), lambda bi: (bi, 0, COL_C_K // KV_W))],
        out_specs=pl.BlockSpec((1, N_KV, s // TQ, BF16_ROWS, COLS), lambda bi: (bi, 0, 0, 0, 0)),
        compiler_params=_cparams(("parallel",)),
        name="moba_gate",
    )(proj, proj)


def _flash_q_block(mode, qi, sub, *refs):
    n_in = 8 if mode == "win" else 6
    ins, o_ref, scr = refs[:n_in], refs[n_in], refs[n_in + 1:]
    if mode == "swa":
        c31_ref, sink_ref, band_ref, q_ref, k_ref, v_ref = ins
    elif mode == "win":
        c31_ref, band_ref, q_ref, k_ref, v_ref, cmp_ref, sel_ref, gate_ref = ins
    else:
        c31_ref, band_ref, q_ref, k_ref, v_ref, msk_ref = ins
    kb_scr, vt_scr, m_scr, l_scr, acc_scr, alpha_scr, pb_scr = (scr[i * N_KV:(i + 1) * N_KV] for i in range(7))
    pend_ref = scr[7 * N_KV]
    s_scr = scr[7 * N_KV + 1:8 * N_KV + 1]
    mb_scr = scr[8 * N_KV + 1:]
    n_kb = k_ref.shape[1] // TK
    head0 = MIXER[mode] * N_HEADS
    q_rows = pl.ds(pl.multiple_of(sub * TQ, TQ), TQ)

    @pl.when(qi == 0)
    def _():
        for h in range(N_KV):
            cols = slice(h * HEAD_DIM, (h + 1) * HEAD_DIM)
            kb_scr[h][...] = k_ref[0, :, cols].astype(BF16)
            for j in range(n_kb):
                vt_scr[h][j] = v_ref[0, j * TK:(j + 1) * TK, cols].T.astype(BF16)

    q4 = [(_stack_heads(q_ref[0, q_rows, h * COLS:(h + 1) * COLS]) * SCALE2).astype(BF16) for h in range(N_KV)]
    far_bias = [_head_row(c31_ref, head0 + h * GROUP) for h in range(N_KV)]

    def penalty(keep):
        return (keep - 1.0) * (-NEG_INF)

    def expand(rows, seg):
        return jnp.concatenate([jnp.broadcast_to(r, (seg, COLS)) for r in rows], axis=0)

    def sel_rows(h, first_blk, n_blocks):
        per = TK // SEL_BLOCK
        rows = []
        for part in range(n_blocks * per):
            r = msk_ref[0, h, sub, pl.ds(first_blk * per + part, 1), :]
            rows.append(jnp.concatenate([penalty(r)] * GROUP, axis=1))
        return rows

    def moba_row(h, key_blk):
        return msk_ref[0, h, sub, pl.ds((key_blk * TK) // MOBA_BLOCK, 1), :]

    far_lo = jnp.maximum(qi - NSA_WINDOW // TK, 0) if mode == "win" else 0
    n_far = jnp.maximum(qi - 1 - far_lo, 0)

    def far_bias_tile(h, k0):
        tail = jnp.where(k0 + 1 >= qi - 1, NEG_INF, 0.0)
        if mode == "sel":
            rows = [far_bias[h] + r for r in sel_rows(h, k0, 2)]
            per = TK // SEL_BLOCK
            return expand(rows[:per] + [r + tail for r in rows[per:]], SEL_BLOCK)
        if mode == "moba":
            row = far_bias[h] + penalty(moba_row(h, k0))
            return expand([row, row + tail], TK)
        key = lax.broadcasted_iota(jnp.int32, (TK, COLS), 0)
        qry = lax.broadcasted_iota(jnp.int32, (TK, COLS), 1) & (TQ - 1)
        edge = jnp.where(k0 == qi - NSA_WINDOW // TK, NEG_INF, 0.0)
        first = far_bias[h] + jnp.where(key <= qry, edge, 0.0)
        return jnp.concatenate([first, jnp.broadcast_to(far_bias[h] + tail, (TK, COLS))], axis=0)

    def scores(h, k0, n_blocks, bias):
        start = pl.multiple_of(k0 * TK, TK)
        return _dot_nt(kb_scr[h][pl.ds(start, n_blocks * TK), :], q4[h]) + bias

    def flush():
        k0 = pend_ref[0]
        for h in range(N_KV):
            v_t = jnp.concatenate([vt_scr[h][k0], vt_scr[h][k0 + 1]], axis=1)
            pv = jnp.dot(v_t, pb_scr[h][...], preferred_element_type=F32)
            acc_scr[h][...] = alpha_scr[h][...] * acc_scr[h][...] + pv

    def softmax(h, s, m_blk, first):
        n_keys = s.shape[0]
        if first:
            m_new = m_blk
        else:
            m_prev = m_scr[h][...]
            m_new = jnp.maximum(m_prev, m_blk)
            alpha = jnp.exp2(m_prev - m_new)
        p = jnp.exp2(s - m_new)
        p_sum = jnp.sum(p, axis=0, keepdims=True)
        pb_scr[h][0:n_keys, :] = p.astype(BF16)
        if n_keys == TK:
            pb_scr[h][TK:2 * TK, :] = jnp.zeros((TK, COLS), BF16)
        if first:
            l_scr[h][...] = p_sum
            alpha_scr[h][...] = jnp.zeros((1, COLS), F32)
            acc_scr[h][...] = jnp.zeros((HEAD_DIM, COLS), F32)
        else:
            l_scr[h][...] = alpha * l_scr[h][...] + p_sum
            alpha_scr[h][...] = alpha
        m_scr[h][...] = m_new

    def first_step(k0, n_blocks, bias):
        for h in range(N_KV):
            s = scores(h, k0, n_blocks, bias[h])
            softmax(h, s, jnp.max(s, axis=0, keepdims=True), True)
        pend_ref[0] = k0

    def produce(pair):
        k0 = far_lo + 2 * pair
        for h in range(N_KV):
            s = scores(h, k0, 2, far_bias_tile(h, k0))
            s_scr[h][...] = s
            mb_scr[h][...] = jnp.max(s, axis=0, keepdims=True)

    @pl.when(qi == 0)
    def _():
        bias = [band_ref[1, :, h * COLS:(h + 1) * COLS] for h in range(N_KV)]
        if mode == "sel":
            bias = [bias[h] + expand(sel_rows(h, 0, 1), SEL_BLOCK) for h in range(N_KV)]
        first_step(0, 1, bias)

    @pl.when(qi >= 1)
    def _():
        bias = [band_ref[:, :, h * COLS:(h + 1) * COLS].reshape(2 * TK, COLS) for h in range(N_KV)]
        if mode == "sel":
            bias = [bias[h] + expand(sel_rows(h, qi - 1, 2), SEL_BLOCK) for h in range(N_KV)]
        elif mode == "moba":
            own = (qi * TQ) // MOBA_BLOCK == ((qi - 1) * TK) // MOBA_BLOCK
            own_f = jnp.where(own, 1.0, 0.0)
            bias = [bias[h] + expand([penalty(jnp.minimum(moba_row(h, qi - 1) + own_f, 1.0)),
                                      jnp.zeros((1, COLS), F32)], TK) for h in range(N_KV)]
        first_step(qi - 1, 2, bias)
        if mode != "swa":
            produce(0)

    if mode != "swa":
        n_pairs = (n_far + 1) >> 1

        def pair_body(pair, carry):
            flush()
            for h in range(N_KV):
                softmax(h, s_scr[h][...], mb_scr[h][...], False)
            pend_ref[0] = far_lo + 2 * pair
            produce(jnp.minimum(pair + 1, n_pairs - 1))
            return carry
        lax.fori_loop(0, n_pairs, pair_body, 0)

    flush()
    for h in range(N_KV):
        m = m_scr[h][...]
        l = l_scr[h][...]
        acc = acc_scr[h][...]
        if mode == "swa":
            sink = _head_row(sink_ref, h * GROUP)
            m_fin = jnp.maximum(m, sink)
            shrink = jnp.exp2(m - m_fin)
            l = l * shrink + jnp.exp2(sink - m_fin)
            acc = acc * shrink
        out = _untranspose_heads(acc / jnp.maximum(l, TINY))
        if mode == "win":
            gate = jax.nn.sigmoid(gate_ref[0, q_rows, :])
            for g in range(GROUP):
                head = h * GROUP + g
                cols = slice(head * HEAD_DIM, (head + 1) * HEAD_DIM)
                mixed = (gate[:, 3 * head:3 * head + 1] * cmp_ref[0, q_rows, cols]
                         + gate[:, 3 * head + 1:3 * head + 2] * sel_ref[0, q_rows, cols]
                         + gate[:, 3 * head + 2:3 * head + 3] * out[:, g * HEAD_DIM:(g + 1) * HEAD_DIM])
                o_ref[0, q_rows, cols] = mixed.astype(o_ref.dtype)
        else:
            o_ref[0, q_rows, h * COLS:(h + 1) * COLS] = out.astype(o_ref.dtype)


def _flash_kernel(mode, *refs):
    def q_block(sub, carry):
        _flash_q_block(mode, pl.program_id(1) * Q_BLOCKS + sub, sub, *refs)
        return carry
    lax.fori_loop(0, Q_BLOCKS, q_block, 0)


def flash_attention(mode, proj, band, c31, *, q_col, k_col, v_col, out_dtype, mask=None, sinks=None, branches=None):
    b, s, _ = proj.shape
    smem = pl.BlockSpec(memory_space=pltpu.SMEM)
    in_specs = [smem]
    args = [c31]
    if mode == "swa":
        in_specs.append(smem)
        args.append(sinks)
    in_specs += [pl.BlockSpec((2, TK, N_KV * COLS), lambda bi, i: (0, 0, MIXER[mode])),
                 pl.BlockSpec((1, Q_BLOCKS * TQ, MIX_WIDTH), lambda bi, i: (bi, i, q_col // MIX_WIDTH)),
                 pl.BlockSpec((1, s, KV_W), lambda bi, i: (bi, 0, k_col // KV_W)),
                 pl.BlockSpec((1, s, KV_W), lambda bi, i: (bi, 0, v_col // KV_W))]
    args += [band, proj, proj, proj]
    if mode == "sel":
        in_specs.append(pl.BlockSpec((1, N_KV, Q_BLOCKS, mask.shape[3], TQ), lambda bi, i: (bi, 0, i, 0, 0)))
        args.append(mask)
    elif mode == "moba":
        in_specs.append(pl.BlockSpec((1, N_KV, Q_BLOCKS, mask.shape[3], COLS), lambda bi, i: (bi, 0, i, 0, 0)))
        args.append(mask)
    elif mode == "win":
        rows = pl.BlockSpec((1, Q_BLOCKS * TQ, MIX_WIDTH), lambda bi, i: (bi, i, 0))
        in_specs += [rows, rows, pl.BlockSpec((1, Q_BLOCKS * TQ, LANES), lambda bi, i: (bi, i, COL_A_GATE // LANES))]
        args += [branches[0], branches[1], proj]
    per_head = [pltpu.VMEM((s, HEAD_DIM), BF16), pltpu.VMEM((s // TK, HEAD_DIM, TK), BF16),
                pltpu.VMEM((1, COLS), F32), pltpu.VMEM((1, COLS), F32), pltpu.VMEM((HEAD_DIM, COLS), F32),
                pltpu.VMEM((1, COLS), F32), pltpu.VMEM((2 * TK, COLS), BF16)]
    scratch = [shape for shape in per_head for _ in range(N_KV)] + [pltpu.SMEM((1,), jnp.int32)]
    if mode != "swa":
        scratch += [pltpu.VMEM((2 * TK, COLS), F32)] * N_KV + [pltpu.VMEM((1, COLS), F32)] * N_KV
    return pl.pallas_call(
        functools.partial(_flash_kernel, mode),
        out_shape=jax.ShapeDtypeStruct((b, s, MIX_WIDTH), out_dtype),
        grid=(b, s // (Q_BLOCKS * TQ)),
        in_specs=in_specs,
        out_specs=pl.BlockSpec((1, Q_BLOCKS * TQ, MIX_WIDTH), lambda bi, i: (bi, i, 0)),
        scratch_shapes=scratch,
        compiler_params=_cparams(("parallel", "arbitrary")),
        name="flash_" + mode,
    )(*args)


def _selection_share(n_sel, n_half):
    blk = np.arange(n_sel)[:, None] * SEL_BLOCK
    starts = np.arange(n_half)[None, :] * CMP_STRIDE
    shared = np.clip(np.minimum(starts + CMP_LEN, blk + SEL_BLOCK) - np.maximum(starts, blk), 0, None)
    shared = shared / CMP_STRIDE
    shared[:, n_half - 1] = 0.0
    return shared.astype(np.float32)


REGROUP_TILE = 512


def _regroup_tables():
    q_w, kv_w = MIX_WIDTH, KV_W
    a_kv = q_w
    a_gate = a_kv + 6 * kv_w
    b_q = a_gate + ORIG_GATE_W
    b_kv = b_q + q_w
    c_q = b_kv + 2 * kv_w
    c_kv = c_q + q_w
    merge = c_kv + 2 * kv_w
    segments = [(COL_A_Q, q_w, 0), (COL_B_Q, q_w, b_q), (COL_C_Q, q_w, c_q), (COL_A_KC, 6 * kv_w, a_kv),
                (COL_B_K, 2 * kv_w, b_kv), (COL_C_K, 2 * kv_w, c_kv), (COL_A_GATE, COL_MERGE - COL_A_GATE, a_gate),
                (COL_MERGE, 3 * D_MODEL, merge)]
    start, is_gate = [], []
    for new0, width, orig0 in segments:
        for off in range(0, width, REGROUP_TILE):
            start.append(orig0 + off)
            is_gate.append(int(new0 == COL_A_GATE))
    return np.asarray(start, np.int32), np.asarray(is_gate, np.int32)


def _regroup_kernel(start_tbl, gate_tbl, wt_ref, o_ref):
    rows = wt_ref[...]
    row = lax.broadcasted_iota(jnp.int32, rows.shape, 0)
    keep = jnp.where(gate_tbl[pl.program_id(1)] == 1, ORIG_GATE_W, REGROUP_TILE)
    o_ref[0] = jnp.where(row < keep, rows, 0.0).T.astype(BF16)


def regroup_w_in(w):
    depth, d, n_in = w.shape
    start, is_gate = (jnp.asarray(t) for t in _regroup_tables())
    return pl.pallas_call(
        _regroup_kernel,
        out_shape=jax.ShapeDtypeStruct((depth, d, N_PROJ), BF16),
        grid_spec=pltpu.PrefetchScalarGridSpec(
            num_scalar_prefetch=2,
            grid=(depth, N_PROJ // REGROUP_TILE),
            in_specs=[pl.BlockSpec((None, pl.Element(REGROUP_TILE), pl.Element(d)),
                                   lambda l, j, st, gt: (l, pl.multiple_of(st[j], ORIG_GATE_W), 0))],
            out_specs=pl.BlockSpec((1, d, REGROUP_TILE), lambda l, j, st, gt: (l, 0, j))),
        compiler_params=_cparams(("parallel", "parallel")),
        name="regroup_w_in",
    )(start, is_gate, jnp.swapaxes(w, 1, 2))


def kernel(x, w_in, cmp_pos, cmp_w1, cmp_w2, swa_sinks, w_branch, w_out, w_mlp_in, w_mlp_out,
           norm_mix, norm_mlp, norm_final, rel_bias):
    b, s, d = x.shape
    depth = w_in.shape[0]
    t = b * s
    n_half = s // CMP_STRIDE
    n_sel = s // SEL_BLOCK

    tbl_flat = rel_bias.reshape(-1)
    c31 = rel_bias[N_BUCKETS - 1]
    band = band_bias(tbl_flat)
    bias_c = cmp_bias(tbl_flat, s)
    share = jnp.asarray(_selection_share(n_sel, n_half), BF16)

    w_in_cols = regroup_w_in(w_in)
    xt = x.reshape(t, d)
    for layer in range(depth):
        proj2d = norm_matmul(xt, norm_mix[layer], w_in_cols, layer)
        proj = proj2d.reshape(b, s, N_PROJ)

        cmp_tokens = nsa_compress(proj, cmp_pos[layer].reshape(2, 2, CMP_STRIDE * HEAD_DIM), cmp_w1, cmp_w2, layer)
        o_cmp, sel_mask = nsa_compressed(bias_c, proj, cmp_tokens, share)
        o_sel = flash_attention("sel", proj, band, c31, q_col=COL_A_Q, k_col=COL_A_KS, v_col=COL_A_VS,
                                out_dtype=F32, mask=sel_mask)
        o_a = flash_attention("win", proj, band, c31, q_col=COL_A_Q, k_col=COL_A_KW, v_col=COL_A_VW,
                              out_dtype=BF16, branches=(o_cmp, o_sel))

        o_b = flash_attention("swa", proj, band, c31, q_col=COL_B_Q, k_col=COL_B_K, v_col=COL_B_V,
                              out_dtype=BF16, sinks=swa_sinks[layer])
        moba_mask = moba_gate(proj)
        o_c = flash_attention("moba", proj, band, c31, q_col=COL_C_Q, k_col=COL_C_K, v_col=COL_C_V,
                              out_dtype=BF16, mask=moba_mask)

        z = merge_branches(o_a.reshape(t, MIX_WIDTH), o_b.reshape(t, MIX_WIDTH), o_c.reshape(t, MIX_WIDTH),
                           w_branch, layer, proj2d)
        xt = matmul_residual(z, w_out, layer, xt)
        xt = mlp_block(xt, norm_mlp[layer], w_mlp_in, w_mlp_out, layer)

    return final_norm(xt, norm_final).reshape(b, s, d)
```
